```python
import math
import jax, jax.numpy as jnp
from jax import lax
import numpy as np

D_MODEL = 4096
BATCH = 8
SEQ = 2048
DEPTH = 2

D_MIX = D_MODEL
D_ATTN = D_MIX // 2
D_POOL = D_MIX - D_ATTN
HEAD_DIM = 64
N_Q_HEADS = D_ATTN // HEAD_DIM
N_KV_HEADS = max(1, N_Q_HEADS // 8)
GQA_GROUP = N_Q_HEADS // N_KV_HEADS
D_KV = N_KV_HEADS * HEAD_DIM
WINDOW = 128
BLOCK = 128
POOL_WINDOWS = (2, 4, 8, 16)
N_POOL_GROUPS = len(POOL_WINDOWS)
POOL_GROUP_DIM = D_POOL // N_POOL_GROUPS
D_IN = D_ATTN + 2 * D_KV + D_POOL
D_FF = ((8 * D_MODEL // 3 + 255) // 256) * 256
N_EXPERTS = 8
TOP_K = 2
MOE_D_FF = D_MODEL
N_DENSE = (DEPTH + 1) // 2
N_MOE = DEPTH // 2
N_MOD = 6
EPS = 1e-6
NEG_INF = -1e30

kernel_name = 'hybrid_swa_pool_moe_block'


def _rmsnorm(x, g):
    xf = x.astype(jnp.float32)
    y = xf * lax.rsqrt(jnp.mean(xf * xf, axis=-1, keepdims=True) + EPS)
    return (y * g.astype(jnp.float32)).astype(x.dtype)


def _modulate(h, shift, scale):
    return h * (1 + scale[:, None, :]) + shift[:, None, :]


def _alibi_slopes():
    h = jnp.arange(1, N_Q_HEADS + 1, dtype=jnp.float32)
    return jnp.exp2(-8.0 * h / N_Q_HEADS)


def _sliding_window_attention(q, k, v, sinks):
    B, S, _ = q.shape
    nb = S // BLOCK
    q = q.reshape(B, nb, BLOCK, N_KV_HEADS, GQA_GROUP, HEAD_DIM)
    k = k.reshape(B, S, N_KV_HEADS, HEAD_DIM)
    v = v.reshape(B, S, N_KV_HEADS, HEAD_DIM)
    pad = ((0, 0), (BLOCK, 0), (0, 0), (0, 0))
    kb = jnp.pad(k, pad).reshape(B, nb + 1, BLOCK, N_KV_HEADS, HEAD_DIM)
    vb = jnp.pad(v, pad).reshape(B, nb + 1, BLOCK, N_KV_HEADS, HEAD_DIM)
    k_win = jnp.concatenate([kb[:, :-1], kb[:, 1:]], axis=2)
    v_win = jnp.concatenate([vb[:, :-1], vb[:, 1:]], axis=2)
    scores = jnp.einsum('bnikgd,bnjkd->bnkgij', q, k_win).astype(jnp.float32)
    scores = scores * (HEAD_DIM ** -0.5)
    i = jnp.arange(BLOCK)[:, None]
    j = jnp.arange(2 * BLOCK)[None, :]
    dist = i + BLOCK - j
    s_abs = jnp.arange(nb)[:, None, None] * BLOCK - BLOCK + j[None]
    valid = (dist[None] >= 0) & (dist[None] < WINDOW) & (s_abs >= 0)
    slopes = _alibi_slopes().reshape(N_KV_HEADS, GQA_GROUP)
    alibi = -slopes[:, :, None, None] * dist.astype(jnp.float32)
    scores = jnp.where(valid[None, :, None, None], scores + alibi[None, None], NEG_INF)
    sink = jnp.broadcast_to(sinks.astype(jnp.float32).reshape(N_KV_HEADS, GQA_GROUP, 1, 1),
                            scores.shape[:-1] + (1,))
    probs = jax.nn.softmax(jnp.concatenate([scores, sink], axis=-1), axis=-1)[..., :-1]
    out = jnp.einsum('bnkgij,bnjkd->bnikgd', probs.astype(v.dtype), v_win)
    return out.reshape(B, S, D_ATTN)


def _multiscale_pool(u, w_pool, pool_scale):
    B, S, _ = u.shape
    ug = u.reshape(B, S, N_POOL_GROUPS, POOL_GROUP_DIM).astype(jnp.float32)
    cs = jnp.pad(jnp.cumsum(ug, axis=1), ((0, 0), (1, 0), (0, 0), (0, 0)))
    t = jnp.arange(S)[:, None]
    win = jnp.array(POOL_WINDOWS, dtype=jnp.int32)[None, :]
    lo = jnp.maximum(t + 1 - win, 0)
    count = (t + 1 - lo).astype(jnp.float32)
    g_idx = jnp.arange(N_POOL_GROUPS)[None, :]
    window_sum = cs[:, 1:] - cs[:, lo, g_idx]
    pooled = window_sum / count[None, :, :, None] - ug
    y = jnp.einsum('bsgc,gcd->bsgd', pooled.astype(u.dtype), w_pool)
    return y.reshape(B, S, D_POOL) * pool_scale


def _swiglu(h, w_gate, w_up, w_down):
    return (jax.nn.silu(h @ w_gate) * (h @ w_up)) @ w_down


def _moe_swiglu(h, router_w, router_b, w_gate, w_up, w_down):
    B, S, D = h.shape
    hf = h.reshape(B * S, D)
    logits = (hf @ router_w).astype(jnp.float32) + router_b.astype(jnp.float32)
    top_vals, top_idx = lax.top_k(logits, TOP_K)
    top_w = jax.nn.softmax(top_vals, axis=-1)
    combine = jnp.sum(jax.nn.one_hot(top_idx, N_EXPERTS, dtype=jnp.float32) * top_w[..., None],
                      axis=1).astype(h.dtype)
    out = jnp.zeros_like(hf)
    for e in range(N_EXPERTS):
        out = out + combine[:, e:e + 1] * _swiglu(hf, w_gate[e], w_up[e], w_down[e])
    return out.reshape(B, S, D)


def setup_inputs(seed: int = 0) -> dict:
    key = jax.random.key(seed)
    ks = jax.random.split(key, 24)
    f32 = jnp.float32
    nrm = lambda k, shape, s: jax.random.normal(k, shape, f32) * s
    gain = lambda k, shape: 1.0 + 0.05 * jax.random.normal(k, shape, f32)
    return {
        'x': nrm(ks[0], (BATCH, SEQ, D_MODEL), 1.0),
        'c': nrm(ks[1], (BATCH, D_MODEL), 1.0),
        'w_ada': nrm(ks[2], (DEPTH, D_MODEL, N_MOD * D_MODEL), 0.5 * D_MODEL ** -0.5),
        'b_ada': nrm(ks[3], (DEPTH, N_MOD * D_MODEL), 0.01),
        'norm_pre_mix': gain(ks[4], (DEPTH, D_MODEL)),
        'norm_post_mix': gain(ks[5], (DEPTH, D_MODEL)),
        'norm_pre_ffn': gain(ks[6], (DEPTH, D_MODEL)),
        'norm_post_ffn': gain(ks[7], (DEPTH, D_MODEL)),
        'w_in': nrm(ks[8], (DEPTH, D_MODEL, D_IN), D_MODEL ** -0.5),
        'sinks': nrm(ks[9], (DEPTH, N_Q_HEADS), 1.0),
        'w_pool': nrm(ks[10], (DEPTH, N_POOL_GROUPS, POOL_GROUP_DIM, POOL_GROUP_DIM), POOL_GROUP_DIM ** -0.5),
        'pool_scale': gain(ks[11], (DEPTH, D_POOL)),
        'w_out': nrm(ks[12], (DEPTH, D_MIX, D_MODEL), D_MIX ** -0.5),
        'ffn_w_gate': nrm(ks[13], (N_DENSE, D_MODEL, D_FF), D_MODEL ** -0.5),
        'ffn_w_up': nrm(ks[14], (N_DENSE, D_MODEL, D_FF), D_MODEL ** -0.5),
        'ffn_w_down': nrm(ks[15], (N_DENSE, D_FF, D_MODEL), D_FF ** -0.5),
        'router_w': nrm(ks[16], (N_MOE, D_MODEL, N_EXPERTS), D_MODEL ** -0.5),
        'router_b': nrm(ks[17], (N_MOE, N_EXPERTS), 0.01),
        'moe_w_gate': nrm(ks[18], (N_MOE, N_EXPERTS, D_MODEL, MOE_D_FF), D_MODEL ** -0.5),
        'moe_w_up': nrm(ks[19], (N_MOE, N_EXPERTS, D_MODEL, MOE_D_FF), D_MODEL ** -0.5),
        'moe_w_down': nrm(ks[20], (N_MOE, N_EXPERTS, MOE_D_FF, D_MODEL), MOE_D_FF ** -0.5),
    }


def reference(x, c, w_ada, b_ada, norm_pre_mix, norm_post_mix, norm_pre_ffn, norm_post_ffn,
              w_in, sinks, w_pool, pool_scale, w_out, ffn_w_gate, ffn_w_up, ffn_w_down,
              router_w, router_b, moe_w_gate, moe_w_up, moe_w_down):
    c_act = jax.nn.silu(c)
    for l in range(DEPTH):
        mod = c_act @ w_ada[l] + b_ada[l]
        shift1, scale1, gate1, shift2, scale2, gate2 = jnp.split(mod, N_MOD, axis=-1)
        h = _modulate(_rmsnorm(x, norm_pre_mix[l]), shift1, scale1)
        proj = h @ w_in[l]
        q = proj[..., :D_ATTN]
        k = proj[..., D_ATTN:D_ATTN + D_KV]
        v = proj[..., D_ATTN + D_KV:D_ATTN + 2 * D_KV]
        u = proj[..., D_ATTN + 2 * D_KV:]
        attn = _sliding_window_attention(q, k, v, sinks[l])
        pool = _multiscale_pool(u, w_pool[l], pool_scale[l])
        mix = jnp.concatenate([attn, pool], axis=-1) @ w_out[l]
        x = x + gate1[:, None, :] * _rmsnorm(mix, norm_post_mix[l])
        h2 = _modulate(_rmsnorm(x, norm_pre_ffn[l]), shift2, scale2)
        if l % 2 == 0:
            d = l // 2
            f = _swiglu(h2, ffn_w_gate[d], ffn_w_up[d], ffn_w_down[d])
        else:
            m = l // 2
            f = _moe_swiglu(h2, router_w[m], router_b[m], moe_w_gate[m], moe_w_up[m], moe_w_down[m])
        x = x + gate2[:, None, :] * _rmsnorm(f, norm_post_ffn[l])
    return x
```

```python
import functools

import jax
import jax.numpy as jnp
from jax import lax
from jax.experimental import pallas as pl
from jax.experimental.pallas import tpu as pltpu

F32 = jnp.float32
BF16 = jnp.bfloat16
U32 = jnp.uint32
I32 = jnp.int32

EPS = 1e-6
NEG_INF = -1e30
BLOCK = 128
HEAD_DIM = 64
GQA_GROUP = 8
POOL_WINDOWS = (2, 4, 8, 16)
N_EXPERTS = 8
N_MOD = 6
LANES = 128

MIB = 1 << 20


def _params(sem, vmem_mib):
    return pltpu.CompilerParams(dimension_semantics=sem, vmem_limit_bytes=vmem_mib * MIB)


def _sigmoid(x):
    return 1.0 / (1.0 + jnp.exp(-x))


def _rms(x, g):
    return x * lax.rsqrt(jnp.mean(x * x, axis=-1, keepdims=True) + EPS) * g


def _pack_pair(a, b):
    ua = pltpu.bitcast(a.astype(BF16).astype(F32), U32)
    ub = pltpu.bitcast(b.astype(BF16).astype(F32), U32)
    return ua | (ub >> 16)


def _unpack_pair(p):
    hi = pltpu.bitcast(p & jnp.uint32(0xFFFF0000), F32).astype(BF16)
    lo = pltpu.bitcast(p << 16, F32).astype(BF16)
    return hi, lo


def _ada_kernel(c_ref, w_ref, b_ref, o_ref):
    c = c_ref[...]
    ca = (c * _sigmoid(c)).astype(BF16)
    o_ref[0] = jnp.dot(ca, w_ref[0].astype(BF16), preferred_element_type=F32) + b_ref[0]


def _ada(c, w_ada, b_ada):
    depth, d, nm = w_ada.shape
    b = c.shape[0]
    tn = 512
    return pl.pallas_call(
        _ada_kernel,
        grid=(depth, nm // tn),
        in_specs=[
            pl.BlockSpec((b, d), lambda l, j: (0, 0)),
            pl.BlockSpec((1, d, tn), lambda l, j: (l, 0, j)),
            pl.BlockSpec((1, 1, tn), lambda l, j: (l, 0, j)),
        ],
        out_specs=pl.BlockSpec((1, b, tn), lambda l, j: (l, 0, j)),
        out_shape=jax.ShapeDtypeStruct((depth, b, nm), F32),
        compiler_params=_params(("parallel", "parallel"), 40),
        name="ada_mod",
    )(c, w_ada, b_ada.reshape(depth, 1, nm))


def _prenorm_kernel(x_ref, g_ref, sh_ref, sc_ref, o_ref):
    h = _rms(x_ref[...], g_ref[...]) * (1.0 + sc_ref[0]) + sh_ref[0]
    o_ref[...] = h.astype(BF16)


def _prenorm(x2, g, shift, scale, seq):
    n, d = x2.shape
    tr = min(256, seq)
    per = seq // tr
    return pl.pallas_call(
        _prenorm_kernel,
        grid=(n // tr,),
        in_specs=[
            pl.BlockSpec((tr, d), lambda i: (i, 0)),
            pl.BlockSpec((1, d), lambda i: (0, 0)),
            pl.BlockSpec((1, 1, d), lambda i: (i // per, 0, 0)),
            pl.BlockSpec((1, 1, d), lambda i: (i // per, 0, 0)),
        ],
        out_specs=pl.BlockSpec((tr, d), lambda i: (i, 0)),
        out_shape=jax.ShapeDtypeStruct((n, d), BF16),
        compiler_params=_params(("parallel",), 40),
        name="prenorm",
    )(x2, g.reshape(1, d), shift, scale)


def _mm_kernel(a_ref, w_ref, o_ref):
    o_ref[...] = jnp.dot(a_ref[...], w_ref[...], preferred_element_type=F32).astype(o_ref.dtype)


def _mm(a, w, out_dtype, tm, tn):
    m, k = a.shape
    n = w.shape[1]
    tm = min(tm, m)
    return pl.pallas_call(
        _mm_kernel,
        grid=(m // tm, n // tn),
        in_specs=[
            pl.BlockSpec((tm, k), lambda i, j: (i, 0)),
            pl.BlockSpec((k, tn), lambda i, j: (0, j)),
        ],
        out_specs=pl.BlockSpec((tm, tn), lambda i, j: (i, j)),
        out_shape=jax.ShapeDtypeStruct((m, n), out_dtype),
        compiler_params=_params(("parallel", "arbitrary"), 48),
        name="proj_in",
    )(a, w)


def _mm2_kernel(a1_ref, a2_ref, w1_ref, w2_ref, o_ref):
    acc = jnp.dot(a1_ref[...], w1_ref[...], preferred_element_type=F32)
    acc = acc + jnp.dot(a2_ref[...], w2_ref[...], preferred_element_type=F32)
    o_ref[...] = acc.astype(o_ref.dtype)


def _mm2(a1, a2, w, out_dtype, tm, tn):
    m, k1 = a1.shape
    n = w.shape[1]
    tm = min(tm, m)
    return pl.pallas_call(
        _mm2_kernel,
        grid=(m // tm, n // tn),
        in_specs=[
            pl.BlockSpec((tm, k1), lambda i, j: (i, 0)),
            pl.BlockSpec((tm, k1), lambda i, j: (i, 0)),
            pl.BlockSpec((k1, tn), lambda i, j: (0, j)),
            pl.BlockSpec((k1, tn), lambda i, j: (1, j)),
        ],
        out_specs=pl.BlockSpec((tm, tn), lambda i, j: (i, j)),
        out_shape=jax.ShapeDtypeStruct((m, n), out_dtype),
        compiler_params=_params(("parallel", "arbitrary"), 48),
        name="proj_out",
    )(a1, a2, w, w)


def _attn_kernel(slopes, sinks_ref, q_ref, kp_ref, kc_ref, vp_ref, vc_ref, o_ref):
    n = pl.program_id(1)
    k = jnp.concatenate([kp_ref[...], kc_ref[...]], axis=0)
    v = jnp.concatenate([vp_ref[...], vc_ref[...]], axis=0)
    i = lax.broadcasted_iota(I32, (BLOCK, 2 * BLOCK), 0)
    j = lax.broadcasted_iota(I32, (BLOCK, 2 * BLOCK), 1)
    dist = i + BLOCK - j
    valid = (dist >= 0) & (dist < BLOCK) & ((j >= BLOCK) | (n > 0))
    distf = dist.astype(F32)
    n_heads = len(slopes)
    outs = []
    for h in range(n_heads):
        kv = h // GQA_GROUP
        kk = k[:, kv * HEAD_DIM:(kv + 1) * HEAD_DIM]
        vv = v[:, kv * HEAD_DIM:(kv + 1) * HEAD_DIM]
        qh = q_ref[:, h * HEAD_DIM:(h + 1) * HEAD_DIM] * jnp.asarray(HEAD_DIM ** -0.5, BF16)
        s = lax.dot_general(qh, kk, (((1,), (1,)), ((), ())), preferred_element_type=F32)
        s = jnp.where(valid, s - slopes[h] * distf, NEG_INF)
        sink = sinks_ref[h]
        m = jnp.maximum(jnp.max(s, axis=-1, keepdims=True), sink)
        p = jnp.exp(s - m)
        denom = jnp.sum(p, axis=-1, keepdims=True) + jnp.exp(sink - m)
        o = jnp.dot(p.astype(BF16), vv, preferred_element_type=F32) / denom
        outs.append(o.astype(BF16))
    o_ref[...] = jnp.concatenate(outs, axis=-1)


def _attention(proj, sinks, batch, seq, d_attn, d_kv):
    n = proj.shape[0]
    nb = seq // BLOCK
    n_heads = d_attn // HEAD_DIM
    slopes = tuple(2.0 ** (-8.0 * (h + 1) / n_heads) for h in range(n_heads))
    kcol = d_attn // d_kv
    vcol = kcol + 1

    def cur(col):
        return lambda b, i: (b * nb + i, col)

    def prev(col):
        return lambda b, i: (b * nb + jnp.maximum(i - 1, 0), col)

    return pl.pallas_call(
        functools.partial(_attn_kernel, slopes),
        grid=(batch, nb),
        in_specs=[
            pl.BlockSpec(memory_space=pltpu.SMEM),
            pl.BlockSpec((BLOCK, d_attn), cur(0)),
            pl.BlockSpec((BLOCK, d_kv), prev(kcol)),
            pl.BlockSpec((BLOCK, d_kv), cur(kcol)),
            pl.BlockSpec((BLOCK, d_kv), prev(vcol)),
            pl.BlockSpec((BLOCK, d_kv), cur(vcol)),
        ],
        out_specs=pl.BlockSpec((BLOCK, d_attn), cur(0)),
        out_shape=jax.ShapeDtypeStruct((n, d_attn), BF16),
        compiler_params=_params(("parallel", "arbitrary"), 40),
        name="swa_attention",
    )(sinks, proj, proj, proj, proj, proj)


def _pool_kernel(u_ref, w_ref, ps_ref, o_ref):
    g = pl.program_id(1)
    u = u_ref[...].astype(F32)
    row = lax.broadcasted_iota(I32, u.shape, 0)

    def shifted(x, k):
        return jnp.where(row >= k, pltpu.roll(x, k, 0), 0.0)

    s2 = u + shifted(u, 1)
    s4 = s2 + shifted(s2, 2)
    s8 = s4 + shifted(s4, 4)
    s16 = s8 + shifted(s8, 8)
    wsum = jnp.where(g == 0, s2, jnp.where(g == 1, s4, jnp.where(g == 2, s8, s16)))
    win = jnp.left_shift(jnp.int32(POOL_WINDOWS[0]), g)
    cnt = jnp.minimum(row + 1, win).astype(F32)
    pooled = wsum / cnt - u
    y = jnp.dot(pooled.astype(BF16), w_ref[0], preferred_element_type=F32)
    o_ref[...] = (y * ps_ref[...]).astype(BF16)


def _pool(proj, w_pool, pool_scale, batch, seq, u_col0):
    n = proj.shape[0]
    ng, c, _ = w_pool.shape
    assert POOL_WINDOWS == tuple(POOL_WINDOWS[0] << g for g in range(ng))
    col0 = u_col0 // c
    return pl.pallas_call(
        _pool_kernel,
        grid=(batch, ng),
        in_specs=[
            pl.BlockSpec((seq, c), lambda b, g: (b, col0 + g)),
            pl.BlockSpec((1, c, c), lambda b, g: (g, 0, 0)),
            pl.BlockSpec((1, c), lambda b, g: (0, g)),
        ],
        out_specs=pl.BlockSpec((seq, c), lambda b, g: (b, g)),
        out_shape=jax.ShapeDtypeStruct((n, ng * c), BF16),
        compiler_params=_params(("parallel", "arbitrary"), 48),
        name="pool_mixer",
    )(proj, w_pool, pool_scale.reshape(1, ng * c))


def _top2_route(logits):
    lane = lax.broadcasted_iota(I32, logits.shape, 1).astype(F32)
    m1 = jnp.max(logits, axis=-1, keepdims=True)
    i1 = jnp.min(jnp.where(logits == m1, lane, float(LANES)), axis=-1, keepdims=True)
    l2 = jnp.where(lane == i1, -jnp.inf, logits)
    m2 = jnp.max(l2, axis=-1, keepdims=True)
    i2 = jnp.min(jnp.where(l2 == m2, lane, float(LANES)), axis=-1, keepdims=True)
    e = jnp.exp(m2 - m1)
    w1 = 1.0 / (1.0 + e)
    w2 = e / (1.0 + e)
    return jnp.where(lane == 0.0, i1, jnp.where(lane == 1.0, i2, jnp.where(lane == 2.0, w1, jnp.where(lane == 3.0, w2, 0.0))))


def _post_kernel(mode, gathered, *refs):
    refs = list(refs)
    if gathered:
        pos_ref, y_hbm, route_in = refs[:3]
        refs = refs[3:]
    else:
        f_ref = refs.pop(0)
    x_ref, gate_ref, gpost_ref = refs[:3]
    refs = refs[3:]
    if mode != "last":
        gpre_ref, sh_ref, sc_ref = refs[:3]
        refs = refs[3:]
    if mode == "route":
        rw_ref, rb_ref = refs[:2]
        refs = refs[2:]
    xo_ref = refs.pop(0)
    if mode != "last":
        ho_ref = refs.pop(0)
    if mode == "route":
        ro_ref = refs.pop(0)

    if gathered:
        ybuf, sem = refs
        tr = x_ref.shape[0]

        def copy(r, kk, src_row):
            return pltpu.make_async_copy(y_hbm.at[pl.ds(src_row, 1)], ybuf.at[kk, pl.ds(r, 1)], sem)

        def start(r, carry):
            copy(r, 0, pos_ref[0, 0, 2 * r]).start()
            copy(r, 1, pos_ref[0, 0, 2 * r + 1]).start()
            return carry

        def wait(r, carry):
            copy(r, 0, 0).wait()
            copy(r, 1, 0).wait()
            return carry

        lax.fori_loop(0, tr, start, 0)
        lax.fori_loop(0, tr, wait, 0)
        rt = route_in[...]
        f = ybuf[0] * rt[:, 2:3] + ybuf[1] * rt[:, 3:4]
    else:
        f = f_ref[...].astype(F32)

    xn = x_ref[...] + gate_ref[0] * _rms(f, gpost_ref[...])
    xo_ref[...] = xn
    if mode == "last":
        return
    h = _rms(xn, gpre_ref[...]) * (1.0 + sc_ref[0]) + sh_ref[0]
    if mode == "next":
        ho_ref[...] = h.astype(BF16)
        return
    half = h.shape[1] // 2
    ho_ref[...] = _pack_pair(h[:, :half], h[:, half:])
    logits = jnp.dot(h.astype(BF16), rw_ref[...], preferred_element_type=F32) + rb_ref[...]
    ro_ref[...] = _top2_route(logits)


def _post(mode, f, x2, gate, g_post, seq, nxt=None, router=None, gather=None):
    n, d = x2.shape
    tr = min(256, seq)
    per = seq // tr
    row = lambda i: (i, 0)
    fix = lambda i: (0, 0)
    bat = lambda i: (i // per, 0, 0)
    in_specs, args, scratch = [], [], []
    if gather is not None:
        pos, y, route = gather
        in_specs += [
            pl.BlockSpec((1, 1, 2 * tr), lambda i: (i, 0, 0), memory_space=pltpu.SMEM),
            pl.BlockSpec(memory_space=pl.ANY),
            pl.BlockSpec((tr, LANES), row),
        ]
        args += [pos.reshape(n // tr, 1, 2 * tr), y, route]
        scratch = [pltpu.VMEM((2, tr, d), F32), pltpu.SemaphoreType.DMA(())]
    else:
        in_specs.append(pl.BlockSpec((tr, d), row))
        args.append(f)
    in_specs += [pl.BlockSpec((tr, d), row), pl.BlockSpec((1, 1, d), bat), pl.BlockSpec((1, d), fix)]
    args += [x2, gate, g_post.reshape(1, d)]
    out_specs = [pl.BlockSpec((tr, d), row)]
    out_shape = [jax.ShapeDtypeStruct((n, d), F32)]
    if mode != "last":
        g_pre, shift, scale = nxt
        in_specs += [pl.BlockSpec((1, d), fix), pl.BlockSpec((1, 1, d), bat), pl.BlockSpec((1, 1, d), bat)]
        args += [g_pre.reshape(1, d), shift, scale]
    if mode == "next":
        out_specs.append(pl.BlockSpec((tr, d), row))
        out_shape.append(jax.ShapeDtypeStruct((n, d), BF16))
    if mode == "route":
        rw, rb = router
        in_specs += [pl.BlockSpec((d, LANES), fix), pl.BlockSpec((1, LANES), fix)]
        args += [rw, rb]
        out_specs += [pl.BlockSpec((tr, d // 2), row), pl.BlockSpec((tr, LANES), row)]
        out_shape += [jax.ShapeDtypeStruct((n, d // 2), U32), jax.ShapeDtypeStruct((n, LANES), F32)]
    return pl.pallas_call(
        functools.partial(_post_kernel, mode, gather is not None),
        grid=(n // tr,),
        in_specs=in_specs,
        out_specs=out_specs,
        out_shape=out_shape,
        scratch_shapes=scratch,
        compiler_params=_params(("arbitrary",), 56),
        name="post_" + mode + ("_gather" if gather is not None else ""),
    )(*args)


def _swiglu_step(h, wg_ref, wu_ref, wd_ref, o_ref, j, n_chunk):
    g = jnp.dot(h, wg_ref[...], preferred_element_type=F32)
    u = jnp.dot(h, wu_ref[...], preferred_element_type=F32)
    a = (g * _sigmoid(g) * u).astype(BF16)
    d = o_ref.shape[1]
    cw = d // n_chunk

    @pl.when(j == 0)
    def _():
        o_ref[...] = jnp.zeros_like(o_ref)

    for c in range(n_chunk):
        o_ref[:, c * cw:(c + 1) * cw] += jnp.dot(a, wd_ref[:, c * cw:(c + 1) * cw], preferred_element_type=F32)


def _ffn_kernel(h_ref, wg_ref, wu_ref, wd_ref, o_ref):
    _swiglu_step(h_ref[...], wg_ref, wu_ref, wd_ref, o_ref, pl.program_id(1), 4)


def _ffn(h, wg, wu, wd, tm, tf):
    n, d = h.shape
    dff = wg.shape[1]
    tm = min(tm, n)
    return pl.pallas_call(
        _ffn_kernel,
        grid=(n // tm, dff // tf),
        in_specs=[
            pl.BlockSpec((tm, d), lambda i, j: (i, 0)),
            pl.BlockSpec((d, tf), lambda i, j: (0, j)),
            pl.BlockSpec((d, tf), lambda i, j: (0, j)),
            pl.BlockSpec((tf, d), lambda i, j: (j, 0)),
        ],
        out_specs=pl.BlockSpec((tm, d), lambda i, j: (i, 0)),
        out_shape=jax.ShapeDtypeStruct((n, d), F32),
        compiler_params=_params(("parallel", "arbitrary"), 56),
        name="dense_swiglu",
    )(h, wg, wu, wd)


def _gather_rows_kernel(idx_ref, src_hbm, o_ref, sem):
    rows = o_ref.shape[0]

    def copy(r, src_row):
        return pltpu.make_async_copy(src_hbm.at[pl.ds(src_row, 1)], o_ref.at[pl.ds(r, 1)], sem)

    def start(r, carry):
        copy(r, idx_ref[0, 0, r]).start()
        return carry

    def wait(r, carry):
        copy(r, 0).wait()
        return carry

    lax.fori_loop(0, rows, start, 0)
    lax.fori_loop(0, rows, wait, 0)


def _gather_rows(src, idx, tg):
    rows = idx.shape[0]
    w = src.shape[1]
    return pl.pallas_call(
        _gather_rows_kernel,
        grid=(rows // tg,),
        in_specs=[
            pl.BlockSpec((1, 1, tg), lambda i: (i, 0, 0), memory_space=pltpu.SMEM),
            pl.BlockSpec(memory_space=pl.ANY),
        ],
        out_specs=pl.BlockSpec((tg, w), lambda i: (i, 0)),
        out_shape=jax.ShapeDtypeStruct((rows, w), src.dtype),
        scratch_shapes=[pltpu.SemaphoreType.DMA(())],
        compiler_params=_params(("arbitrary",), 40),
        name="moe_gather",
    )(idx.reshape(rows // tg, 1, tg), src)


def _expert_kernel(te_ref, nt_ref, xs_ref, wg_ref, wu_ref, wd_ref, o_ref, h_scr):
    t = pl.program_id(0)
    j = pl.program_id(1)
    live = t < nt_ref[0]

    @pl.when(live & (j == 0))
    def _():
        hi, lo = _unpack_pair(xs_ref[...])
        half = hi.shape[1]
        h_scr[:, :half] = hi
        h_scr[:, half:] = lo

    @pl.when(live)
    def _():
        _swiglu_step(h_scr[...], wg_ref.at[0], wu_ref.at[0], wd_ref.at[0], o_ref, j, 4)

    @pl.when(jnp.logical_not(live) & (j == 0))
    def _():
        o_ref[...] = jnp.zeros_like(o_ref)


def _experts(xs, tile_expert, n_tiles, wg, wu, wd, tm, tf):
    rows, half = xs.shape
    d = 2 * half
    dff = wg.shape[2]
    nj = dff // tf
    n_t = rows // tm

    def tile(t, j, te, nt):
        return (jnp.minimum(t, nt[0] - 1), 0)

    def chunk(t, j, nt):
        return jnp.where(t < nt[0], j, nj - 1)

    return pl.pallas_call(
        _expert_kernel,
        grid_spec=pltpu.PrefetchScalarGridSpec(
            num_scalar_prefetch=2,
            grid=(n_t, nj),
            in_specs=[
                pl.BlockSpec((tm, half), tile),
                pl.BlockSpec((1, d, tf), lambda t, j, te, nt: (te[t], 0, chunk(t, j, nt))),
                pl.BlockSpec((1, d, tf), lambda t, j, te, nt: (te[t], 0, chunk(t, j, nt))),
                pl.BlockSpec((1, tf, d), lambda t, j, te, nt: (te[t], chunk(t, j, nt), 0)),
            ],
            out_specs=pl.BlockSpec((tm, d), lambda t, j, te, nt: (t, 0)),
            scratch_shapes=[pltpu.VMEM((tm, d), BF16)],
        ),
        out_shape=jax.ShapeDtypeStruct((rows, d), F32),
        compiler_params=_params(("arbitrary", "arbitrary"), 56),
        name="expert_swiglu",
    )(tile_expert, n_tiles, xs, wg, wu, wd)


def _moe_plan(top_idx, tm):
    n = top_idx.shape[0]
    p = 2 * n
    flat = top_idx.reshape(p)
    experts = jnp.arange(N_EXPERTS, dtype=I32)
    onehot = (flat[:, None] == experts[None, :]).astype(I32)
    csum = jnp.cumsum(onehot, axis=0)
    rank = jnp.sum(onehot * csum, axis=1) - 1
    counts = csum[-1]
    tiles_per = (counts + (tm - 1)) // tm
    tile_end = jnp.cumsum(tiles_per)
    tile_start = tile_end - tiles_per
    pos = (tile_start * tm)[flat] + rank
    n_tiles = tile_end[-1:]
    t_max = p // tm + N_EXPERTS
    t_ids = jnp.arange(t_max, dtype=I32)
    te = jnp.sum((t_ids[:, None] >= tile_end[None, :]).astype(I32), axis=1)
    last_e = jnp.max(jnp.where(tiles_per > 0, experts, 0))
    te = jnp.minimum(te, last_e)
    src_tok = jnp.zeros((t_max * tm,), I32).at[pos].set(jnp.arange(p, dtype=I32) // 2)
    return pos.astype(I32), src_tok, te.astype(I32), n_tiles.astype(I32)


def kernel(x, c, w_ada, b_ada, norm_pre_mix, norm_post_mix, norm_pre_ffn, norm_post_ffn, w_in, sinks, w_pool,
           pool_scale, w_out, ffn_w_gate, ffn_w_up, ffn_w_down, router_w, router_b, moe_w_gate, moe_w_up, moe_w_down):
    batch, seq, d = x.shape
    depth = w_ada.shape[0]
    n = batch * seq
    d_attn = d // 2
    n_heads = sinks.shape[1]
    d_kv = (n_heads // GQA_GROUP) * HEAD_DIM
    u_col0 = d_attn + 2 * d_kv
    assert n_heads * HEAD_DIM == d_attn and seq % BLOCK == 0

    x2 = x.reshape(n, d)
    mod = _ada(c, w_ada, b_ada)

    def mods(l):
        return [mod[l, :, k * d:(k + 1) * d].reshape(batch, 1, d) for k in range(N_MOD)]

    shift1, scale1, gate1, shift2, scale2, gate2 = mods(0)
    h = _prenorm(x2, norm_pre_mix[0], shift1, scale1, seq)
    for l in range(depth):
        proj = _mm(h, w_in[l].astype(BF16), BF16, 1024, 512)
        attn = _attention(proj, sinks[l], batch, seq, d_attn, d_kv)
        pool = _pool(proj, w_pool[l].astype(BF16), pool_scale[l], batch, seq, u_col0)
        mix = _mm2(attn, pool, w_out[l].astype(BF16), F32, 1024, 512)
        nxt = (norm_pre_ffn[l], shift2, scale2)
        if l % 2 == 0:
            i = l // 2
            x2, h2 = _post("next", mix, x2, gate1, norm_post_mix[l], seq, nxt=nxt)
            f = _ffn(h2, ffn_w_gate[i].astype(BF16), ffn_w_up[i].astype(BF16), ffn_w_down[i].astype(BF16), 512, 256)
            gather = None
        else:
            i = l // 2
            rw = jnp.zeros((d, LANES), BF16).at[:, :N_EXPERTS].set(router_w[i].astype(BF16))
            rb = jnp.full((1, LANES), NEG_INF, F32).at[0, :N_EXPERTS].set(router_b[i])
            x2, h2p, route = _post("route", mix, x2, gate1, norm_post_mix[l], seq, nxt=nxt, router=(rw, rb))
            tm = 512
            pos, src_tok, tile_expert, n_tiles = _moe_plan(route[:, :2].astype(I32), tm)
            xs = _gather_rows(h2p, src_tok, tm)
            y = _experts(xs, tile_expert, n_tiles, moe_w_gate[i].astype(BF16), moe_w_up[i].astype(BF16),
                         moe_w_down[i].astype(BF16), tm, 256)
            f, gather = None, (pos, y, route)
        if l + 1 < depth:
            shift1, scale1, gate1n, shift2n, scale2n, gate2n = mods(l + 1)
            x2, h = _post("next", f, x2, gate2, norm_post_ffn[l], seq,
                          nxt=(norm_pre_mix[l + 1], shift1, scale1), gather=gather)
            gate1, shift2, scale2, gate2 = gate1n, shift2n, scale2n, gate2n
        else:
            (x2,) = _post("last", f, x2, gate2, norm_post_ffn[l], seq, gather=gather)
    return x2.reshape(batch, seq, d)
```

```python
import functools

import jax
import jax.numpy as jnp
from jax import lax
from jax.experimental import pallas as pl
from jax.experimental.pallas import tpu as pltpu

F32 = jnp.float32
BF16 = jnp.bfloat16
U32 = jnp.uint32
I32 = jnp.int32

EPS = 1e-6
NEG_INF = -1e30
BLOCK = 128
HEAD_DIM = 64
GQA_GROUP = 8
POOL_WINDOWS = (2, 4, 8, 16)
N_EXPERTS = 8
N_MOD = 6
LANES = 128
FFN_ROWS = 512
FFN_CHUNK = 512

MIB = 1 << 20


def _params(sem, vmem_mib):
    return pltpu.CompilerParams(dimension_semantics=sem, vmem_limit_bytes=vmem_mib * MIB)


def _sigmoid(x):
    return 1.0 / (1.0 + jnp.exp(-x))


def _rms(x, g):
    return x * lax.rsqrt(jnp.mean(x * x, axis=-1, keepdims=True) + EPS) * g


def _pack_pair(a, b):
    ua = pltpu.bitcast(a.astype(BF16).astype(F32), U32)
    ub = pltpu.bitcast(b.astype(BF16).astype(F32), U32)
    return ua | (ub >> 16)


def _unpack_pair(p):
    hi = pltpu.bitcast(p & jnp.uint32(0xFFFF0000), F32).astype(BF16)
    lo = pltpu.bitcast(p << 16, F32).astype(BF16)
    return hi, lo


def _ada_kernel(c_ref, w_ref, b_ref, o_ref):
    c = c_ref[...]
    ca = (c * _sigmoid(c)).astype(BF16)
    o_ref[0] = jnp.dot(ca, w_ref[0].astype(BF16), preferred_element_type=F32) + b_ref[0]


def _ada(c, w_ada, b_ada):
    depth, d, nm = w_ada.shape
    b = c.shape[0]
    tn = 512
    return pl.pallas_call(
        _ada_kernel,
        grid=(depth, nm // tn),
        in_specs=[
            pl.BlockSpec((b, d), lambda l, j: (0, 0)),
            pl.BlockSpec((1, d, tn), lambda l, j: (l, 0, j)),
            pl.BlockSpec((1, 1, tn), lambda l, j: (l, 0, j)),
        ],
        out_specs=pl.BlockSpec((1, b, tn), lambda l, j: (l, 0, j)),
        out_shape=jax.ShapeDtypeStruct((depth, b, nm), F32),
        compiler_params=_params(("parallel", "parallel"), 40),
        name="ada_mod",
    )(c, w_ada, b_ada.reshape(depth, 1, nm))


def _prenorm_kernel(x_ref, g_ref, sh_ref, sc_ref, o_ref):
    h = _rms(x_ref[...], g_ref[...]) * (1.0 + sc_ref[0]) + sh_ref[0]
    o_ref[...] = h.astype(BF16)


def _prenorm(x2, g, shift, scale, seq):
    n, d = x2.shape
    tr = min(256, seq)
    per = seq // tr
    return pl.pallas_call(
        _prenorm_kernel,
        grid=(n // tr,),
        in_specs=[
            pl.BlockSpec((tr, d), lambda i: (i, 0)),
            pl.BlockSpec((1, d), lambda i: (0, 0)),
            pl.BlockSpec((1, 1, d), lambda i: (i // per, 0, 0)),
            pl.BlockSpec((1, 1, d), lambda i: (i // per, 0, 0)),
        ],
        out_specs=pl.BlockSpec((tr, d), lambda i: (i, 0)),
        out_shape=jax.ShapeDtypeStruct((n, d), BF16),
        compiler_params=_params(("parallel",), 40),
        name="prenorm",
    )(x2, g.reshape(1, d), shift, scale)


def _mm_kernel(a_ref, w_ref, o_ref):
    o_ref[...] = jnp.dot(a_ref[...], w_ref[...], preferred_element_type=F32).astype(o_ref.dtype)


def _mm(a, w, out_dtype, tm, tn):
    m, k = a.shape
    n = w.shape[1]
    tm = min(tm, m)
    return pl.pallas_call(
        _mm_kernel,
        grid=(m // tm, n // tn),
        in_specs=[
            pl.BlockSpec((tm, k), lambda i, j: (i, 0)),
            pl.BlockSpec((k, tn), lambda i, j: (0, j)),
        ],
        out_specs=pl.BlockSpec((tm, tn), lambda i, j: (i, j)),
        out_shape=jax.ShapeDtypeStruct((m, n), out_dtype),
        compiler_params=_params(("parallel", "arbitrary"), 48),
        name="proj_in",
    )(a, w)


def _mm2_kernel(a1_ref, a2_ref, w1_ref, w2_ref, o_ref):
    acc = jnp.dot(a1_ref[...], w1_ref[...], preferred_element_type=F32)
    acc = acc + jnp.dot(a2_ref[...], w2_ref[...], preferred_element_type=F32)
    o_ref[...] = acc.astype(o_ref.dtype)


def _mm2(a1, a2, w, out_dtype, tm, tn):
    m, k1 = a1.shape
    n = w.shape[1]
    tm = min(tm, m)
    return pl.pallas_call(
        _mm2_kernel,
        grid=(m // tm, n // tn),
        in_specs=[
            pl.BlockSpec((tm, k1), lambda i, j: (i, 0)),
            pl.BlockSpec((tm, k1), lambda i, j: (i, 0)),
            pl.BlockSpec((k1, tn), lambda i, j: (0, j)),
            pl.BlockSpec((k1, tn), lambda i, j: (1, j)),
        ],
        out_specs=pl.BlockSpec((tm, tn), lambda i, j: (i, j)),
        out_shape=jax.ShapeDtypeStruct((m, n), out_dtype),
        compiler_params=_params(("parallel", "arbitrary"), 48),
        name="proj_out",
    )(a1, a2, w, w)


def _attn_kernel(slopes, sinks_ref, q_ref, kp_ref, kc_ref, vp_ref, vc_ref, o_ref):
    n = pl.program_id(1)
    k = jnp.concatenate([kp_ref[...], kc_ref[...]], axis=0)
    v = jnp.concatenate([vp_ref[...], vc_ref[...]], axis=0)
    i = lax.broadcasted_iota(I32, (BLOCK, 2 * BLOCK), 0)
    j = lax.broadcasted_iota(I32, (BLOCK, 2 * BLOCK), 1)
    dist = i + BLOCK - j
    valid = (dist >= 0) & (dist < BLOCK) & ((j >= BLOCK) | (n > 0))
    distf = dist.astype(F32)
    n_heads = len(slopes)
    outs = []
    for h in range(n_heads):
        kv = h // GQA_GROUP
        kk = k[:, kv * HEAD_DIM:(kv + 1) * HEAD_DIM]
        vv = v[:, kv * HEAD_DIM:(kv + 1) * HEAD_DIM]
        qh = q_ref[:, h * HEAD_DIM:(h + 1) * HEAD_DIM] * jnp.asarray(HEAD_DIM ** -0.5, BF16)
        s = lax.dot_general(qh, kk, (((1,), (1,)), ((), ())), preferred_element_type=F32)
        s = jnp.where(valid, s - slopes[h] * distf, NEG_INF)
        sink = sinks_ref[h]
        m = jnp.maximum(jnp.max(s, axis=-1, keepdims=True), sink)
        p = jnp.exp(s - m)
        denom = jnp.sum(p, axis=-1, keepdims=True) + jnp.exp(sink - m)
        o = jnp.dot(p.astype(BF16), vv, preferred_element_type=F32) / denom
        outs.append(o.astype(BF16))
    o_ref[...] = jnp.concatenate(outs, axis=-1)


def _attention(proj, sinks, batch, seq, d_attn, d_kv):
    n = proj.shape[0]
    nb = seq // BLOCK
    n_heads = d_attn // HEAD_DIM
    slopes = tuple(2.0 ** (-8.0 * (h + 1) / n_heads) for h in range(n_heads))
    kcol = d_attn // d_kv
    vcol = kcol + 1

    def cur(col):
        return lambda b, i: (b * nb + i, col)

    def prev(col):
        return lambda b, i: (b * nb + jnp.maximum(i - 1, 0), col)

    return pl.pallas_call(
        functools.partial(_attn_kernel, slopes),
        grid=(batch, nb),
        in_specs=[
            pl.BlockSpec(memory_space=pltpu.SMEM),
            pl.BlockSpec((BLOCK, d_attn), cur(0)),
            pl.BlockSpec((BLOCK, d_kv), prev(kcol)),
            pl.BlockSpec((BLOCK, d_kv), cur(kcol)),
            pl.BlockSpec((BLOCK, d_kv), prev(vcol)),
            pl.BlockSpec((BLOCK, d_kv), cur(vcol)),
        ],
        out_specs=pl.BlockSpec((BLOCK, d_attn), cur(0)),
        out_shape=jax.ShapeDtypeStruct((n, d_attn), BF16),
        compiler_params=_params(("parallel", "arbitrary"), 40),
        name="swa_attention",
    )(sinks, proj, proj, proj, proj, proj)


def _pool_kernel(u_ref, w_ref, ps_ref, o_ref):
    g = pl.program_id(1)
    u = u_ref[...].astype(F32)
    row = lax.broadcasted_iota(I32, u.shape, 0)

    def shifted(x, k):
        return jnp.where(row >= k, pltpu.roll(x, k, 0), 0.0)

    s2 = u + shifted(u, 1)
    s4 = s2 + shifted(s2, 2)
    s8 = s4 + shifted(s4, 4)
    s16 = s8 + shifted(s8, 8)
    wsum = jnp.where(g == 0, s2, jnp.where(g == 1, s4, jnp.where(g == 2, s8, s16)))
    win = jnp.left_shift(jnp.int32(POOL_WINDOWS[0]), g)
    cnt = jnp.minimum(row + 1, win).astype(F32)
    pooled = wsum / cnt - u
    y = jnp.dot(pooled.astype(BF16), w_ref[0], preferred_element_type=F32)
    o_ref[...] = (y * ps_ref[...]).astype(BF16)


def _pool(proj, w_pool, pool_scale, batch, seq, u_col0):
    n = proj.shape[0]
    ng, c, _ = w_pool.shape
    assert POOL_WINDOWS == tuple(POOL_WINDOWS[0] << g for g in range(ng))
    col0 = u_col0 // c
    return pl.pallas_call(
        _pool_kernel,
        grid=(batch, ng),
        in_specs=[
            pl.BlockSpec((seq, c), lambda b, g: (b, col0 + g)),
            pl.BlockSpec((1, c, c), lambda b, g: (g, 0, 0)),
            pl.BlockSpec((1, c), lambda b, g: (0, g)),
        ],
        out_specs=pl.BlockSpec((seq, c), lambda b, g: (b, g)),
        out_shape=jax.ShapeDtypeStruct((n, ng * c), BF16),
        compiler_params=_params(("parallel", "arbitrary"), 48),
        name="pool_mixer",
    )(proj, w_pool, pool_scale.reshape(1, ng * c))


def _top2_route(logits):
    lane = lax.broadcasted_iota(I32, logits.shape, 1).astype(F32)
    m1 = jnp.max(logits, axis=-1, keepdims=True)
    i1 = jnp.min(jnp.where(logits == m1, lane, float(LANES)), axis=-1, keepdims=True)
    l2 = jnp.where(lane == i1, -jnp.inf, logits)
    m2 = jnp.max(l2, axis=-1, keepdims=True)
    i2 = jnp.min(jnp.where(l2 == m2, lane, float(LANES)), axis=-1, keepdims=True)
    e = jnp.exp(m2 - m1)
    w1 = 1.0 / (1.0 + e)
    w2 = e / (1.0 + e)
    return jnp.where(lane == 0.0, i1, jnp.where(lane == 1.0, i2, jnp.where(lane == 2.0, w1, jnp.where(lane == 3.0, w2, 0.0))))


def _post_kernel(mode, gathered, *refs):
    refs = list(refs)
    if gathered:
        pos_ref, pos_next_ref, y_hbm, route_in = refs[:4]
        refs = refs[4:]
    else:
        f_ref = refs.pop(0)
    x_ref, gate_ref, gpost_ref = refs[:3]
    refs = refs[3:]
    if mode != "last":
        gpre_ref, sh_ref, sc_ref = refs[:3]
        refs = refs[3:]
    if mode == "route":
        rw_ref, rb_ref = refs[:2]
        refs = refs[2:]
    xo_ref = refs.pop(0)
    if mode != "last":
        ho_ref = refs.pop(0)
    if mode == "route":
        ro_ref = refs.pop(0)

    if gathered:
        ybuf, sems = refs
        tr = x_ref.shape[0]
        i = pl.program_id(0)
        slot = i % 2

        def gather(idx_ref, s):
            def start(r, carry):
                for kk in range(2):
                    pltpu.make_async_copy(y_hbm.at[pl.ds(idx_ref[0, 0, 2 * r + kk], 1)],
                                          ybuf.at[s, kk, pl.ds(r, 1)], sems.at[s]).start()
                return carry
            lax.fori_loop(0, tr, start, 0, unroll=8)

        @pl.when(i == 0)
        def _():
            gather(pos_ref, 0)

        @pl.when(i + 1 < pl.num_programs(0))
        def _():
            gather(pos_next_ref, 1 - slot)

        pltpu.make_async_copy(ybuf.at[slot], ybuf.at[slot], sems.at[slot]).wait()
        rt = route_in[...]
        f = ybuf[slot, 0] * rt[:, 2:3] + ybuf[slot, 1] * rt[:, 3:4]
    else:
        f = f_ref[...].astype(F32)

    xn = x_ref[...] + gate_ref[0] * _rms(f, gpost_ref[...])
    xo_ref[...] = xn
    if mode == "last":
        return
    h = _rms(xn, gpre_ref[...]) * (1.0 + sc_ref[0]) + sh_ref[0]
    if mode == "next":
        ho_ref[...] = h.astype(BF16)
        return
    half = h.shape[1] // 2
    ho_ref[...] = _pack_pair(h[:, :half], h[:, half:])
    logits = jnp.dot(h.astype(BF16), rw_ref[...], preferred_element_type=F32) + rb_ref[...]
    ro_ref[...] = _top2_route(logits)


def _post(mode, f, x2, gate, g_post, seq, nxt=None, router=None, gather=None):
    n, d = x2.shape
    tr = min(256, seq)
    per = seq // tr
    row = lambda i: (i, 0)
    fix = lambda i: (0, 0)
    bat = lambda i: (i // per, 0, 0)
    in_specs, args, scratch = [], [], []
    if gather is not None:
        pos, y, route = gather
        last = n // tr - 1
        pos3 = pos.reshape(n // tr, 1, 2 * tr)
        in_specs += [
            pl.BlockSpec((1, 1, 2 * tr), lambda i: (i, 0, 0), memory_space=pltpu.SMEM),
            pl.BlockSpec((1, 1, 2 * tr), lambda i: (jnp.minimum(i + 1, last), 0, 0), memory_space=pltpu.SMEM),
            pl.BlockSpec(memory_space=pl.ANY),
            pl.BlockSpec((tr, LANES), row),
        ]
        args += [pos3, pos3, y, route]
        scratch = [pltpu.VMEM((2, 2, tr, d), F32), pltpu.SemaphoreType.DMA((2,))]
    else:
        in_specs.append(pl.BlockSpec((tr, d), row))
        args.append(f)
    in_specs += [pl.BlockSpec((tr, d), row), pl.BlockSpec((1, 1, d), bat), pl.BlockSpec((1, d), fix)]
    args += [x2, gate, g_post.reshape(1, d)]
    out_specs = [pl.BlockSpec((tr, d), row)]
    out_shape = [jax.ShapeDtypeStruct((n, d), F32)]
    if mode != "last":
        g_pre, shift, scale = nxt
        in_specs += [pl.BlockSpec((1, d), fix), pl.BlockSpec((1, 1, d), bat), pl.BlockSpec((1, 1, d), bat)]
        args += [g_pre.reshape(1, d), shift, scale]
    if mode == "next":
        out_specs.append(pl.BlockSpec((tr, d), row))
        out_shape.append(jax.ShapeDtypeStruct((n, d), BF16))
    if mode == "route":
        rw, rb = router
        in_specs += [pl.BlockSpec((d, LANES), fix), pl.BlockSpec((1, LANES), fix)]
        args += [rw, rb]
        out_specs += [pl.BlockSpec((tr, d // 2), row), pl.BlockSpec((tr, LANES), row)]
        out_shape += [jax.ShapeDtypeStruct((n, d // 2), U32), jax.ShapeDtypeStruct((n, LANES), F32)]
    return pl.pallas_call(
        functools.partial(_post_kernel, mode, gather is not None),
        grid=(n // tr,),
        in_specs=in_specs,
        out_specs=out_specs,
        out_shape=out_shape,
        scratch_shapes=scratch,
        compiler_params=_params(("arbitrary",), 56),
        name="post_" + mode + ("_gather" if gather is not None else ""),
    )(*args)


def _swiglu_step(h, wg_ref, wu_ref, wd_ref, o_ref, j, n_chunk):
    g = jnp.dot(h, wg_ref[...], preferred_element_type=F32)
    u = jnp.dot(h, wu_ref[...], preferred_element_type=F32)
    a = (g * _sigmoid(g) * u).astype(BF16)
    d = o_ref.shape[1]
    cw = d // n_chunk

    @pl.when(j == 0)
    def _():
        o_ref[...] = jnp.zeros_like(o_ref)

    for c in range(n_chunk):
        o_ref[:, c * cw:(c + 1) * cw] += jnp.dot(a, wd_ref[:, c * cw:(c + 1) * cw], preferred_element_type=F32)


def _ffn_kernel(h_ref, wg_ref, wu_ref, wd_ref, o_ref):
    _swiglu_step(h_ref[...], wg_ref, wu_ref, wd_ref, o_ref, pl.program_id(1), 4)


def _ffn(h, wg, wu, wd, tm, tf):
    n, d = h.shape
    dff = wg.shape[1]
    tm = min(tm, n)
    return pl.pallas_call(
        _ffn_kernel,
        grid=(n // tm, dff // tf),
        in_specs=[
            pl.BlockSpec((tm, d), lambda i, j: (i, 0)),
            pl.BlockSpec((d, tf), lambda i, j: (0, j)),
            pl.BlockSpec((d, tf), lambda i, j: (0, j)),
            pl.BlockSpec((tf, d), lambda i, j: (j, 0)),
        ],
        out_specs=pl.BlockSpec((tm, d), lambda i, j: (i, 0)),
        out_shape=jax.ShapeDtypeStruct((n, d), F32),
        compiler_params=_params(("parallel", "arbitrary"), 58),
        name="dense_swiglu",
    )(h, wg, wu, wd)


def _expert_kernel(te_ref, nt_ref, idx_ref, idx_next_ref, src_hbm, wg_ref, wu_ref, wd_ref, o_ref, xs_buf, h_scr, sem):
    t = pl.program_id(0)
    j = pl.program_id(1)
    n_live = nt_ref[0]
    live = t < n_live
    tm, half = xs_buf.shape

    def gather(iref):
        def start(r, carry):
            pltpu.make_async_copy(src_hbm.at[pl.ds(iref[0, 0, r], 1)], xs_buf.at[pl.ds(r, 1)], sem).start()
            return carry
        lax.fori_loop(0, tm, start, 0, unroll=8)

    @pl.when(j == 0)
    def _():
        @pl.when(t == 0)
        def _():
            gather(idx_ref)

        @pl.when(live)
        def _():
            pltpu.make_async_copy(xs_buf, xs_buf, sem).wait()
            hi, lo = _unpack_pair(xs_buf[...])
            h_scr[:, :half] = hi
            h_scr[:, half:] = lo

        @pl.when(t + 1 < n_live)
        def _():
            gather(idx_next_ref)

        @pl.when(jnp.logical_not(live))
        def _():
            o_ref[...] = jnp.zeros_like(o_ref)

    @pl.when(live)
    def _():
        _swiglu_step(h_scr[...], wg_ref.at[0], wu_ref.at[0], wd_ref.at[0], o_ref, j, 4)


def _experts(src, src_tok, tile_expert, n_tiles, wg, wu, wd, tm, tf):
    half = src.shape[1]
    d = 2 * half
    rows = src_tok.shape[0]
    dff = wg.shape[2]
    nj = dff // tf
    n_t = rows // tm
    idx3 = src_tok.reshape(n_t, 1, tm)

    def chunk(t, j, nt):
        return jnp.where(t < nt[0], j, nj - 1)

    return pl.pallas_call(
        _expert_kernel,
        grid_spec=pltpu.PrefetchScalarGridSpec(
            num_scalar_prefetch=2,
            grid=(n_t, nj),
            in_specs=[
                pl.BlockSpec((1, 1, tm), lambda t, j, te, nt: (t, 0, 0), memory_space=pltpu.SMEM),
                pl.BlockSpec((1, 1, tm), lambda t, j, te, nt: (jnp.minimum(t + 1, n_t - 1), 0, 0),
                             memory_space=pltpu.SMEM),
                pl.BlockSpec(memory_space=pl.ANY),
                pl.BlockSpec((1, d, tf), lambda t, j, te, nt: (te[t], 0, chunk(t, j, nt))),
                pl.BlockSpec((1, d, tf), lambda t, j, te, nt: (te[t], 0, chunk(t, j, nt))),
                pl.BlockSpec((1, tf, d), lambda t, j, te, nt: (te[t], chunk(t, j, nt), 0)),
            ],
            out_specs=pl.BlockSpec((tm, d), lambda t, j, te, nt: (t, 0)),
            scratch_shapes=[pltpu.VMEM((tm, half), U32), pltpu.VMEM((tm, d), BF16), pltpu.SemaphoreType.DMA(())],
        ),
        out_shape=jax.ShapeDtypeStruct((rows, d), F32),
        compiler_params=_params(("arbitrary", "arbitrary"), 58),
        name="expert_swiglu",
    )(tile_expert, n_tiles, idx3, idx3, src, wg, wu, wd)


def _moe_plan(top_idx, tm):
    n = top_idx.shape[0]
    p = 2 * n
    flat = top_idx.reshape(p)
    experts = jnp.arange(N_EXPERTS, dtype=I32)
    onehot = (flat[:, None] == experts[None, :]).astype(I32)
    csum = jnp.cumsum(onehot, axis=0)
    rank = jnp.sum(onehot * csum, axis=1) - 1
    counts = csum[-1]
    tiles_per = (counts + (tm - 1)) // tm
    tile_end = jnp.cumsum(tiles_per)
    tile_start = tile_end - tiles_per
    pos = (tile_start * tm)[flat] + rank
    n_tiles = tile_end[-1:]
    t_max = p // tm + N_EXPERTS
    t_ids = jnp.arange(t_max, dtype=I32)
    te = jnp.sum((t_ids[:, None] >= tile_end[None, :]).astype(I32), axis=1)
    last_e = jnp.max(jnp.where(tiles_per > 0, experts, 0))
    te = jnp.minimum(te, last_e)
    src_tok = jnp.zeros((t_max * tm,), I32).at[pos].set(jnp.arange(p, dtype=I32) // 2)
    return pos.astype(I32), src_tok, te.astype(I32), n_tiles.astype(I32)


def kernel(x, c, w_ada, b_ada, norm_pre_mix, norm_post_mix, norm_pre_ffn, norm_post_ffn, w_in, sinks, w_pool,
           pool_scale, w_out, ffn_w_gate, ffn_w_up, ffn_w_down, router_w, router_b, moe_w_gate, moe_w_up, moe_w_down):
    batch, seq, d = x.shape
    depth = w_ada.shape[0]
    n = batch * seq
    d_attn = d // 2
    n_heads = sinks.shape[1]
    d_kv = (n_heads // GQA_GROUP) * HEAD_DIM
    u_col0 = d_attn + 2 * d_kv
    assert n_heads * HEAD_DIM == d_attn and seq % BLOCK == 0

    x2 = x.reshape(n, d)
    mod = _ada(c, w_ada, b_ada)

    def mods(l):
        return [mod[l, :, k * d:(k + 1) * d].reshape(batch, 1, d) for k in range(N_MOD)]

    shift1, scale1, gate1, shift2, scale2, gate2 = mods(0)
    h = _prenorm(x2, norm_pre_mix[0], shift1, scale1, seq)
    for l in range(depth):
        proj = _mm(h, w_in[l].astype(BF16), BF16, 1024, 512)
        attn = _attention(proj, sinks[l], batch, seq, d_attn, d_kv)
        pool = _pool(proj, w_pool[l].astype(BF16), pool_scale[l], batch, seq, u_col0)
        mix = _mm2(attn, pool, w_out[l].astype(BF16), BF16, 1024, 512)
        nxt = (norm_pre_ffn[l], shift2, scale2)
        if l % 2 == 0:
            i = l // 2
            x2, h2 = _post("next", mix, x2, gate1, norm_post_mix[l], seq, nxt=nxt)
            pad = -ffn_w_gate.shape[2] % FFN_CHUNK
            wg = jnp.pad(ffn_w_gate[i].astype(BF16), ((0, 0), (0, pad)))
            wu = jnp.pad(ffn_w_up[i].astype(BF16), ((0, 0), (0, pad)))
            wd = jnp.pad(ffn_w_down[i].astype(BF16), ((0, pad), (0, 0)))
            f = _ffn(h2, wg, wu, wd, FFN_ROWS, FFN_CHUNK)
            gather = None
        else:
            i = l // 2
            rw = jnp.zeros((d, LANES), BF16).at[:, :N_EXPERTS].set(router_w[i].astype(BF16))
            rb = jnp.full((1, LANES), NEG_INF, F32).at[0, :N_EXPERTS].set(router_b[i])
            x2, h2p, route = _post("route", mix, x2, gate1, norm_post_mix[l], seq, nxt=nxt, router=(rw, rb))
            pos, src_tok, tile_expert, n_tiles = _moe_plan(route[:, :2].astype(I32), FFN_ROWS)
            y = _experts(h2p, src_tok, tile_expert, n_tiles, moe_w_gate[i].astype(BF16), moe_w_up[i].astype(BF16),
                         moe_w_down[i].astype(BF16), FFN_ROWS, FFN_CHUNK)
            f, gather = None, (pos, y, route)
        if l + 1 < depth:
            shift1, scale1, gate1n, shift2n, scale2n, gate2n = mods(l + 1)
            x2, h = _post("next", f, x2, gate2, norm_post_ffn[l], seq,
                          nxt=(norm_pre_mix[l + 1], shift1, scale1), gather=gather)
            gate1, shift2, scale2, gate2 = gate1n, shift2n, scale2n, gate2n
        else:
            (x2,) = _post("last", f, x2, gate2, norm_post_ffn[l], seq, gather=gather)
    return x2.reshape(batch, seq, d)
```

```python
import functools

import jax
import jax.numpy as jnp
from jax import lax
from jax.experimental import pallas as pl
from jax.experimental.pallas import tpu as pltpu

F32 = jnp.float32
BF16 = jnp.bfloat16
U32 = jnp.uint32
I32 = jnp.int32

EPS = 1e-6
NEG_INF = -1e30
BLOCK = 128
HEAD_DIM = 64
GQA_GROUP = 8
POOL_WINDOWS = (2, 4, 8, 16)
N_EXPERTS = 8
N_MOD = 6
LANES = 128
FFN_ROWS = 512
FFN_CHUNK = 512
BF16_SUBLANES = 16
RIDER_BLOCK_BYTES = 4 << 20

MIB = 1 << 20


def _params(sem, vmem_mib):
    return pltpu.CompilerParams(dimension_semantics=sem, vmem_limit_bytes=vmem_mib * MIB)


def _sigmoid(x):
    return 1.0 / (1.0 + jnp.exp(-x))


def _rms(x, g):
    return x * lax.rsqrt(jnp.mean(x * x, axis=-1, keepdims=True) + EPS) * g


def _pack_pair(a, b):
    ua = pltpu.bitcast(a.astype(BF16).astype(F32), U32)
    ub = pltpu.bitcast(b.astype(BF16).astype(F32), U32)
    return ua | (ub >> 16)


def _unpack_pair(p):
    hi = pltpu.bitcast(p & jnp.uint32(0xFFFF0000), F32).astype(BF16)
    lo = pltpu.bitcast(p << 16, F32).astype(BF16)
    return hi, lo


def _plan_rider(w, grid):
    if w is None:
        return None
    rows, cols = w.shape
    steps = grid[0] * grid[1]
    for rb in range(BF16_SUBLANES, rows + 1, BF16_SUBLANES):
        if rows % rb == 0 and rows // rb <= steps:
            return (w, rb) if rb * cols * 4 <= RIDER_BLOCK_BYTES else None
    return None


def _ridden(outs, w, rider):
    if w is None:
        return outs[0], None
    return outs[0], (outs[1] if rider is not None else w.astype(BF16))


def _add_rider(kernel_fn, args, in_specs, out_specs, out_shape, grid, rider):
    if rider is None:
        return kernel_fn, list(args)
    w, rb = rider
    rows, cols = w.shape
    nblk = rows // rb
    n1 = grid[1]
    blk = lambda a, b: (jnp.minimum(a * n1 + b, nblk - 1), 0)
    n_in, n_out = len(in_specs), len(out_specs)
    in_specs.append(pl.BlockSpec((rb, cols), blk))
    out_specs.append(pl.BlockSpec((rb, cols), blk))
    out_shape.append(jax.ShapeDtypeStruct((rows, cols), BF16))

    def body(*refs):
        ins, src = refs[:n_in], refs[n_in]
        outs, dst = refs[n_in + 1:n_in + 1 + n_out], refs[n_in + 1 + n_out]
        kernel_fn(*ins, *outs, *refs[n_in + 2 + n_out:])
        dst[...] = src[...].astype(BF16)

    return body, list(args) + [w]


def _ada_kernel(c_ref, w_ref, b_ref, o_ref):
    c = c_ref[...]
    ca = (c * _sigmoid(c)).astype(BF16)
    o_ref[0] = jnp.dot(ca, w_ref[0].astype(BF16), preferred_element_type=F32) + b_ref[0]


def _ada(c, w_ada, b_ada):
    depth, d, nm = w_ada.shape
    b = c.shape[0]
    tn = 512
    return pl.pallas_call(
        _ada_kernel,
        grid=(depth, nm // tn),
        in_specs=[
            pl.BlockSpec((b, d), lambda l, j: (0, 0)),
            pl.BlockSpec((1, d, tn), lambda l, j: (l, 0, j)),
            pl.BlockSpec((1, 1, tn), lambda l, j: (l, 0, j)),
        ],
        out_specs=pl.BlockSpec((1, b, tn), lambda l, j: (l, 0, j)),
        out_shape=jax.ShapeDtypeStruct((depth, b, nm), F32),
        compiler_params=_params(("parallel", "parallel"), 40),
        name="ada_mod",
    )(c, w_ada, b_ada.reshape(depth, 1, nm))


def _prenorm_kernel(x_ref, g_ref, sh_ref, sc_ref, o_ref):
    h = _rms(x_ref[...], g_ref[...]) * (1.0 + sc_ref[0]) + sh_ref[0]
    o_ref[...] = h.astype(BF16)


def _prenorm(x2, g, shift, scale, seq):
    n, d = x2.shape
    tr = min(256, seq)
    per = seq // tr
    return pl.pallas_call(
        _prenorm_kernel,
        grid=(n // tr,),
        in_specs=[
            pl.BlockSpec((tr, d), lambda i: (i, 0)),
            pl.BlockSpec((1, d), lambda i: (0, 0)),
            pl.BlockSpec((1, 1, d), lambda i: (i // per, 0, 0)),
            pl.BlockSpec((1, 1, d), lambda i: (i // per, 0, 0)),
        ],
        out_specs=pl.BlockSpec((tr, d), lambda i: (i, 0)),
        out_shape=jax.ShapeDtypeStruct((n, d), BF16),
        compiler_params=_params(("parallel",), 40),
        name="prenorm",
    )(x2, g.reshape(1, d), shift, scale)


def _mm_kernel(a_ref, w_ref, o_ref):
    o_ref[...] = jnp.dot(a_ref[...], w_ref[...], preferred_element_type=F32).astype(o_ref.dtype)


def _mm(a, w, out_dtype, tm, tn, ride=None):
    m, k = a.shape
    n = w.shape[1]
    tm = min(tm, m)
    grid = (m // tm, n // tn)
    rider = _plan_rider(ride, grid)
    in_specs = [
        pl.BlockSpec((tm, k), lambda i, j: (i, 0)),
        pl.BlockSpec((k, tn), lambda i, j: (0, j)),
    ]
    out_specs = [pl.BlockSpec((tm, tn), lambda i, j: (i, j))]
    out_shape = [jax.ShapeDtypeStruct((m, n), out_dtype)]
    body, args = _add_rider(_mm_kernel, [a, w], in_specs, out_specs, out_shape, grid, rider)
    outs = pl.pallas_call(
        body,
        grid=grid,
        in_specs=in_specs,
        out_specs=out_specs,
        out_shape=out_shape,
        compiler_params=_params(("parallel", "arbitrary"), 48),
        name="proj_in",
    )(*args)
    return _ridden(outs, ride, rider)


def _mm2_kernel(a1_ref, a2_ref, w1_ref, w2_ref, o_ref):
    acc = jnp.dot(a1_ref[...], w1_ref[...], preferred_element_type=F32)
    acc = acc + jnp.dot(a2_ref[...], w2_ref[...], preferred_element_type=F32)
    o_ref[...] = acc.astype(o_ref.dtype)


def _mm2(a1, a2, w, out_dtype, tm, tn, ride=None):
    m, k1 = a1.shape
    n = w.shape[1]
    tm = min(tm, m)
    grid = (m // tm, n // tn)
    rider = _plan_rider(ride, grid)
    in_specs = [
        pl.BlockSpec((tm, k1), lambda i, j: (i, 0)),
        pl.BlockSpec((tm, k1), lambda i, j: (i, 0)),
        pl.BlockSpec((k1, tn), lambda i, j: (0, j)),
        pl.BlockSpec((k1, tn), lambda i, j: (1, j)),
    ]
    out_specs = [pl.BlockSpec((tm, tn), lambda i, j: (i, j))]
    out_shape = [jax.ShapeDtypeStruct((m, n), out_dtype)]
    body, args = _add_rider(_mm2_kernel, [a1, a2, w, w], in_specs, out_specs, out_shape, grid, rider)
    outs = pl.pallas_call(
        body,
        grid=grid,
        in_specs=in_specs,
        out_specs=out_specs,
        out_shape=out_shape,
        compiler_params=_params(("parallel", "arbitrary"), 48),
        name="proj_out",
    )(*args)
    return _ridden(outs, ride, rider)


def _attn_kernel(slopes, sinks_ref, q_ref, kp_ref, kc_ref, vp_ref, vc_ref, o_ref):
    n = pl.program_id(1)
    k = jnp.concatenate([kp_ref[...], kc_ref[...]], axis=0)
    v = jnp.concatenate([vp_ref[...], vc_ref[...]], axis=0)
    i = lax.broadcasted_iota(I32, (BLOCK, 2 * BLOCK), 0)
    j = lax.broadcasted_iota(I32, (BLOCK, 2 * BLOCK), 1)
    dist = i + BLOCK - j
    valid = (dist >= 0) & (dist < BLOCK) & ((j >= BLOCK) | (n > 0))
    distf = dist.astype(F32)
    n_heads = len(slopes)
    outs = []
    for h in range(n_heads):
        kv = h // GQA_GROUP
        kk = k[:, kv * HEAD_DIM:(kv + 1) * HEAD_DIM]
        vv = v[:, kv * HEAD_DIM:(kv + 1) * HEAD_DIM]
        qh = q_ref[:, h * HEAD_DIM:(h + 1) * HEAD_DIM] * jnp.asarray(HEAD_DIM ** -0.5, BF16)
        s = lax.dot_general(qh, kk, (((1,), (1,)), ((), ())), preferred_element_type=F32)
        s = jnp.where(valid, s - slopes[h] * distf, NEG_INF)
        sink = sinks_ref[h]
        m = jnp.maximum(jnp.max(s, axis=-1, keepdims=True), sink)
        p = jnp.exp(s - m)
        denom = jnp.sum(p, axis=-1, keepdims=True) + jnp.exp(sink - m)
        o = jnp.dot(p.astype(BF16), vv, preferred_element_type=F32) / denom
        outs.append(o.astype(BF16))
    o_ref[...] = jnp.concatenate(outs, axis=-1)


def _attention(proj, sinks, batch, seq, d_attn, d_kv, ride=None):
    n = proj.shape[0]
    nb = seq // BLOCK
    n_heads = d_attn // HEAD_DIM
    slopes = tuple(2.0 ** (-8.0 * (h + 1) / n_heads) for h in range(n_heads))
    kcol = d_attn // d_kv
    vcol = kcol + 1

    def cur(col):
        return lambda b, i: (b * nb + i, col)

    def prev(col):
        return lambda b, i: (b * nb + jnp.maximum(i - 1, 0), col)

    grid = (batch, nb)
    rider = _plan_rider(ride, grid)
    in_specs = [
        pl.BlockSpec(memory_space=pltpu.SMEM),
        pl.BlockSpec((BLOCK, d_attn), cur(0)),
        pl.BlockSpec((BLOCK, d_kv), prev(kcol)),
        pl.BlockSpec((BLOCK, d_kv), cur(kcol)),
        pl.BlockSpec((BLOCK, d_kv), prev(vcol)),
        pl.BlockSpec((BLOCK, d_kv), cur(vcol)),
    ]
    out_specs = [pl.BlockSpec((BLOCK, d_attn), cur(0))]
    out_shape = [jax.ShapeDtypeStruct((n, d_attn), BF16)]
    body, args = _add_rider(functools.partial(_attn_kernel, slopes), [sinks, proj, proj, proj, proj, proj],
                            in_specs, out_specs, out_shape, grid, rider)
    outs = pl.pallas_call(
        body,
        grid=grid,
        in_specs=in_specs,
        out_specs=out_specs,
        out_shape=out_shape,
        compiler_params=_params(("parallel", "arbitrary"), 40),
        name="swa_attention",
    )(*args)
    return _ridden(outs, ride, rider)


def _pool_kernel(u_ref, w_ref, ps_ref, o_ref):
    g = pl.program_id(1)
    u = u_ref[...].astype(F32)
    row = lax.broadcasted_iota(I32, u.shape, 0)

    def shifted(x, k):
        return jnp.where(row >= k, pltpu.roll(x, k, 0), 0.0)

    s2 = u + shifted(u, 1)
    s4 = s2 + shifted(s2, 2)
    s8 = s4 + shifted(s4, 4)
    s16 = s8 + shifted(s8, 8)
    wsum = jnp.where(g == 0, s2, jnp.where(g == 1, s4, jnp.where(g == 2, s8, s16)))
    win = jnp.left_shift(jnp.int32(POOL_WINDOWS[0]), g)
    cnt = jnp.minimum(row + 1, win).astype(F32)
    pooled = wsum / cnt - u
    y = jnp.dot(pooled.astype(BF16), w_ref[0], preferred_element_type=F32)
    o_ref[...] = (y * ps_ref[...]).astype(BF16)


def _pool(proj, w_pool, pool_scale, batch, seq, u_col0):
    n = proj.shape[0]
    ng, c, _ = w_pool.shape
    assert POOL_WINDOWS == tuple(POOL_WINDOWS[0] << g for g in range(ng))
    col0 = u_col0 // c
    return pl.pallas_call(
        _pool_kernel,
        grid=(batch, ng),
        in_specs=[
            pl.BlockSpec((seq, c), lambda b, g: (b, col0 + g)),
            pl.BlockSpec((1, c, c), lambda b, g: (g, 0, 0)),
            pl.BlockSpec((1, c), lambda b, g: (0, g)),
        ],
        out_specs=pl.BlockSpec((seq, c), lambda b, g: (b, g)),
        out_shape=jax.ShapeDtypeStruct((n, ng * c), BF16),
        compiler_params=_params(("parallel", "arbitrary"), 48),
        name="pool_mixer",
    )(proj, w_pool, pool_scale.reshape(1, ng * c))


def _top2_route(logits):
    lane = lax.broadcasted_iota(I32, logits.shape, 1).astype(F32)
    m1 = jnp.max(logits, axis=-1, keepdims=True)
    i1 = jnp.min(jnp.where(logits == m1, lane, float(LANES)), axis=-1, keepdims=True)
    l2 = jnp.where(lane == i1, -jnp.inf, logits)
    m2 = jnp.max(l2, axis=-1, keepdims=True)
    i2 = jnp.min(jnp.where(l2 == m2, lane, float(LANES)), axis=-1, keepdims=True)
    e = jnp.exp(m2 - m1)
    w1 = 1.0 / (1.0 + e)
    w2 = e / (1.0 + e)
    return jnp.where(lane == 0.0, i1, jnp.where(lane == 1.0, i2, jnp.where(lane == 2.0, w1, jnp.where(lane == 3.0, w2, 0.0))))


def _post_kernel(mode, gathered, *refs):
    refs = list(refs)
    if gathered:
        pos_ref, pos_next_ref, y_hbm, route_in = refs[:4]
        refs = refs[4:]
    else:
        f_ref = refs.pop(0)
    x_ref, gate_ref, gpost_ref = refs[:3]
    refs = refs[3:]
    if mode != "last":
        gpre_ref, sh_ref, sc_ref = refs[:3]
        refs = refs[3:]
    if mode == "route":
        rw_ref, rb_ref = refs[:2]
        refs = refs[2:]
    xo_ref = refs.pop(0)
    if mode != "last":
        ho_ref = refs.pop(0)
    if mode == "route":
        ro_ref = refs.pop(0)

    if gathered:
        ybuf, sems = refs
        tr = x_ref.shape[0]
        i = pl.program_id(0)
        slot = i % 2

        def gather(idx_ref, s):
            def start(r, carry):
                for kk in range(2):
                    pltpu.make_async_copy(y_hbm.at[pl.ds(idx_ref[0, 0, 2 * r + kk], 1)],
                                          ybuf.at[s, kk, pl.ds(r, 1)], sems.at[s]).start()
                return carry
            lax.fori_loop(0, tr, start, 0, unroll=8)

        @pl.when(i == 0)
        def _():
            gather(pos_ref, 0)

        @pl.when(i + 1 < pl.num_programs(0))
        def _():
            gather(pos_next_ref, 1 - slot)

        pltpu.make_async_copy(ybuf.at[slot], ybuf.at[slot], sems.at[slot]).wait()
        rt = route_in[...]
        f = ybuf[slot, 0] * rt[:, 2:3] + ybuf[slot, 1] * rt[:, 3:4]
    else:
        f = f_ref[...].astype(F32)

    xn = x_ref[...] + gate_ref[0] * _rms(f, gpost_ref[...])
    xo_ref[...] = xn
    if mode == "last":
        return
    h = _rms(xn, gpre_ref[...]) * (1.0 + sc_ref[0]) + sh_ref[0]
    if mode == "next":
        ho_ref[...] = h.astype(BF16)
        return
    half = h.shape[1] // 2
    ho_ref[...] = _pack_pair(h[:, :half], h[:, half:])
    logits = jnp.dot(h.astype(BF16), rw_ref[...], preferred_element_type=F32) + rb_ref[...]
    ro_ref[...] = _top2_route(logits)


def _post(mode, f, x2, gate, g_post, seq, nxt=None, router=None, gather=None):
    n, d = x2.shape
    tr = min(256, seq)
    per = seq // tr
    row = lambda i: (i, 0)
    fix = lambda i: (0, 0)
    bat = lambda i: (i // per, 0, 0)
    in_specs, args, scratch = [], [], []
    if gather is not None:
        pos, y, route = gather
        last = n // tr - 1
        pos3 = pos.reshape(n // tr, 1, 2 * tr)
        in_specs += [
            pl.BlockSpec((1, 1, 2 * tr), lambda i: (i, 0, 0), memory_space=pltpu.SMEM),
            pl.BlockSpec((1, 1, 2 * tr), lambda i: (jnp.minimum(i + 1, last), 0, 0), memory_space=pltpu.SMEM),
            pl.BlockSpec(memory_space=pl.ANY),
            pl.BlockSpec((tr, LANES), row),
        ]
        args += [pos3, pos3, y, route]
        scratch = [pltpu.VMEM((2, 2, tr, d), F32), pltpu.SemaphoreType.DMA((2,))]
    else:
        in_specs.append(pl.BlockSpec((tr, d), row))
        args.append(f)
    in_specs += [pl.BlockSpec((tr, d), row), pl.BlockSpec((1, 1, d), bat), pl.BlockSpec((1, d), fix)]
    args += [x2, gate, g_post.reshape(1, d)]
    out_specs = [pl.BlockSpec((tr, d), row)]
    out_shape = [jax.ShapeDtypeStruct((n, d), F32)]
    if mode != "last":
        g_pre, shift, scale = nxt
        in_specs += [pl.BlockSpec((1, d), fix), pl.BlockSpec((1, 1, d), bat), pl.BlockSpec((1, 1, d), bat)]
        args += [g_pre.reshape(1, d), shift, scale]
    if mode == "next":
        out_specs.append(pl.BlockSpec((tr, d), row))
        out_shape.append(jax.ShapeDtypeStruct((n, d), BF16))
    if mode == "route":
        rw, rb = router
        in_specs += [pl.BlockSpec((d, LANES), fix), pl.BlockSpec((1, LANES), fix)]
        args += [rw, rb]
        out_specs += [pl.BlockSpec((tr, d // 2), row), pl.BlockSpec((tr, LANES), row)]
        out_shape += [jax.ShapeDtypeStruct((n, d // 2), U32), jax.ShapeDtypeStruct((n, LANES), F32)]
    return pl.pallas_call(
        functools.partial(_post_kernel, mode, gather is not None),
        grid=(n // tr,),
        in_specs=in_specs,
        out_specs=out_specs,
        out_shape=out_shape,
        scratch_shapes=scratch,
        compiler_params=_params(("arbitrary",), 56),
        name="post_" + mode + ("_gather" if gather is not None else ""),
    )(*args)


def _swiglu_step(h, wg_ref, wu_ref, wd_ref, o_ref, j, n_chunk, skip=None):
    g = jnp.dot(h, wg_ref[...], preferred_element_type=F32)
    u = jnp.dot(h, wu_ref[...], preferred_element_type=F32)
    a = g * _sigmoid(g) * u
    if skip is not None:
        a = jnp.where(lax.broadcasted_iota(I32, a.shape, 1) >= skip, a, 0.0)
    a = a.astype(BF16)
    d = o_ref.shape[1]
    cw = d // n_chunk

    @pl.when(j == 0)
    def _():
        o_ref[...] = jnp.zeros_like(o_ref)

    for c in range(n_chunk):
        o_ref[:, c * cw:(c + 1) * cw] += jnp.dot(a, wd_ref[:, c * cw:(c + 1) * cw], preferred_element_type=F32)


def _ffn_kernel(dff, h_ref, wg_ref, wu_ref, wd_ref, o_ref):
    j = pl.program_id(1)
    tf = wg_ref.shape[1]
    skip = j * tf - jnp.minimum(j * tf, dff - tf)
    _swiglu_step(h_ref[...], wg_ref, wu_ref, wd_ref, o_ref, j, 4, skip=skip)


def _ffn(h, wg, wu, wd, tm, tf, ride=None):
    n, d = h.shape
    dff = wg.shape[1]
    tm = min(tm, n)
    grid = (n // tm, pl.cdiv(dff, tf))
    rider = _plan_rider(ride, grid)
    assert (dff - tf) % LANES == 0
    start = lambda j: pl.multiple_of(jnp.minimum(j * tf, dff - tf), LANES)
    in_specs = [
        pl.BlockSpec((tm, d), lambda i, j: (i, 0)),
        pl.BlockSpec((pl.Element(d), pl.Element(tf)), lambda i, j: (0, start(j))),
        pl.BlockSpec((pl.Element(d), pl.Element(tf)), lambda i, j: (0, start(j))),
        pl.BlockSpec((pl.Element(tf), pl.Element(d)), lambda i, j: (start(j), 0)),
    ]
    out_specs = [pl.BlockSpec((tm, d), lambda i, j: (i, 0))]
    out_shape = [jax.ShapeDtypeStruct((n, d), F32)]
    body, args = _add_rider(functools.partial(_ffn_kernel, dff), [h, wg, wu, wd], in_specs, out_specs, out_shape,
                            grid, rider)
    outs = pl.pallas_call(
        body,
        grid=grid,
        in_specs=in_specs,
        out_specs=out_specs,
        out_shape=out_shape,
        compiler_params=_params(("parallel", "arbitrary"), 58),
        name="dense_swiglu",
    )(*args)
    return _ridden(outs, ride, rider)


def _expert_kernel(te_ref, nt_ref, idx_ref, idx_next_ref, src_hbm, wg_ref, wu_ref, wd_ref, o_ref, xs_buf, h_scr, sem):
    t = pl.program_id(0)
    j = pl.program_id(1)
    n_live = nt_ref[0]
    live = t < n_live
    tm, half = xs_buf.shape

    def gather(iref):
        def start(r, carry):
            pltpu.make_async_copy(src_hbm.at[pl.ds(iref[0, 0, r], 1)], xs_buf.at[pl.ds(r, 1)], sem).start()
            return carry
        lax.fori_loop(0, tm, start, 0, unroll=8)

    @pl.when(j == 0)
    def _():
        @pl.when(t == 0)
        def _():
            gather(idx_ref)

        @pl.when(live)
        def _():
            pltpu.make_async_copy(xs_buf, xs_buf, sem).wait()
            hi, lo = _unpack_pair(xs_buf[...])
            h_scr[:, :half] = hi
            h_scr[:, half:] = lo

        @pl.when(t + 1 < n_live)
        def _():
            gather(idx_next_ref)

        @pl.when(jnp.logical_not(live))
        def _():
            o_ref[...] = jnp.zeros_like(o_ref)

    @pl.when(live)
    def _():
        _swiglu_step(h_scr[...], wg_ref.at[0], wu_ref.at[0], wd_ref.at[0], o_ref, j, 4)


def _experts(src, src_tok, tile_expert, n_tiles, wg, wu, wd, tm, tf):
    half = src.shape[1]
    d = 2 * half
    rows = src_tok.shape[0]
    dff = wg.shape[2]
    nj = dff // tf
    n_t = rows // tm
    idx3 = src_tok.reshape(n_t, 1, tm)

    def chunk(t, j, nt):
        return jnp.where(t < nt[0], j, nj - 1)

    return pl.pallas_call(
        _expert_kernel,
        grid_spec=pltpu.PrefetchScalarGridSpec(
            num_scalar_prefetch=2,
            grid=(n_t, nj),
            in_specs=[
                pl.BlockSpec((1, 1, tm), lambda t, j, te, nt: (t, 0, 0), memory_space=pltpu.SMEM),
                pl.BlockSpec((1, 1, tm), lambda t, j, te, nt: (jnp.minimum(t + 1, n_t - 1), 0, 0),
                             memory_space=pltpu.SMEM),
                pl.BlockSpec(memory_space=pl.ANY),
                pl.BlockSpec((1, d, tf), lambda t, j, te, nt: (te[t], 0, chunk(t, j, nt))),
                pl.BlockSpec((1, d, tf), lambda t, j, te, nt: (te[t], 0, chunk(t, j, nt))),
                pl.BlockSpec((1, tf, d), lambda t, j, te, nt: (te[t], chunk(t, j, nt), 0)),
            ],
            out_specs=pl.BlockSpec((tm, d), lambda t, j, te, nt: (t, 0)),
            scratch_shapes=[pltpu.VMEM((tm, half), U32), pltpu.VMEM((tm, d), BF16), pltpu.SemaphoreType.DMA(())],
        ),
        out_shape=jax.ShapeDtypeStruct((rows, d), F32),
        compiler_params=_params(("arbitrary", "arbitrary"), 58),
        name="expert_swiglu",
    )(tile_expert, n_tiles, idx3, idx3, src, wg, wu, wd)


def _moe_plan(top_idx, tm):
    n = top_idx.shape[0]
    p = 2 * n
    flat = top_idx.reshape(p)
    experts = jnp.arange(N_EXPERTS, dtype=I32)
    onehot = (flat[:, None] == experts[None, :]).astype(I32)
    csum = jnp.cumsum(onehot, axis=0)
    rank = jnp.sum(onehot * csum, axis=1) - 1
    counts = csum[-1]
    tiles_per = (counts + (tm - 1)) // tm
    tile_end = jnp.cumsum(tiles_per)
    tile_start = tile_end - tiles_per
    pos = (tile_start * tm)[flat] + rank
    n_tiles = tile_end[-1:]
    t_max = p // tm + N_EXPERTS
    t_ids = jnp.arange(t_max, dtype=I32)
    te = jnp.sum((t_ids[:, None] >= tile_end[None, :]).astype(I32), axis=1)
    last_e = jnp.max(jnp.where(tiles_per > 0, experts, 0))
    te = jnp.minimum(te, last_e)
    src_tok = jnp.zeros((t_max * tm,), I32).at[pos].set(jnp.arange(p, dtype=I32) // 2)
    return pos.astype(I32), src_tok, te.astype(I32), n_tiles.astype(I32)


def kernel(x, c, w_ada, b_ada, norm_pre_mix, norm_post_mix, norm_pre_ffn, norm_post_ffn, w_in, sinks, w_pool,
           pool_scale, w_out, ffn_w_gate, ffn_w_up, ffn_w_down, router_w, router_b, moe_w_gate, moe_w_up, moe_w_down):
    batch, seq, d = x.shape
    depth = w_ada.shape[0]
    n = batch * seq
    d_attn = d // 2
    n_heads = sinks.shape[1]
    d_kv = (n_heads // GQA_GROUP) * HEAD_DIM
    u_col0 = d_attn + 2 * d_kv
    assert n_heads * HEAD_DIM == d_attn and seq % BLOCK == 0

    x2 = x.reshape(n, d)
    mod = _ada(c, w_ada, b_ada)

    def mods(l):
        return [mod[l, :, k * d:(k + 1) * d].reshape(batch, 1, d) for k in range(N_MOD)]

    shift1, scale1, gate1, shift2, scale2, gate2 = mods(0)
    h = _prenorm(x2, norm_pre_mix[0], shift1, scale1, seq)
    flat = lambda w: w.reshape(-1, w.shape[-1])
    moe_next = lambda l: (l + 1) // 2 if (l + 1 < depth and (l + 1) % 2 == 1) else None
    ready = {}

    def take(key, w):
        return ready.pop(key).reshape(w.shape) if key in ready else w.astype(BF16)

    for l in range(depth):
        i = l // 2
        dense = l % 2 == 0
        nm = moe_next(l)
        proj, ready["ffn_wg", i] = _mm(h, w_in[l].astype(BF16), BF16, 1024, 512,
                                       ride=ffn_w_gate[i] if dense else None)
        attn, ready["moe_wg" if dense else "moe_wu", nm if dense else i] = _attention(
            proj, sinks[l], batch, seq, d_attn, d_kv,
            ride=(flat(moe_w_gate[nm]) if nm is not None else None) if dense else flat(moe_w_up[i]))
        pool = _pool(proj, w_pool[l].astype(BF16), pool_scale[l], batch, seq, u_col0)
        mix, ready["ffn_wu", i] = _mm2(attn, pool, w_out[l].astype(BF16), BF16, 1024, 512,
                                       ride=ffn_w_up[i] if dense else None)
        ready = {k: v for k, v in ready.items() if v is not None}
        nxt = (norm_pre_ffn[l], shift2, scale2)
        if dense:
            x2, h2 = _post("next", mix, x2, gate1, norm_post_mix[l], seq, nxt=nxt)
            f, wd_next = _ffn(h2, take(("ffn_wg", i), ffn_w_gate[i]), take(("ffn_wu", i), ffn_w_up[i]),
                              ffn_w_down[i].astype(BF16), FFN_ROWS, FFN_CHUNK,
                              ride=flat(moe_w_down[nm]) if nm is not None else None)
            if wd_next is not None:
                ready["moe_wd", nm] = wd_next
            gather = None
        else:
            rw = jnp.zeros((d, LANES), BF16).at[:, :N_EXPERTS].set(router_w[i].astype(BF16))
            rb = jnp.full((1, LANES), NEG_INF, F32).at[0, :N_EXPERTS].set(router_b[i])
            x2, h2p, route = _post("route", mix, x2, gate1, norm_post_mix[l], seq, nxt=nxt, router=(rw, rb))
            pos, src_tok, tile_expert, n_tiles = _moe_plan(route[:, :2].astype(I32), FFN_ROWS)
            y = _experts(h2p, src_tok, tile_expert, n_tiles, take(("moe_wg", i), moe_w_gate[i]),
                         take(("moe_wu", i), moe_w_up[i]), take(("moe_wd", i), moe_w_down[i]), FFN_ROWS, FFN_CHUNK)
            f, gather = None, (pos, y, route)
        if l + 1 < depth:
            shift1, scale1, gate1n, shift2n, scale2n, gate2n = mods(l + 1)
            x2, h = _post("next", f, x2, gate2, norm_post_ffn[l], seq,
                          nxt=(norm_pre_mix[l + 1], shift1, scale1), gather=gather)
            gate1, shift2, scale2, gate2 = gate1n, shift2n, scale2n, gate2n
        else:
            (x2,) = _post("last", f, x2, gate2, norm_post_ffn[l], seq, gather=gather)
    return x2.reshape(batch, seq, d)
```

```python
import functools

import jax
import jax.numpy as jnp
import numpy as np
from jax import lax
from jax.experimental import pallas as pl
from jax.experimental.pallas import tpu as pltpu

F32 = jnp.float32
BF16 = jnp.bfloat16
U32 = jnp.uint32
I32 = jnp.int32

EPS = 1e-6
NEG_INF = -1e30
BLOCK = 128
HEAD_DIM = 64
GQA_GROUP = 8
POOL_WINDOWS = (2, 4, 8, 16)
N_EXPERTS = 8
N_MOD = 6
LANES = 128
FFN_ROWS = 512
FFN_CHUNK = 512
BF16_SUBLANES = 16
RIDER_BLOCK_BYTES = 4 << 20

MIB = 1 << 20


def _params(sem, vmem_mib):
    return pltpu.CompilerParams(dimension_semantics=sem, vmem_limit_bytes=vmem_mib * MIB)


def _sigmoid(x):
    return 1.0 / (1.0 + jnp.exp(-x))


def _rms(x, g):
    return x * lax.rsqrt(jnp.mean(x * x, axis=-1, keepdims=True) + EPS) * g


def _pack_pair(a, b):
    ua = pltpu.bitcast(a.astype(BF16).astype(F32), U32)
    ub = pltpu.bitcast(b.astype(BF16).astype(F32), U32)
    return ua | (ub >> 16)


def _unpack_pair(p):
    hi = pltpu.bitcast(p & jnp.uint32(0xFFFF0000), F32).astype(BF16)
    lo = pltpu.bitcast(p << 16, F32).astype(BF16)
    return hi, lo


def _select(w, lead):
    return w if lead is None else w[lead]


def _plan_riders(ride, grid):
    steps = grid[0] * grid[1]
    plans = []
    for w, _ in ride:
        rows, cols = w.shape[-2:]
        plan = None
        for rb in range(BF16_SUBLANES, rows + 1, BF16_SUBLANES):
            if rows % rb == 0 and rows // rb <= steps:
                plan = rb if rb * cols * 4 <= RIDER_BLOCK_BYTES else None
                break
        plans.append(plan)
    return plans


def _ridden(outs, ride, plans):
    extra = iter(outs[1:])
    return outs[0], tuple(next(extra) if rb is not None else _select(w, lead).astype(BF16)
                          for (w, lead), rb in zip(ride, plans))


def _add_riders(kernel_fn, args, in_specs, out_specs, out_shape, grid, ride, plans):
    riders = [(w, lead, rb) for (w, lead), rb in zip(ride, plans) if rb is not None]
    if not riders:
        return kernel_fn, list(args)
    n1 = grid[1]
    n_in, n_out, n_r = len(in_specs), len(out_specs), len(riders)
    for w, lead, rb in riders:
        rows, cols = w.shape[-2:]
        last = rows // rb - 1
        blk = functools.partial(lambda a, b, last: (jnp.minimum(a * n1 + b, last), 0), last=last)
        if lead is None:
            in_specs.append(pl.BlockSpec((rb, cols), blk))
        else:
            in_specs.append(pl.BlockSpec((None, rb, cols), functools.partial(
                lambda a, b, last, lead: (lead, jnp.minimum(a * n1 + b, last), 0), last=last, lead=lead)))
        out_specs.append(pl.BlockSpec((rb, cols), blk))
        out_shape.append(jax.ShapeDtypeStruct((rows, cols), BF16))

    def body(*refs):
        ins, srcs = refs[:n_in], refs[n_in:n_in + n_r]
        refs = refs[n_in + n_r:]
        outs, dsts = refs[:n_out], refs[n_out:n_out + n_r]
        kernel_fn(*ins, *outs, *refs[n_out + n_r:])
        for src, dst in zip(srcs, dsts):
            dst[...] = src[...].astype(BF16)

    return body, list(args) + [w for w, _, _ in riders]


def _ada_kernel(c_ref, w_ref, b_ref, o_ref):
    c = c_ref[...]
    ca = (c * _sigmoid(c)).astype(BF16)
    o_ref[0] = jnp.dot(ca, w_ref[0].astype(BF16), preferred_element_type=F32) + b_ref[0]


def _ada(c, w_ada, b_ada):
    depth, d, nm = w_ada.shape
    b = c.shape[0]
    tn = 512
    return pl.pallas_call(
        _ada_kernel,
        grid=(depth, nm // tn),
        in_specs=[
            pl.BlockSpec((b, d), lambda l, j: (0, 0)),
            pl.BlockSpec((1, d, tn), lambda l, j: (l, 0, j)),
            pl.BlockSpec((1, 1, tn), lambda l, j: (l, 0, j)),
        ],
        out_specs=pl.BlockSpec((1, b, tn), lambda l, j: (l, 0, j)),
        out_shape=jax.ShapeDtypeStruct((depth, b, nm), F32),
        compiler_params=_params(("parallel", "parallel"), 40),
        name="ada_mod",
    )(c, w_ada, b_ada.reshape(depth, 1, nm))


def _prenorm_kernel(x_ref, g_ref, sh_ref, sc_ref, o_ref):
    h = _rms(x_ref[...], g_ref[...]) * (1.0 + sc_ref[0]) + sh_ref[0]
    o_ref[...] = h.astype(BF16)


def _prenorm(x2, g, shift, scale, seq):
    n, d = x2.shape
    tr = min(256, seq)
    per = seq // tr
    return pl.pallas_call(
        _prenorm_kernel,
        grid=(n // tr,),
        in_specs=[
            pl.BlockSpec((tr, d), lambda i: (i, 0)),
            pl.BlockSpec((1, d), lambda i: (0, 0)),
            pl.BlockSpec((1, 1, d), lambda i: (i // per, 0, 0)),
            pl.BlockSpec((1, 1, d), lambda i: (i // per, 0, 0)),
        ],
        out_specs=pl.BlockSpec((tr, d), lambda i: (i, 0)),
        out_shape=jax.ShapeDtypeStruct((n, d), BF16),
        compiler_params=_params(("parallel",), 40),
        name="prenorm",
    )(x2, g.reshape(1, d), shift, scale)


def _mm_kernel(a_ref, w_ref, o_ref):
    o_ref[...] = jnp.dot(a_ref[...], w_ref[...], preferred_element_type=F32).astype(o_ref.dtype)


def _mm(a, w, out_dtype, tm, tn, ride=()):
    m, k = a.shape
    n = w.shape[1]
    tm = min(tm, m)
    grid = (m // tm, n // tn)
    plans = _plan_riders(ride, grid)
    in_specs = [
        pl.BlockSpec((tm, k), lambda i, j: (i, 0)),
        pl.BlockSpec((k, tn), lambda i, j: (0, j)),
    ]
    out_specs = [pl.BlockSpec((tm, tn), lambda i, j: (i, j))]
    out_shape = [jax.ShapeDtypeStruct((m, n), out_dtype)]
    body, args = _add_riders(_mm_kernel, [a, w], in_specs, out_specs, out_shape, grid, ride, plans)
    outs = pl.pallas_call(
        body,
        grid=grid,
        in_specs=in_specs,
        out_specs=out_specs,
        out_shape=out_shape,
        compiler_params=_params(("parallel", "arbitrary"), 48),
        name="proj_in",
    )(*args)
    return _ridden(outs, ride, plans)


def _mm2_kernel(a1_ref, a2_ref, w1_ref, w2_ref, o_ref):
    acc = jnp.dot(a1_ref[...], w1_ref[...], preferred_element_type=F32)
    acc = acc + jnp.dot(a2_ref[...], w2_ref[...], preferred_element_type=F32)
    o_ref[...] = acc.astype(o_ref.dtype)


def _mm2(a1, a2, w, out_dtype, tm, tn, ride=()):
    m, k1 = a1.shape
    n = w.shape[1]
    tm = min(tm, m)
    grid = (m // tm, n // tn)
    plans = _plan_riders(ride, grid)
    in_specs = [
        pl.BlockSpec((tm, k1), lambda i, j: (i, 0)),
        pl.BlockSpec((tm, k1), lambda i, j: (i, 0)),
        pl.BlockSpec((k1, tn), lambda i, j: (0, j)),
        pl.BlockSpec((k1, tn), lambda i, j: (1, j)),
    ]
    out_specs = [pl.BlockSpec((tm, tn), lambda i, j: (i, j))]
    out_shape = [jax.ShapeDtypeStruct((m, n), out_dtype)]
    body, args = _add_riders(_mm2_kernel, [a1, a2, w, w], in_specs, out_specs, out_shape, grid, ride, plans)
    outs = pl.pallas_call(
        body,
        grid=grid,
        in_specs=in_specs,
        out_specs=out_specs,
        out_shape=out_shape,
        compiler_params=_params(("parallel", "arbitrary"), 48),
        name="proj_out",
    )(*args)
    return _ridden(outs, ride, plans)


def _split3_const(x):
    parts = []
    r = np.float32(x)
    for _ in range(3):
        p = np.float32(np.asarray(r, dtype=jnp.bfloat16))
        parts.append(float(p))
        r = np.float32(r - p)
    return parts


def _swap_halves(x):
    half = x.shape[1] // 2
    return jnp.concatenate([x[:, half:], x[:, :half]], axis=1)


def _attn_tables(slopes):
    n_heads = len(slopes)
    qx = np.zeros((n_heads, BLOCK, LANES), np.float32)
    for h, s in enumerate(slopes):
        qx[h, :, HEAD_DIM:HEAD_DIM + 3] = _split3_const(s)
    kx = np.zeros((2 * BLOCK, LANES), np.float32)
    kx[:, HEAD_DIM:HEAD_DIM + 3] = np.arange(2 * BLOCK, dtype=np.float32)[:, None]
    return jnp.asarray(qx, BF16), jnp.asarray(kx, BF16)


def _attn_kernel(slopes, sinks_ref, qx_ref, kx_ref, q_ref, kp_ref, kc_ref, vp_ref, vc_ref, o_ref):
    n = pl.program_id(1)
    n_heads = len(slopes)
    k = jnp.concatenate([kp_ref[...], kc_ref[...]], axis=0)
    v = jnp.concatenate([vp_ref[...], vc_ref[...]], axis=0)
    c_idx = lax.broadcasted_iota(I32, (BLOCK, BLOCK), 0)
    i_idx = lax.broadcasted_iota(I32, (BLOCK, BLOCK), 1)
    from_prev = c_idx > i_idx
    prev_bias = jnp.where(n > 0, 0.0, NEG_INF)
    qpos = (lax.broadcasted_iota(I32, (1, BLOCK), 1) + BLOCK).astype(F32)
    qlane = lax.broadcasted_iota(I32, (BLOCK, LANES), 1) < HEAD_DIM
    klane = lax.broadcasted_iota(I32, (2 * BLOCK, LANES), 1) < HEAD_DIM
    scale = jnp.asarray(HEAD_DIM ** -0.5, BF16)
    outs = []
    for kv in range(n_heads // GQA_GROUP):
        heads = range(kv * GQA_GROUP, (kv + 1) * GQA_GROUP)
        kt = k[:, (kv // 2) * LANES:(kv // 2 + 1) * LANES]
        vt = v[:, (kv // 2) * LANES:(kv // 2 + 1) * LANES]
        if kv % 2:
            kt, vt = _swap_halves(kt), _swap_halves(vt)
        k_aug = jnp.where(klane, kt, kx_ref[...])
        pieces = []
        for h in heads:
            qt = q_ref[:, (h // 2) * LANES:(h // 2 + 1) * LANES]
            if h % 2:
                qt = _swap_halves(qt)
            pieces.append(jnp.where(qlane, qt * scale, qx_ref[h]))
        qg = jnp.concatenate(pieces, axis=0)
        st = lax.dot_general(k_aug, qg, (((1,), (1,)), ((), ())), preferred_element_type=F32)
        pts = []
        for g, h in enumerate(heads):
            s_prev = st[:BLOCK, g * BLOCK:(g + 1) * BLOCK] + prev_bias
            s_cur = st[BLOCK:, g * BLOCK:(g + 1) * BLOCK]
            s = jnp.where(from_prev, s_prev, s_cur)
            sink = sinks_ref[h] + np.float32(slopes[h]) * qpos
            m = jnp.maximum(jnp.max(s, axis=0, keepdims=True), sink)
            e = jnp.exp(s - m)
            denom = jnp.sum(e, axis=0, keepdims=True) + jnp.exp(sink - m)
            p = e * (1.0 / denom)
            pts.append(jnp.concatenate([jnp.where(from_prev, p, 0.0).astype(BF16),
                                        jnp.where(from_prev, 0.0, p).astype(BF16)], axis=0))
        pt = jnp.concatenate(pts, axis=1)
        og = lax.dot_general(pt, vt, (((0,), (0,)), ((), ())), preferred_element_type=F32)
        for g in range(GQA_GROUP):
            outs.append(og[g * BLOCK:(g + 1) * BLOCK, :HEAD_DIM].astype(BF16))
    o_ref[...] = jnp.concatenate(outs, axis=-1)


def _attention(proj, sinks, batch, seq, d_attn, d_kv, ride=()):
    n = proj.shape[0]
    nb = seq // BLOCK
    n_heads = d_attn // HEAD_DIM
    slopes = tuple(2.0 ** (-8.0 * (h + 1) / n_heads) for h in range(n_heads))
    kcol = d_attn // d_kv
    vcol = kcol + 1

    def cur(col):
        return lambda b, i: (b * nb + i, col)

    def prev(col):
        return lambda b, i: (b * nb + jnp.maximum(i - 1, 0), col)

    grid = (batch, nb)
    plans = _plan_riders(ride, grid)
    qx, kx = _attn_tables(slopes)
    in_specs = [
        pl.BlockSpec(memory_space=pltpu.SMEM),
        pl.BlockSpec(qx.shape, lambda b, i: (0, 0, 0)),
        pl.BlockSpec(kx.shape, lambda b, i: (0, 0)),
        pl.BlockSpec((BLOCK, d_attn), cur(0)),
        pl.BlockSpec((BLOCK, d_kv), prev(kcol)),
        pl.BlockSpec((BLOCK, d_kv), cur(kcol)),
        pl.BlockSpec((BLOCK, d_kv), prev(vcol)),
        pl.BlockSpec((BLOCK, d_kv), cur(vcol)),
    ]
    out_specs = [pl.BlockSpec((BLOCK, d_attn), cur(0))]
    out_shape = [jax.ShapeDtypeStruct((n, d_attn), BF16)]
    body, args = _add_riders(functools.partial(_attn_kernel, slopes), [sinks, qx, kx, proj, proj, proj, proj, proj],
                             in_specs, out_specs, out_shape, grid, ride, plans)
    outs = pl.pallas_call(
        body,
        grid=grid,
        in_specs=in_specs,
        out_specs=out_specs,
        out_shape=out_shape,
        compiler_params=_params(("parallel", "arbitrary"), 40),
        name="swa_attention",
    )(*args)
    return _ridden(outs, ride, plans)


def _pool_kernel(u_ref, w_ref, ps_ref, o_ref):
    g = pl.program_id(1)
    u = u_ref[...].astype(F32)
    row = lax.broadcasted_iota(I32, u.shape, 0)

    def shifted(x, k):
        return jnp.where(row >= k, pltpu.roll(x, k, 0), 0.0)

    s2 = u + shifted(u, 1)
    s4 = s2 + shifted(s2, 2)
    s8 = s4 + shifted(s4, 4)
    s16 = s8 + shifted(s8, 8)
    wsum = jnp.where(g == 0, s2, jnp.where(g == 1, s4, jnp.where(g == 2, s8, s16)))
    win = jnp.left_shift(jnp.int32(POOL_WINDOWS[0]), g)
    cnt = jnp.minimum(row + 1, win).astype(F32)
    pooled = wsum / cnt - u
    y = jnp.dot(pooled.astype(BF16), w_ref[0], preferred_element_type=F32)
    o_ref[...] = (y * ps_ref[...]).astype(BF16)


def _pool(proj, w_pool, pool_scale, batch, seq, u_col0):
    n = proj.shape[0]
    ng, c, _ = w_pool.shape
    assert POOL_WINDOWS == tuple(POOL_WINDOWS[0] << g for g in range(ng))
    col0 = u_col0 // c
    return pl.pallas_call(
        _pool_kernel,
        grid=(batch, ng),
        in_specs=[
            pl.BlockSpec((seq, c), lambda b, g: (b, col0 + g)),
            pl.BlockSpec((1, c, c), lambda b, g: (g, 0, 0)),
            pl.BlockSpec((1, c), lambda b, g: (0, g)),
        ],
        out_specs=pl.BlockSpec((seq, c), lambda b, g: (b, g)),
        out_shape=jax.ShapeDtypeStruct((n, ng * c), BF16),
        compiler_params=_params(("parallel", "arbitrary"), 48),
        name="pool_mixer",
    )(proj, w_pool, pool_scale.reshape(1, ng * c))


def _top2_route(logits):
    lane = lax.broadcasted_iota(I32, logits.shape, 1).astype(F32)
    m1 = jnp.max(logits, axis=-1, keepdims=True)
    i1 = jnp.min(jnp.where(logits == m1, lane, float(LANES)), axis=-1, keepdims=True)
    l2 = jnp.where(lane == i1, -jnp.inf, logits)
    m2 = jnp.max(l2, axis=-1, keepdims=True)
    i2 = jnp.min(jnp.where(l2 == m2, lane, float(LANES)), axis=-1, keepdims=True)
    e = jnp.exp(m2 - m1)
    w1 = 1.0 / (1.0 + e)
    w2 = e / (1.0 + e)
    return jnp.where(lane == 0.0, i1, jnp.where(lane == 1.0, i2, jnp.where(lane == 2.0, w1, jnp.where(lane == 3.0, w2, 0.0))))


def _post_kernel(mode, gathered, *refs):
    refs = list(refs)
    if gathered:
        pos_ref, pos_next_ref, y_hbm, route_in = refs[:4]
        refs = refs[4:]
    else:
        f_ref = refs.pop(0)
    x_ref, gate_ref, gpost_ref = refs[:3]
    refs = refs[3:]
    if mode != "last":
        gpre_ref, sh_ref, sc_ref = refs[:3]
        refs = refs[3:]
    if mode == "route":
        rw_ref, rb_ref = refs[:2]
        refs = refs[2:]
    xo_ref = refs.pop(0)
    if mode != "last":
        ho_ref = refs.pop(0)
    if mode == "route":
        ro_ref = refs.pop(0)

    if gathered:
        ybuf, sems = refs
        tr = x_ref.shape[0]
        i = pl.program_id(0)
        slot = i % 2

        def gather(idx_ref, s):
            def start(r, carry):
                for kk in range(2):
                    pltpu.make_async_copy(y_hbm.at[pl.ds(idx_ref[0, 0, 2 * r + kk], 1)],
                                          ybuf.at[s, kk, pl.ds(r, 1)], sems.at[s]).start()
                return carry
            lax.fori_loop(0, tr, start, 0, unroll=8)

        @pl.when(i == 0)
        def _():
            gather(pos_ref, 0)

        @pl.when(i + 1 < pl.num_programs(0))
        def _():
            gather(pos_next_ref, 1 - slot)

        pltpu.make_async_copy(ybuf.at[slot], ybuf.at[slot], sems.at[slot]).wait()
        rt = route_in[...]
        f = ybuf[slot, 0] * rt[:, 2:3] + ybuf[slot, 1] * rt[:, 3:4]
    else:
        f = f_ref[...].astype(F32)

    xn = x_ref[...] + gate_ref[0] * _rms(f, gpost_ref[...])
    xo_ref[...] = xn
    if mode == "last":
        return
    h = _rms(xn, gpre_ref[...]) * (1.0 + sc_ref[0]) + sh_ref[0]
    if mode == "next":
        ho_ref[...] = h.astype(BF16)
        return
    half = h.shape[1] // 2
    ho_ref[...] = _pack_pair(h[:, :half], h[:, half:])
    logits = jnp.dot(h.astype(BF16), rw_ref[...], preferred_element_type=F32) + rb_ref[...]
    ro_ref[...] = _top2_route(logits)


def _post(mode, f, x2, gate, g_post, seq, nxt=None, router=None, gather=None):
    n, d = x2.shape
    tr = min(256, seq)
    per = seq // tr
    row = lambda i: (i, 0)
    fix = lambda i: (0, 0)
    bat = lambda i: (i // per, 0, 0)
    in_specs, args, scratch = [], [], []
    if gather is not None:
        pos, y, route = gather
        last = n // tr - 1
        pos3 = pos.reshape(n // tr, 1, 2 * tr)
        in_specs += [
            pl.BlockSpec((1, 1, 2 * tr), lambda i: (i, 0, 0), memory_space=pltpu.SMEM),
            pl.BlockSpec((1, 1, 2 * tr), lambda i: (jnp.minimum(i + 1, last), 0, 0), memory_space=pltpu.SMEM),
            pl.BlockSpec(memory_space=pl.ANY),
            pl.BlockSpec((tr, LANES), row),
        ]
        args += [pos3, pos3, y, route]
        scratch = [pltpu.VMEM((2, 2, tr, d), F32), pltpu.SemaphoreType.DMA((2,))]
    else:
        in_specs.append(pl.BlockSpec((tr, d), row))
        args.append(f)
    in_specs += [pl.BlockSpec((tr, d), row), pl.BlockSpec((1, 1, d), bat), pl.BlockSpec((1, d), fix)]
    args += [x2, gate, g_post.reshape(1, d)]
    out_specs = [pl.BlockSpec((tr, d), row)]
    out_shape = [jax.ShapeDtypeStruct((n, d), F32)]
    if mode != "last":
        g_pre, shift, scale = nxt
        in_specs += [pl.BlockSpec((1, d), fix), pl.BlockSpec((1, 1, d), bat), pl.BlockSpec((1, 1, d), bat)]
        args += [g_pre.reshape(1, d), shift, scale]
    if mode == "next":
        out_specs.append(pl.BlockSpec((tr, d), row))
        out_shape.append(jax.ShapeDtypeStruct((n, d), BF16))
    if mode == "route":
        rw, rb = router
        in_specs += [pl.BlockSpec((d, LANES), fix), pl.BlockSpec((1, LANES), fix)]
        args += [rw, rb]
        out_specs += [pl.BlockSpec((tr, d // 2), row), pl.BlockSpec((tr, LANES), row)]
        out_shape += [jax.ShapeDtypeStruct((n, d // 2), U32), jax.ShapeDtypeStruct((n, LANES), F32)]
    return pl.pallas_call(
        functools.partial(_post_kernel, mode, gather is not None),
        grid=(n // tr,),
        in_specs=in_specs,
        out_specs=out_specs,
        out_shape=out_shape,
        scratch_shapes=scratch,
        compiler_params=_params(("arbitrary",), 56),
        name="post_" + mode + ("_gather" if gather is not None else ""),
    )(*args)


def _swiglu_step(h_ref, wg_ref, wu_ref, wd_ref, o_ref, j, n_chunk, skip=None):
    @pl.when(j == 0)
    def _():
        o_ref[...] = jnp.zeros_like(o_ref)

    h = h_ref[...]
    g = jnp.dot(h, wg_ref[...], preferred_element_type=F32)
    u = jnp.dot(h, wu_ref[...], preferred_element_type=F32)
    a = g * _sigmoid(g) * u
    if skip is not None:
        a = jnp.where(lax.broadcasted_iota(I32, a.shape, 1) >= skip, a, 0.0)
    a = a.astype(BF16)
    d = o_ref.shape[1]
    cw = d // n_chunk
    for c in range(n_chunk):
        o_ref[:, c * cw:(c + 1) * cw] += jnp.dot(a, wd_ref[:, c * cw:(c + 1) * cw], preferred_element_type=F32)


def _ffn_kernel(dff, h_ref, wg_ref, wu_ref, wd_ref, o_ref):
    j = pl.program_id(1)
    tf = wg_ref.shape[1]
    skip = j * tf - jnp.minimum(j * tf, dff - tf)
    _swiglu_step(h_ref, wg_ref, wu_ref, wd_ref, o_ref, j, 4, skip=skip)


def _ffn(h, wg, wu, wd, tm, tf, ride=()):
    n, d = h.shape
    dff = wg.shape[1]
    tm = min(tm, n)
    grid = (n // tm, pl.cdiv(dff, tf))
    plans = _plan_riders(ride, grid)
    assert (dff - tf) % LANES == 0
    start = lambda j: pl.multiple_of(jnp.minimum(j * tf, dff - tf), LANES)
    in_specs = [
        pl.BlockSpec((tm, d), lambda i, j: (i, 0)),
        pl.BlockSpec((pl.Element(d), pl.Element(tf)), lambda i, j: (0, start(j))),
        pl.BlockSpec((pl.Element(d), pl.Element(tf)), lambda i, j: (0, start(j))),
        pl.BlockSpec((pl.Element(tf), pl.Element(d)), lambda i, j: (start(j), 0)),
    ]
    out_specs = [pl.BlockSpec((tm, d), lambda i, j: (i, 0))]
    out_shape = [jax.ShapeDtypeStruct((n, d), F32)]
    body, args = _add_riders(functools.partial(_ffn_kernel, dff), [h, wg, wu, wd], in_specs, out_specs, out_shape,
                             grid, ride, plans)
    outs = pl.pallas_call(
        body,
        grid=grid,
        in_specs=in_specs,
        out_specs=out_specs,
        out_shape=out_shape,
        compiler_params=_params(("parallel", "arbitrary"), 58),
        name="dense_swiglu",
    )(*args)
    return _ridden(outs, ride, plans)


def _expert_kernel(te_ref, nt_ref, idx_ref, idx_next_ref, src_hbm, wg_ref, wu_ref, wd_ref, o_ref, xs_buf, h_scr, sem):
    t = pl.program_id(0)
    j = pl.program_id(1)
    n_live = nt_ref[0]
    live = t < n_live
    tm, half = xs_buf.shape

    def gather(iref):
        def start(r, carry):
            pltpu.make_async_copy(src_hbm.at[pl.ds(iref[0, 0, r], 1)], xs_buf.at[pl.ds(r, 1)], sem).start()
            return carry
        lax.fori_loop(0, tm, start, 0, unroll=8)

    @pl.when(j == 0)
    def _():
        @pl.when(t == 0)
        def _():
            gather(idx_ref)

        @pl.when(live)
        def _():
            pltpu.make_async_copy(xs_buf, xs_buf, sem).wait()
            hi, lo = _unpack_pair(xs_buf[...])
            h_scr[:, :half] = hi
            h_scr[:, half:] = lo

        @pl.when(t + 1 < n_live)
        def _():
            gather(idx_next_ref)

        @pl.when(jnp.logical_not(live))
        def _():
            o_ref[...] = jnp.zeros_like(o_ref)

    @pl.when(live)
    def _():
        _swiglu_step(h_scr, wg_ref.at[0], wu_ref.at[0], wd_ref.at[0], o_ref, j, 4)


def _experts(src, src_tok, tile_expert, n_tiles, wg, wu, wd, tm, tf):
    half = src.shape[1]
    d = 2 * half
    rows = src_tok.shape[0]
    dff = wg.shape[2]
    nj = dff // tf
    n_t = rows // tm
    idx3 = src_tok.reshape(n_t, 1, tm)

    def chunk(t, j, nt):
        return jnp.where(t < nt[0], j, nj - 1)

    return pl.pallas_call(
        _expert_kernel,
        grid_spec=pltpu.PrefetchScalarGridSpec(
            num_scalar_prefetch=2,
            grid=(n_t, nj),
            in_specs=[
                pl.BlockSpec((1, 1, tm), lambda t, j, te, nt: (t, 0, 0), memory_space=pltpu.SMEM),
                pl.BlockSpec((1, 1, tm), lambda t, j, te, nt: (jnp.minimum(t + 1, n_t - 1), 0, 0),
                             memory_space=pltpu.SMEM),
                pl.BlockSpec(memory_space=pl.ANY),
                pl.BlockSpec((1, d, tf), lambda t, j, te, nt: (te[t], 0, chunk(t, j, nt))),
                pl.BlockSpec((1, d, tf), lambda t, j, te, nt: (te[t], 0, chunk(t, j, nt))),
                pl.BlockSpec((1, tf, d), lambda t, j, te, nt: (te[t], chunk(t, j, nt), 0)),
            ],
            out_specs=pl.BlockSpec((tm, d), lambda t, j, te, nt: (t, 0)),
            scratch_shapes=[pltpu.VMEM((tm, half), U32), pltpu.VMEM((tm, d), BF16), pltpu.SemaphoreType.DMA(())],
        ),
        out_shape=jax.ShapeDtypeStruct((rows, d), F32),
        compiler_params=_params(("arbitrary", "arbitrary"), 58),
        name="expert_swiglu",
    )(tile_expert, n_tiles, idx3, idx3, src, wg, wu, wd)


def _moe_plan(top_idx, tm):
    n = top_idx.shape[0]
    p = 2 * n
    flat = top_idx.reshape(p)
    experts = jnp.arange(N_EXPERTS, dtype=I32)
    onehot = (flat[:, None] == experts[None, :]).astype(I32)
    csum = jnp.cumsum(onehot, axis=0)
    rank = jnp.sum(onehot * csum, axis=1) - 1
    counts = csum[-1]
    tiles_per = (counts + (tm - 1)) // tm
    tile_end = jnp.cumsum(tiles_per)
    tile_start = tile_end - tiles_per
    pos = (tile_start * tm)[flat] + rank
    n_tiles = tile_end[-1:]
    t_max = p // tm + N_EXPERTS
    t_ids = jnp.arange(t_max, dtype=I32)
    te = jnp.sum((t_ids[:, None] >= tile_end[None, :]).astype(I32), axis=1)
    last_e = jnp.max(jnp.where(tiles_per > 0, experts, 0))
    te = jnp.minimum(te, last_e)
    src_tok = jnp.zeros((t_max * tm,), I32).at[pos].set(jnp.arange(p, dtype=I32) // 2)
    return pos.astype(I32), src_tok, te.astype(I32), n_tiles.astype(I32)


def kernel(x, c, w_ada, b_ada, norm_pre_mix, norm_post_mix, norm_pre_ffn, norm_post_ffn, w_in, sinks, w_pool,
           pool_scale, w_out, ffn_w_gate, ffn_w_up, ffn_w_down, router_w, router_b, moe_w_gate, moe_w_up, moe_w_down):
    batch, seq, d = x.shape
    depth = w_ada.shape[0]
    n = batch * seq
    d_attn = d // 2
    n_heads = sinks.shape[1]
    d_kv = (n_heads // GQA_GROUP) * HEAD_DIM
    u_col0 = d_attn + 2 * d_kv
    assert n_heads * HEAD_DIM == d_attn and seq % BLOCK == 0

    x2 = x.reshape(n, d)
    mod = _ada(c, w_ada, b_ada)

    def mods(l):
        return [mod[l, :, k * d:(k + 1) * d].reshape(batch, 1, d) for k in range(N_MOD)]

    shift1, scale1, gate1, shift2, scale2, gate2 = mods(0)
    h = _prenorm(x2, norm_pre_mix[0], shift1, scale1, seq)
    flat = lambda w: w.reshape(-1, w.shape[-1])
    ready = {}

    def hosted(host, rides, *a):
        out, casts = host(*a, ride=[(w, lead) for _, w, lead in rides])
        ready.update({key: v for (key, _, _), v in zip(rides, casts)})
        return out

    def take(key, w, lead=None):
        shape = w.shape if lead is None else w.shape[1:]
        return ready.pop(key).reshape(shape) if key in ready else _select(w, lead).astype(BF16)

    for l in range(depth):
        i = l // 2
        dense = l % 2 == 0
        nm = (l + 1) // 2 if (dense and l + 1 < depth) else None
        on_proj_in, on_attn, on_proj_out, on_ffn = [(("w_out", l), w_out, l)], [], [], []
        if dense:
            on_proj_in += [(("ffn_wg", i), ffn_w_gate, i), (("ffn_wd", i), ffn_w_down, i)]
            on_proj_out.append((("ffn_wu", i), ffn_w_up, i))
            if nm is not None:
                on_attn.append((("moe_wg", nm), flat(moe_w_gate[nm]), None))
                on_ffn.append((("moe_wd", nm), flat(moe_w_down[nm]), None))
            if l + 1 < depth:
                on_ffn.append((("w_in", l + 1), w_in, l + 1))
        else:
            on_attn.append((("moe_wu", i), flat(moe_w_up[i]), None))

        proj = hosted(_mm, on_proj_in, h, take(("w_in", l), w_in, l), BF16, 1024, 512)
        attn = hosted(_attention, on_attn, proj, sinks[l], batch, seq, d_attn, d_kv)
        pool = _pool(proj, w_pool[l].astype(BF16), pool_scale[l], batch, seq, u_col0)
        mix = hosted(_mm2, on_proj_out, attn, pool, take(("w_out", l), w_out, l), BF16, 1024, 512)
        nxt = (norm_pre_ffn[l], shift2, scale2)
        if dense:
            x2, h2 = _post("next", mix, x2, gate1, norm_post_mix[l], seq, nxt=nxt)
            f = hosted(_ffn, on_ffn, h2, take(("ffn_wg", i), ffn_w_gate, i), take(("ffn_wu", i), ffn_w_up, i),
                       take(("ffn_wd", i), ffn_w_down, i), FFN_ROWS, FFN_CHUNK)
            gather = None
        else:
            rw = jnp.zeros((d, LANES), BF16).at[:, :N_EXPERTS].set(router_w[i].astype(BF16))
            rb = jnp.full((1, LANES), NEG_INF, F32).at[0, :N_EXPERTS].set(router_b[i])
            x2, h2p, route = _post("route", mix, x2, gate1, norm_post_mix[l], seq, nxt=nxt, router=(rw, rb))
            pos, src_tok, tile_expert, n_tiles = _moe_plan(route[:, :2].astype(I32), FFN_ROWS)
            y = _experts(h2p, src_tok, tile_expert, n_tiles, take(("moe_wg", i), moe_w_gate[i]),
                         take(("moe_wu", i), moe_w_up[i]), take(("moe_wd", i), moe_w_down[i]), FFN_ROWS, FFN_CHUNK)
            f, gather = None, (pos, y, route)
        if l + 1 < depth:
            shift1, scale1, gate1n, shift2n, scale2n, gate2n = mods(l + 1)
            x2, h = _post("next", f, x2, gate2, norm_post_ffn[l], seq,
                          nxt=(norm_pre_mix[l + 1], shift1, scale1), gather=gather)
            gate1, shift2, scale2, gate2 = gate1n, shift2n, scale2n, gate2n
        else:
            (x2,) = _post("last", f, x2, gate2, norm_post_ffn[l], seq, gather=gather)
    return x2.reshape(batch, seq, d)
```

```python
import functools

import jax
import jax.numpy as jnp
import numpy as np
from jax import lax
from jax.experimental import pallas as pl
from jax.experimental.pallas import tpu as pltpu

F32 = jnp.float32
BF16 = jnp.bfloat16
U32 = jnp.uint32
I32 = jnp.int32

EPS = 1e-6
NEG_INF = -1e30
BLOCK = 128
HEAD_DIM = 64
GQA_GROUP = 8
POOL_WINDOWS = (2, 4, 8, 16)
N_EXPERTS = 8
N_MOD = 6
LANES = 128
PROJ_ROWS = 1024
PROJ_IN_COLS = 768
PROJ_OUT_COLS = 1024
FFN_ROWS = 512
FFN_CHUNK = 512
BF16_SUBLANES = 16
RIDER_BLOCK_BYTES = 4 << 20

MIB = 1 << 20


def _params(sem, vmem_mib):
    return pltpu.CompilerParams(dimension_semantics=sem, vmem_limit_bytes=vmem_mib * MIB)


def _sigmoid(x):
    return 1.0 / (1.0 + jnp.exp(-x))


def _rms(x, g):
    return x * lax.rsqrt(jnp.mean(x * x, axis=-1, keepdims=True) + EPS) * g


def _pack_pair(a, b):
    ua = pltpu.bitcast(a.astype(BF16).astype(F32), U32)
    ub = pltpu.bitcast(b.astype(BF16).astype(F32), U32)
    return ua | (ub >> 16)


def _unpack_pair_f32(p):
    return pltpu.bitcast(p & jnp.uint32(0xFFFF0000), F32), pltpu.bitcast(p << 16, F32)


def _unpack_pair(p):
    hi, lo = _unpack_pair_f32(p)
    return hi.astype(BF16), lo.astype(BF16)


def _select(w, lead):
    return w if lead is None else w[lead]


def _plan_riders(ride, grid):
    steps = grid[0] * grid[1]
    plans = []
    for w, _ in ride:
        rows, cols = w.shape[-2:]
        plan = None
        for rb in range(BF16_SUBLANES, rows + 1, BF16_SUBLANES):
            if rows % rb == 0 and rows // rb <= steps:
                plan = rb if rb * cols * 4 <= RIDER_BLOCK_BYTES else None
                break
        plans.append(plan)
    return plans


def _ridden(outs, ride, plans):
    extra = iter(outs[1:])
    return outs[0], tuple(next(extra) if rb is not None else _select(w, lead).astype(BF16)
                          for (w, lead), rb in zip(ride, plans))


def _add_riders(kernel_fn, args, in_specs, out_specs, out_shape, grid, ride, plans):
    riders = [(w, lead, rb) for (w, lead), rb in zip(ride, plans) if rb is not None]
    if not riders:
        return kernel_fn, list(args)
    n1 = grid[1]
    n_in, n_out, n_r = len(in_specs), len(out_specs), len(riders)
    for w, lead, rb in riders:
        rows, cols = w.shape[-2:]
        last = rows // rb - 1
        blk = functools.partial(lambda a, b, last: (jnp.minimum(a * n1 + b, last), 0), last=last)
        if lead is None:
            in_specs.append(pl.BlockSpec((rb, cols), blk))
        else:
            in_specs.append(pl.BlockSpec((None, rb, cols), functools.partial(
                lambda a, b, last, lead: (lead, jnp.minimum(a * n1 + b, last), 0), last=last, lead=lead)))
        out_specs.append(pl.BlockSpec((rb, cols), blk))
        out_shape.append(jax.ShapeDtypeStruct((rows, cols), BF16))

    def body(*refs):
        ins, srcs = refs[:n_in], refs[n_in:n_in + n_r]
        refs = refs[n_in + n_r:]
        outs, dsts = refs[:n_out], refs[n_out:n_out + n_r]
        kernel_fn(*ins, *outs, *refs[n_out + n_r:])
        for src, dst in zip(srcs, dsts):
            dst[...] = src[...].astype(BF16)

    return body, list(args) + [w for w, _, _ in riders]


def _ada_kernel(c_ref, w_ref, b_ref, o_ref):
    c = c_ref[...]
    ca = (c * _sigmoid(c)).astype(BF16)
    o_ref[0] = jnp.dot(ca, w_ref[0].astype(BF16), preferred_element_type=F32) + b_ref[0]


def _ada(c, w_ada, b_ada):
    depth, d, nm = w_ada.shape
    b = c.shape[0]
    tn = 512
    return pl.pallas_call(
        _ada_kernel,
        grid=(depth, nm // tn),
        in_specs=[
            pl.BlockSpec((b, d), lambda l, j: (0, 0)),
            pl.BlockSpec((1, d, tn), lambda l, j: (l, 0, j)),
            pl.BlockSpec((1, 1, tn), lambda l, j: (l, 0, j)),
        ],
        out_specs=pl.BlockSpec((1, b, tn), lambda l, j: (l, 0, j)),
        out_shape=jax.ShapeDtypeStruct((depth, b, nm), F32),
        compiler_params=_params(("parallel", "parallel"), 40),
        name="ada_mod",
    )(c, w_ada, b_ada.reshape(depth, 1, nm))


def _prenorm_kernel(x_ref, g_ref, sh_ref, sc_ref, o_ref):
    h = _rms(x_ref[...], g_ref[...]) * (1.0 + sc_ref[0]) + sh_ref[0]
    o_ref[...] = h.astype(BF16)


def _prenorm(x2, g, shift, scale, seq):
    n, d = x2.shape
    tr = min(256, seq)
    per = seq // tr
    return pl.pallas_call(
        _prenorm_kernel,
        grid=(n // tr,),
        in_specs=[
            pl.BlockSpec((tr, d), lambda i: (i, 0)),
            pl.BlockSpec((1, d), lambda i: (0, 0)),
            pl.BlockSpec((1, 1, d), lambda i: (i // per, 0, 0)),
            pl.BlockSpec((1, 1, d), lambda i: (i // per, 0, 0)),
        ],
        out_specs=pl.BlockSpec((tr, d), lambda i: (i, 0)),
        out_shape=jax.ShapeDtypeStruct((n, d), BF16),
        compiler_params=_params(("parallel",), 40),
        name="prenorm",
    )(x2, g.reshape(1, d), shift, scale)


def _mm_kernel(a_ref, w_ref, o_ref):
    o_ref[...] = jnp.dot(a_ref[...], w_ref[...], preferred_element_type=F32).astype(o_ref.dtype)


def _mm(a, w, out_dtype, tm, tn, ride=()):
    m, k = a.shape
    n = w.shape[1]
    tm = min(tm, m)
    grid = (m // tm, n // tn)
    plans = _plan_riders(ride, grid)
    in_specs = [
        pl.BlockSpec((tm, k), lambda i, j: (i, 0)),
        pl.BlockSpec((k, tn), lambda i, j: (0, j)),
    ]
    out_specs = [pl.BlockSpec((tm, tn), lambda i, j: (i, j))]
    out_shape = [jax.ShapeDtypeStruct((m, n), out_dtype)]
    body, args = _add_riders(_mm_kernel, [a, w], in_specs, out_specs, out_shape, grid, ride, plans)
    outs = pl.pallas_call(
        body,
        grid=grid,
        in_specs=in_specs,
        out_specs=out_specs,
        out_shape=out_shape,
        compiler_params=_params(("parallel", "arbitrary"), 56),
        name="proj_in",
    )(*args)
    return _ridden(outs, ride, plans)


def _mm2_kernel(a1_ref, a2_ref, w1_ref, w2_ref, o_ref):
    acc = jnp.dot(a1_ref[...], w1_ref[...], preferred_element_type=F32)
    acc = acc + jnp.dot(a2_ref[...], w2_ref[...], preferred_element_type=F32)
    o_ref[...] = acc.astype(o_ref.dtype)


def _mm2(a1, a2, w, out_dtype, tm, tn, ride=()):
    m, k1 = a1.shape
    n = w.shape[1]
    tm = min(tm, m)
    grid = (m // tm, n // tn)
    plans = _plan_riders(ride, grid)
    in_specs = [
        pl.BlockSpec((tm, k1), lambda i, j: (i, 0)),
        pl.BlockSpec((tm, k1), lambda i, j: (i, 0)),
        pl.BlockSpec((k1, tn), lambda i, j: (0, j)),
        pl.BlockSpec((k1, tn), lambda i, j: (1, j)),
    ]
    out_specs = [pl.BlockSpec((tm, tn), lambda i, j: (i, j))]
    out_shape = [jax.ShapeDtypeStruct((m, n), out_dtype)]
    body, args = _add_riders(_mm2_kernel, [a1, a2, w, w], in_specs, out_specs, out_shape, grid, ride, plans)
    outs = pl.pallas_call(
        body,
        grid=grid,
        in_specs=in_specs,
        out_specs=out_specs,
        out_shape=out_shape,
        compiler_params=_params(("parallel", "arbitrary"), 56),
        name="proj_out",
    )(*args)
    return _ridden(outs, ride, plans)


def _split3_const(x):
    parts = []
    r = np.float32(x)
    for _ in range(3):
        p = np.float32(np.asarray(r, dtype=jnp.bfloat16))
        parts.append(float(p))
        r = np.float32(r - p)
    return parts


def _swap_halves(x):
    half = x.shape[1] // 2
    return jnp.concatenate([x[:, half:], x[:, :half]], axis=1)


def _attn_tables(slopes):
    n_heads = len(slopes)
    qx = np.zeros((n_heads, BLOCK, LANES), np.float32)
    for h, s in enumerate(slopes):
        qx[h, :, HEAD_DIM:HEAD_DIM + 3] = _split3_const(s)
    kx = np.zeros((2 * BLOCK, LANES), np.float32)
    kx[:, HEAD_DIM:HEAD_DIM + 3] = np.arange(2 * BLOCK, dtype=np.float32)[:, None]
    return jnp.asarray(qx, BF16), jnp.asarray(kx, BF16)


def _attn_kernel(slopes, sinks_ref, qx_ref, kx_ref, q_ref, kp_ref, kc_ref, vp_ref, vc_ref, o_ref):
    n = pl.program_id(1)
    n_heads = len(slopes)
    k = jnp.concatenate([kp_ref[...], kc_ref[...]], axis=0)
    v = jnp.concatenate([vp_ref[...], vc_ref[...]], axis=0)
    c_idx = lax.broadcasted_iota(I32, (BLOCK, BLOCK), 0)
    i_idx = lax.broadcasted_iota(I32, (BLOCK, BLOCK), 1)
    from_prev = c_idx > i_idx
    prev_bias = jnp.where(n > 0, 0.0, NEG_INF)
    qpos = (lax.broadcasted_iota(I32, (1, BLOCK), 1) + BLOCK).astype(F32)
    qlane = lax.broadcasted_iota(I32, (BLOCK, LANES), 1) < HEAD_DIM
    klane = lax.broadcasted_iota(I32, (2 * BLOCK, LANES), 1) < HEAD_DIM
    scale = jnp.asarray(HEAD_DIM ** -0.5, BF16)
    outs = []
    for kv in range(n_heads // GQA_GROUP):
        heads = range(kv * GQA_GROUP, (kv + 1) * GQA_GROUP)
        kt = k[:, (kv // 2) * LANES:(kv // 2 + 1) * LANES]
        vt = v[:, (kv // 2) * LANES:(kv // 2 + 1) * LANES]
        if kv % 2:
            kt, vt = _swap_halves(kt), _swap_halves(vt)
        k_aug = jnp.where(klane, kt, kx_ref[...])
        pieces = []
        for h in heads:
            qt = q_ref[:, (h // 2) * LANES:(h // 2 + 1) * LANES]
            if h % 2:
                qt = _swap_halves(qt)
            pieces.append(jnp.where(qlane, qt * scale, qx_ref[h]))
        qg = jnp.concatenate(pieces, axis=0)
        st = lax.dot_general(k_aug, qg, (((1,), (1,)), ((), ())), preferred_element_type=F32)
        pts = []
        for g, h in enumerate(heads):
            s_prev = st[:BLOCK, g * BLOCK:(g + 1) * BLOCK] + prev_bias
            s_cur = st[BLOCK:, g * BLOCK:(g + 1) * BLOCK]
            s = jnp.where(from_prev, s_prev, s_cur)
            sink = sinks_ref[h] + np.float32(slopes[h]) * qpos
            m = jnp.maximum(jnp.max(s, axis=0, keepdims=True), sink)
            e = jnp.exp(s - m)
            denom = jnp.sum(e, axis=0, keepdims=True) + jnp.exp(sink - m)
            p = e * (1.0 / denom)
            pts.append(jnp.concatenate([jnp.where(from_prev, p, 0.0).astype(BF16),
                                        jnp.where(from_prev, 0.0, p).astype(BF16)], axis=0))
        pt = jnp.concatenate(pts, axis=1)
        og = lax.dot_general(pt, vt, (((0,), (0,)), ((), ())), preferred_element_type=F32)
        for g in range(GQA_GROUP):
            outs.append(og[g * BLOCK:(g + 1) * BLOCK, :HEAD_DIM].astype(BF16))
    o_ref[...] = jnp.concatenate(outs, axis=-1)


def _attention(proj, sinks, batch, seq, d_attn, d_kv, ride=()):
    n = proj.shape[0]
    nb = seq // BLOCK
    n_heads = d_attn // HEAD_DIM
    slopes = tuple(2.0 ** (-8.0 * (h + 1) / n_heads) for h in range(n_heads))
    kcol = d_attn // d_kv
    vcol = kcol + 1

    def cur(col):
        return lambda b, i: (b * nb + i, col)

    def prev(col):
        return lambda b, i: (b * nb + jnp.maximum(i - 1, 0), col)

    grid = (batch, nb)
    plans = _plan_riders(ride, grid)
    qx, kx = _attn_tables(slopes)
    in_specs = [
        pl.BlockSpec(memory_space=pltpu.SMEM),
        pl.BlockSpec(qx.shape, lambda b, i: (0, 0, 0)),
        pl.BlockSpec(kx.shape, lambda b, i: (0, 0)),
        pl.BlockSpec((BLOCK, d_attn), cur(0)),
        pl.BlockSpec((BLOCK, d_kv), prev(kcol)),
        pl.BlockSpec((BLOCK, d_kv), cur(kcol)),
        pl.BlockSpec((BLOCK, d_kv), prev(vcol)),
        pl.BlockSpec((BLOCK, d_kv), cur(vcol)),
    ]
    out_specs = [pl.BlockSpec((BLOCK, d_attn), cur(0))]
    out_shape = [jax.ShapeDtypeStruct((n, d_attn), BF16)]
    body, args = _add_riders(functools.partial(_attn_kernel, slopes), [sinks, qx, kx, proj, proj, proj, proj, proj],
                             in_specs, out_specs, out_shape, grid, ride, plans)
    outs = pl.pallas_call(
        body,
        grid=grid,
        in_specs=in_specs,
        out_specs=out_specs,
        out_shape=out_shape,
        compiler_params=_params(("parallel", "arbitrary"), 40),
        name="swa_attention",
    )(*args)
    return _ridden(outs, ride, plans)


def _pool_kernel(u_ref, w_ref, ps_ref, o_ref):
    g = pl.program_id(1)
    u = u_ref[...].astype(F32)
    row = lax.broadcasted_iota(I32, u.shape, 0)

    def shifted(x, k):
        return jnp.where(row >= k, pltpu.roll(x, k, 0), 0.0)

    s2 = u + shifted(u, 1)
    s4 = s2 + shifted(s2, 2)
    s8 = s4 + shifted(s4, 4)
    s16 = s8 + shifted(s8, 8)
    wsum = jnp.where(g == 0, s2, jnp.where(g == 1, s4, jnp.where(g == 2, s8, s16)))
    win = jnp.left_shift(jnp.int32(POOL_WINDOWS[0]), g)
    cnt = jnp.minimum(row + 1, win).astype(F32)
    pooled = wsum / cnt - u
    y = jnp.dot(pooled.astype(BF16), w_ref[0], preferred_element_type=F32)
    o_ref[...] = (y * ps_ref[...]).astype(BF16)


def _pool(proj, w_pool, pool_scale, batch, seq, u_col0):
    n = proj.shape[0]
    ng, c, _ = w_pool.shape
    assert POOL_WINDOWS == tuple(POOL_WINDOWS[0] << g for g in range(ng))
    col0 = u_col0 // c
    return pl.pallas_call(
        _pool_kernel,
        grid=(batch, ng),
        in_specs=[
            pl.BlockSpec((seq, c), lambda b, g: (b, col0 + g)),
            pl.BlockSpec((1, c, c), lambda b, g: (g, 0, 0)),
            pl.BlockSpec((1, c), lambda b, g: (0, g)),
        ],
        out_specs=pl.BlockSpec((seq, c), lambda b, g: (b, g)),
        out_shape=jax.ShapeDtypeStruct((n, ng * c), BF16),
        compiler_params=_params(("parallel", "arbitrary"), 48),
        name="pool_mixer",
    )(proj, w_pool, pool_scale.reshape(1, ng * c))


def _top2_route(logits):
    lane = lax.broadcasted_iota(I32, logits.shape, 1).astype(F32)
    m1 = jnp.max(logits, axis=-1, keepdims=True)
    i1 = jnp.min(jnp.where(logits == m1, lane, float(LANES)), axis=-1, keepdims=True)
    l2 = jnp.where(lane == i1, -jnp.inf, logits)
    m2 = jnp.max(l2, axis=-1, keepdims=True)
    i2 = jnp.min(jnp.where(l2 == m2, lane, float(LANES)), axis=-1, keepdims=True)
    e = jnp.exp(m2 - m1)
    w1 = 1.0 / (1.0 + e)
    w2 = e / (1.0 + e)
    return jnp.where(lane == 0.0, i1, jnp.where(lane == 1.0, i2, jnp.where(lane == 2.0, w1, jnp.where(lane == 3.0, w2, 0.0))))


def _post_kernel(mode, gathered, *refs):
    refs = list(refs)
    if gathered:
        pos_ref, pos_next_ref, y_hbm, route_in = refs[:4]
        refs = refs[4:]
    else:
        f_ref = refs.pop(0)
    x_ref, gate_ref, gpost_ref = refs[:3]
    refs = refs[3:]
    if mode != "last":
        gpre_ref, sh_ref, sc_ref = refs[:3]
        refs = refs[3:]
    if mode == "route":
        rw_ref, rb_ref = refs[:2]
        refs = refs[2:]
    xo_ref = refs.pop(0)
    if mode != "last":
        ho_ref = refs.pop(0)
    if mode == "route":
        ro_ref = refs.pop(0)

    if gathered:
        ybuf, sems = refs
        tr = x_ref.shape[0]
        i = pl.program_id(0)
        slot = i % 2

        def gather(idx_ref, s):
            def start(r, carry):
                for kk in range(2):
                    pltpu.make_async_copy(y_hbm.at[pl.ds(idx_ref[0, 0, 2 * r + kk], 1)],
                                          ybuf.at[s, kk, pl.ds(r, 1)], sems.at[s]).start()
                return carry
            lax.fori_loop(0, tr, start, 0, unroll=8)

        @pl.when(i == 0)
        def _():
            gather(pos_ref, 0)

        @pl.when(i + 1 < pl.num_programs(0))
        def _():
            gather(pos_next_ref, 1 - slot)

        pltpu.make_async_copy(ybuf.at[slot], ybuf.at[slot], sems.at[slot]).wait()
        rt = route_in[...]
        a_hi, a_lo = _unpack_pair_f32(ybuf[slot, 0])
        b_hi, b_lo = _unpack_pair_f32(ybuf[slot, 1])
        w_a, w_b = rt[:, 2:3], rt[:, 3:4]
        f = jnp.concatenate([a_hi * w_a + b_hi * w_b, a_lo * w_a + b_lo * w_b], axis=1)
    else:
        f = f_ref[...].astype(F32)

    xn = x_ref[...] + gate_ref[0] * _rms(f, gpost_ref[...])
    xo_ref[...] = xn
    if mode == "last":
        return
    h = _rms(xn, gpre_ref[...]) * (1.0 + sc_ref[0]) + sh_ref[0]
    if mode == "next":
        ho_ref[...] = h.astype(BF16)
        return
    half = h.shape[1] // 2
    ho_ref[...] = _pack_pair(h[:, :half], h[:, half:])
    logits = jnp.dot(h.astype(BF16), rw_ref[...], preferred_element_type=F32) + rb_ref[...]
    ro_ref[...] = _top2_route(logits)


def _post(mode, f, x2, gate, g_post, seq, nxt=None, router=None, gather=None):
    n, d = x2.shape
    tr = min(256, seq)
    per = seq // tr
    row = lambda i: (i, 0)
    fix = lambda i: (0, 0)
    bat = lambda i: (i // per, 0, 0)
    in_specs, args, scratch = [], [], []
    if gather is not None:
        pos, y, route = gather
        last = n // tr - 1
        pos3 = pos.reshape(n // tr, 1, 2 * tr)
        in_specs += [
            pl.BlockSpec((1, 1, 2 * tr), lambda i: (i, 0, 0), memory_space=pltpu.SMEM),
            pl.BlockSpec((1, 1, 2 * tr), lambda i: (jnp.minimum(i + 1, last), 0, 0), memory_space=pltpu.SMEM),
            pl.BlockSpec(memory_space=pl.ANY),
            pl.BlockSpec((tr, LANES), row),
        ]
        args += [pos3, pos3, y, route]
        scratch = [pltpu.VMEM((2, 2, tr, d // 2), U32), pltpu.SemaphoreType.DMA((2,))]
    else:
        in_specs.append(pl.BlockSpec((tr, d), row))
        args.append(f)
    in_specs += [pl.BlockSpec((tr, d), row), pl.BlockSpec((1, 1, d), bat), pl.BlockSpec((1, d), fix)]
    args += [x2, gate, g_post.reshape(1, d)]
    out_specs = [pl.BlockSpec((tr, d), row)]
    out_shape = [jax.ShapeDtypeStruct((n, d), F32)]
    if mode != "last":
        g_pre, shift, scale = nxt
        in_specs += [pl.BlockSpec((1, d), fix), pl.BlockSpec((1, 1, d), bat), pl.BlockSpec((1, 1, d), bat)]
        args += [g_pre.reshape(1, d), shift, scale]
    if mode == "next":
        out_specs.append(pl.BlockSpec((tr, d), row))
        out_shape.append(jax.ShapeDtypeStruct((n, d), BF16))
    if mode == "route":
        rw, rb = router
        in_specs += [pl.BlockSpec((d, LANES), fix), pl.BlockSpec((1, LANES), fix)]
        args += [rw, rb]
        out_specs += [pl.BlockSpec((tr, d // 2), row), pl.BlockSpec((tr, LANES), row)]
        out_shape += [jax.ShapeDtypeStruct((n, d // 2), U32), jax.ShapeDtypeStruct((n, LANES), F32)]
    return pl.pallas_call(
        functools.partial(_post_kernel, mode, gather is not None),
        grid=(n // tr,),
        in_specs=in_specs,
        out_specs=out_specs,
        out_shape=out_shape,
        scratch_shapes=scratch,
        compiler_params=_params(("arbitrary",), 56),
        name="post_" + mode + ("_gather" if gather is not None else ""),
    )(*args)


def _swiglu_step(h_ref, wg_ref, wu_ref, wd_ref, o_ref, j, n_chunk, skip=None, init=True):
    if init:
        @pl.when(j == 0)
        def _():
            o_ref[...] = jnp.zeros_like(o_ref)

    h = h_ref[...]
    g = jnp.dot(h, wg_ref[...], preferred_element_type=F32)
    u = jnp.dot(h, wu_ref[...], preferred_element_type=F32)
    a = g * _sigmoid(g) * u
    if skip is not None:
        a = jnp.where(lax.broadcasted_iota(I32, a.shape, 1) >= skip, a, 0.0)
    a = a.astype(BF16)
    d = o_ref.shape[1]
    cw = d // n_chunk
    for c in range(n_chunk):
        o_ref[:, c * cw:(c + 1) * cw] += jnp.dot(a, wd_ref[:, c * cw:(c + 1) * cw], preferred_element_type=F32)


def _ffn_kernel(dff, h_ref, wg_ref, wu_ref, wd_ref, o_ref):
    j = pl.program_id(1)
    tf = wg_ref.shape[1]
    skip = j * tf - jnp.minimum(j * tf, dff - tf)
    _swiglu_step(h_ref, wg_ref, wu_ref, wd_ref, o_ref, j, 4, skip=skip)


def _ffn(h, wg, wu, wd, tm, tf, ride=()):
    n, d = h.shape
    dff = wg.shape[1]
    tm = min(tm, n)
    grid = (n // tm, pl.cdiv(dff, tf))
    plans = _plan_riders(ride, grid)
    assert (dff - tf) % LANES == 0
    start = lambda j: pl.multiple_of(jnp.minimum(j * tf, dff - tf), LANES)
    in_specs = [
        pl.BlockSpec((tm, d), lambda i, j: (i, 0)),
        pl.BlockSpec((pl.Element(d), pl.Element(tf)), lambda i, j: (0, start(j))),
        pl.BlockSpec((pl.Element(d), pl.Element(tf)), lambda i, j: (0, start(j))),
        pl.BlockSpec((pl.Element(tf), pl.Element(d)), lambda i, j: (start(j), 0)),
    ]
    out_specs = [pl.BlockSpec((tm, d), lambda i, j: (i, 0))]
    out_shape = [jax.ShapeDtypeStruct((n, d), F32)]
    body, args = _add_riders(functools.partial(_ffn_kernel, dff), [h, wg, wu, wd], in_specs, out_specs, out_shape,
                             grid, ride, plans)
    outs = pl.pallas_call(
        body,
        grid=grid,
        in_specs=in_specs,
        out_specs=out_specs,
        out_shape=out_shape,
        compiler_params=_params(("parallel", "arbitrary"), 58),
        name="dense_swiglu",
    )(*args)
    return _ridden(outs, ride, plans)


def _expert_kernel(te_ref, nt_ref, rows_ref, idx_ref, idx_next_ref, src_hbm, wg_ref, wu_ref, wd_ref, y_ref,
                   xs_buf, h_scr, acc, sem):
    t = pl.program_id(0)
    j = pl.program_id(1)
    n_live = nt_ref[0]
    live = t < n_live
    tm, half = xs_buf.shape
    top = tm // 2
    mostly_padding = rows_ref[t] <= top

    def gather(iref):
        def start(r, carry):
            pltpu.make_async_copy(src_hbm.at[pl.ds(iref[0, 0, r], 1)], xs_buf.at[pl.ds(r, 1)], sem).start()
            return carry
        lax.fori_loop(0, tm, start, 0, unroll=8)

    @pl.when(j == 0)
    def _():
        @pl.when(t == 0)
        def _():
            gather(idx_ref)

        @pl.when(live)
        def _():
            pltpu.make_async_copy(xs_buf, xs_buf, sem).wait()
            hi, lo = _unpack_pair(xs_buf[...])
            h_scr[:, :half] = hi
            h_scr[:, half:] = lo
            acc[...] = jnp.zeros_like(acc)

        @pl.when(t + 1 < n_live)
        def _():
            gather(idx_next_ref)

        @pl.when(jnp.logical_not(live))
        def _():
            y_ref[...] = jnp.zeros_like(y_ref)

    weights = (wg_ref.at[0], wu_ref.at[0], wd_ref.at[0])

    @pl.when(live & jnp.logical_not(mostly_padding))
    def _():
        _swiglu_step(h_scr, *weights, acc, j, 4, init=False)

    @pl.when(live & mostly_padding)
    def _():
        _swiglu_step(h_scr.at[pl.ds(0, top)], *weights, acc.at[pl.ds(0, top)], j, 4, init=False)

    @pl.when(live & (j == pl.num_programs(1) - 1))
    def _():
        y_ref[...] = _pack_pair(acc[:, :half], acc[:, half:])


def _experts(src, src_tok, tile_expert, n_tiles, tile_rows, wg, wu, wd, tm, tf):
    half = src.shape[1]
    d = 2 * half
    rows = src_tok.shape[0]
    dff = wg.shape[2]
    nj = dff // tf
    n_t = rows // tm
    idx3 = src_tok.reshape(n_t, 1, tm)

    def chunk(t, j, nt):
        return jnp.where(t < nt[0], j, nj - 1)

    return pl.pallas_call(
        _expert_kernel,
        grid_spec=pltpu.PrefetchScalarGridSpec(
            num_scalar_prefetch=3,
            grid=(n_t, nj),
            in_specs=[
                pl.BlockSpec((1, 1, tm), lambda t, j, te, nt, tr: (t, 0, 0), memory_space=pltpu.SMEM),
                pl.BlockSpec((1, 1, tm), lambda t, j, te, nt, tr: (jnp.minimum(t + 1, n_t - 1), 0, 0),
                             memory_space=pltpu.SMEM),
                pl.BlockSpec(memory_space=pl.ANY),
                pl.BlockSpec((1, d, tf), lambda t, j, te, nt, tr: (te[t], 0, chunk(t, j, nt))),
                pl.BlockSpec((1, d, tf), lambda t, j, te, nt, tr: (te[t], 0, chunk(t, j, nt))),
                pl.BlockSpec((1, tf, d), lambda t, j, te, nt, tr: (te[t], chunk(t, j, nt), 0)),
            ],
            out_specs=pl.BlockSpec((tm, half), lambda t, j, te, nt, tr: (t, 0)),
            scratch_shapes=[pltpu.VMEM((tm, half), U32), pltpu.VMEM((tm, d), BF16), pltpu.VMEM((tm, d), F32),
                            pltpu.SemaphoreType.DMA(())],
        ),
        out_shape=jax.ShapeDtypeStruct((rows, half), U32),
        compiler_params=_params(("arbitrary", "arbitrary"), 58),
        name="expert_swiglu",
    )(tile_expert, n_tiles, tile_rows, idx3, idx3, src, wg, wu, wd)


def _moe_plan(top_idx, tm):
    n = top_idx.shape[0]
    p = 2 * n
    flat = top_idx.reshape(p)
    experts = jnp.arange(N_EXPERTS, dtype=I32)
    onehot = (flat[:, None] == experts[None, :]).astype(I32)
    csum = jnp.cumsum(onehot, axis=0)
    rank = jnp.sum(onehot * csum, axis=1) - 1
    counts = csum[-1]
    tiles_per = (counts + (tm - 1)) // tm
    tile_end = jnp.cumsum(tiles_per)
    tile_start = tile_end - tiles_per
    pos = (tile_start * tm)[flat] + rank
    n_tiles = tile_end[-1:]
    t_max = p // tm + N_EXPERTS
    t_ids = jnp.arange(t_max, dtype=I32)
    te = jnp.sum((t_ids[:, None] >= tile_end[None, :]).astype(I32), axis=1)
    last_e = jnp.max(jnp.where(tiles_per > 0, experts, 0))
    te = jnp.minimum(te, last_e)
    tile_rows = jnp.clip(counts[te] - (t_ids - tile_start[te]) * tm, 0, tm)
    src_tok = jnp.zeros((t_max * tm,), I32).at[pos].set(jnp.arange(p, dtype=I32) // 2)
    return pos.astype(I32), src_tok, te.astype(I32), n_tiles.astype(I32), tile_rows.astype(I32)


def kernel(x, c, w_ada, b_ada, norm_pre_mix, norm_post_mix, norm_pre_ffn, norm_post_ffn, w_in, sinks, w_pool,
           pool_scale, w_out, ffn_w_gate, ffn_w_up, ffn_w_down, router_w, router_b, moe_w_gate, moe_w_up, moe_w_down):
    batch, seq, d = x.shape
    depth = w_ada.shape[0]
    n = batch * seq
    d_attn = d // 2
    n_heads = sinks.shape[1]
    d_kv = (n_heads // GQA_GROUP) * HEAD_DIM
    u_col0 = d_attn + 2 * d_kv
    assert n_heads * HEAD_DIM == d_attn and seq % BLOCK == 0

    x2 = x.reshape(n, d)
    mod = _ada(c, w_ada, b_ada)

    def mods(l):
        return [mod[l, :, k * d:(k + 1) * d].reshape(batch, 1, d) for k in range(N_MOD)]

    shift1, scale1, gate1, shift2, scale2, gate2 = mods(0)
    h = _prenorm(x2, norm_pre_mix[0], shift1, scale1, seq)
    flat = lambda w: w.reshape(-1, w.shape[-1])
    ready = {}

    def hosted(host, rides, *a):
        out, casts = host(*a, ride=[(w, lead) for _, w, lead in rides])
        ready.update({key: v for (key, _, _), v in zip(rides, casts)})
        return out

    def take(key, w, lead=None):
        shape = w.shape if lead is None else w.shape[1:]
        return ready.pop(key).reshape(shape) if key in ready else _select(w, lead).astype(BF16)

    for l in range(depth):
        i = l // 2
        dense = l % 2 == 0
        nm = (l + 1) // 2 if (dense and l + 1 < depth) else None
        on_proj_in, on_attn, on_proj_out, on_ffn = [(("w_out", l), w_out, l)], [], [], []
        if dense:
            on_proj_in += [(("ffn_wg", i), ffn_w_gate, i), (("ffn_wd", i), ffn_w_down, i)]
            on_proj_out.append((("ffn_wu", i), ffn_w_up, i))
            if nm is not None:
                on_attn.append((("moe_wg", nm), flat(moe_w_gate[nm]), None))
                on_ffn.append((("moe_wd", nm), flat(moe_w_down[nm]), None))
            if l + 1 < depth:
                on_ffn.append((("w_in", l + 1), w_in, l + 1))
        else:
            on_attn.append((("moe_wu", i), flat(moe_w_up[i]), None))

        proj = hosted(_mm, on_proj_in, h, take(("w_in", l), w_in, l), BF16, PROJ_ROWS, PROJ_IN_COLS)
        attn = hosted(_attention, on_attn, proj, sinks[l], batch, seq, d_attn, d_kv)
        pool = _pool(proj, w_pool[l].astype(BF16), pool_scale[l], batch, seq, u_col0)
        mix = hosted(_mm2, on_proj_out, attn, pool, take(("w_out", l), w_out, l), BF16, PROJ_ROWS, PROJ_OUT_COLS)
        nxt = (norm_pre_ffn[l], shift2, scale2)
        if dense:
            x2, h2 = _post("next", mix, x2, gate1, norm_post_mix[l], seq, nxt=nxt)
            f = hosted(_ffn, on_ffn, h2, take(("ffn_wg", i), ffn_w_gate, i), take(("ffn_wu", i), ffn_w_up, i),
                       take(("ffn_wd", i), ffn_w_down, i), FFN_ROWS, FFN_CHUNK)
            gather = None
        else:
            rw = jnp.zeros((d, LANES), BF16).at[:, :N_EXPERTS].set(router_w[i].astype(BF16))
            rb = jnp.full((1, LANES), NEG_INF, F32).at[0, :N_EXPERTS].set(router_b[i])
            x2, h2p, route = _post("route", mix, x2, gate1, norm_post_mix[l], seq, nxt=nxt, router=(rw, rb))
            pos, src_tok, tile_expert, n_tiles, tile_rows = _moe_plan(route[:, :2].astype(I32), FFN_ROWS)
            y = _experts(h2p, src_tok, tile_expert, n_tiles, tile_rows, take(("moe_wg", i), moe_w_gate[i]),
                         take(("moe_wu", i), moe_w_up[i]), take(("moe_wd", i), moe_w_down[i]), FFN_ROWS, FFN_CHUNK)
            f, gather = None, (pos, y, route)
        if l + 1 < depth:
            shift1, scale1, gate1n, shift2n, scale2n, gate2n = mods(l + 1)
            x2, h = _post("next", f, x2, gate2, norm_post_ffn[l], seq,
                          nxt=(norm_pre_mix[l + 1], shift1, scale1), gather=gather)
            gate1, shift2, scale2, gate2 = gate1n, shift2n, scale2n, gate2n
        else:
            (x2,) = _post("last", f, x2, gate2, norm_post_ffn[l], seq, gather=gather)
    return x2.reshape(batch, seq, d)
```

```python
import functools

import jax
import jax.numpy as jnp
import numpy as np
from jax import lax
from jax.experimental import pallas as pl
from jax.experimental.pallas import tpu as pltpu

F32 = jnp.float32
BF16 = jnp.bfloat16
U32 = jnp.uint32
I32 = jnp.int32

EPS = 1e-6
NEG_INF = -1e30
BLOCK = 128
HEAD_DIM = 64
GQA_GROUP = 8
POOL_WINDOWS = (2, 4, 8, 16)
N_EXPERTS = 8
N_MOD = 6
LANES = 128
PROJ_ROWS = 1024
PROJ_IN_COLS = 768
PROJ_IN_COLS_HOST = 512
PROJ_OUT_COLS = 1024
FFN_ROWS = 512
FFN_CHUNK = 512
BF16_SUBLANES = 16
RIDER_BLOCK_BYTES = 4 << 20

MIB = 1 << 20


def _params(sem, vmem_mib):
    return pltpu.CompilerParams(dimension_semantics=sem, vmem_limit_bytes=vmem_mib * MIB)


def _sigmoid(x):
    return 1.0 / (1.0 + jnp.exp(-x))


def _rms(x, g):
    return x * lax.rsqrt(jnp.mean(x * x, axis=-1, keepdims=True) + EPS) * g


def _pack_pair(a, b):
    ua = pltpu.bitcast(a.astype(BF16).astype(F32), U32)
    ub = pltpu.bitcast(b.astype(BF16).astype(F32), U32)
    return ua | (ub >> 16)


def _unpack_pair_f32(p):
    return pltpu.bitcast(p & jnp.uint32(0xFFFF0000), F32), pltpu.bitcast(p << 16, F32)


def _unpack_pair(p):
    hi, lo = _unpack_pair_f32(p)
    return hi.astype(BF16), lo.astype(BF16)


def _select(w, lead):
    return w if lead is None else w[lead]


def _plan_riders(ride, grid):
    steps = grid[0] * grid[1]
    plans = []
    for w, _ in ride:
        rows, cols = w.shape[-2:]
        plan = None
        for rb in range(BF16_SUBLANES, rows + 1, BF16_SUBLANES):
            if rows % rb == 0 and rows // rb <= steps:
                plan = rb if rb * cols * 4 <= RIDER_BLOCK_BYTES else None
                break
        plans.append(plan)
    return plans


def _ridden(outs, ride, plans):
    extra = iter(outs[1:])
    return outs[0], tuple(next(extra) if rb is not None else _select(w, lead).astype(BF16)
                          for (w, lead), rb in zip(ride, plans))


def _add_riders(kernel_fn, args, in_specs, out_specs, out_shape, grid, ride, plans):
    riders = [(w, lead, rb) for (w, lead), rb in zip(ride, plans) if rb is not None]
    if not riders:
        return kernel_fn, list(args)
    n1 = grid[1]
    n_in, n_out, n_r = len(in_specs), len(out_specs), len(riders)
    for w, lead, rb in riders:
        rows, cols = w.shape[-2:]
        last = rows // rb - 1
        blk = functools.partial(lambda a, b, last: (jnp.minimum(a * n1 + b, last), 0), last=last)
        if lead is None:
            in_specs.append(pl.BlockSpec((rb, cols), blk))
        else:
            in_specs.append(pl.BlockSpec((None, rb, cols), functools.partial(
                lambda a, b, last, lead: (lead, jnp.minimum(a * n1 + b, last), 0), last=last, lead=lead)))
        out_specs.append(pl.BlockSpec((rb, cols), blk))
        out_shape.append(jax.ShapeDtypeStruct((rows, cols), BF16))

    def body(*refs):
        ins, srcs = refs[:n_in], refs[n_in:n_in + n_r]
        refs = refs[n_in + n_r:]
        outs, dsts = refs[:n_out], refs[n_out:n_out + n_r]
        kernel_fn(*ins, *outs, *refs[n_out + n_r:])
        for src, dst in zip(srcs, dsts):
            dst[...] = src[...].astype(BF16)

    return body, list(args) + [w for w, _, _ in riders]


def _ada_kernel(c_ref, w_ref, b_ref, o_ref):
    c = c_ref[...]
    ca = (c * _sigmoid(c)).astype(BF16)
    o_ref[0] = jnp.dot(ca, w_ref[0].astype(BF16), preferred_element_type=F32) + b_ref[0]


def _ada(c, w_ada, b_ada):
    depth, d, nm = w_ada.shape
    b = c.shape[0]
    tn = 512
    return pl.pallas_call(
        _ada_kernel,
        grid=(depth, nm // tn),
        in_specs=[
            pl.BlockSpec((b, d), lambda l, j: (0, 0)),
            pl.BlockSpec((1, d, tn), lambda l, j: (l, 0, j)),
            pl.BlockSpec((1, 1, tn), lambda l, j: (l, 0, j)),
        ],
        out_specs=pl.BlockSpec((1, b, tn), lambda l, j: (l, 0, j)),
        out_shape=jax.ShapeDtypeStruct((depth, b, nm), F32),
        compiler_params=_params(("parallel", "parallel"), 40),
        name="ada_mod",
    )(c, w_ada, b_ada.reshape(depth, 1, nm))


def _prenorm_kernel(x_ref, g_ref, sh_ref, sc_ref, o_ref):
    h = _rms(x_ref[...], g_ref[...]) * (1.0 + sc_ref[0]) + sh_ref[0]
    o_ref[...] = h.astype(BF16)


def _prenorm(x2, g, shift, scale, seq):
    n, d = x2.shape
    tr = min(256, seq)
    per = seq // tr
    return pl.pallas_call(
        _prenorm_kernel,
        grid=(n // tr,),
        in_specs=[
            pl.BlockSpec((tr, d), lambda i: (i, 0)),
            pl.BlockSpec((1, d), lambda i: (0, 0)),
            pl.BlockSpec((1, 1, d), lambda i: (i // per, 0, 0)),
            pl.BlockSpec((1, 1, d), lambda i: (i // per, 0, 0)),
        ],
        out_specs=pl.BlockSpec((tr, d), lambda i: (i, 0)),
        out_shape=jax.ShapeDtypeStruct((n, d), BF16),
        compiler_params=_params(("parallel",), 40),
        name="prenorm",
    )(x2, g.reshape(1, d), shift, scale)


def _mm_kernel(a_ref, w_ref, o_ref):
    o_ref[...] = jnp.dot(a_ref[...], w_ref[...], preferred_element_type=F32).astype(o_ref.dtype)


def _mm(a, w, out_dtype, tm, tn, ride=()):
    m, k = a.shape
    n = w.shape[1]
    tm = min(tm, m)
    grid = (m // tm, n // tn)
    plans = _plan_riders(ride, grid)
    in_specs = [
        pl.BlockSpec((tm, k), lambda i, j: (i, 0)),
        pl.BlockSpec((k, tn), lambda i, j: (0, j)),
    ]
    out_specs = [pl.BlockSpec((tm, tn), lambda i, j: (i, j))]
    out_shape = [jax.ShapeDtypeStruct((m, n), out_dtype)]
    body, args = _add_riders(_mm_kernel, [a, w], in_specs, out_specs, out_shape, grid, ride, plans)
    outs = pl.pallas_call(
        body,
        grid=grid,
        in_specs=in_specs,
        out_specs=out_specs,
        out_shape=out_shape,
        compiler_params=_params(("parallel", "arbitrary"), 56),
        name="proj_in",
    )(*args)
    return _ridden(outs, ride, plans)


def _mm2_kernel(a1_ref, a2_ref, w1_ref, w2_ref, o_ref):
    acc = jnp.dot(a1_ref[...], w1_ref[...], preferred_element_type=F32)
    acc = acc + jnp.dot(a2_ref[...], w2_ref[...], preferred_element_type=F32)
    o_ref[...] = acc.astype(o_ref.dtype)


def _mm2(a1, a2, w, out_dtype, tm, tn, ride=()):
    m, k1 = a1.shape
    n = w.shape[1]
    tm = min(tm, m)
    grid = (m // tm, n // tn)
    plans = _plan_riders(ride, grid)
    in_specs = [
        pl.BlockSpec((tm, k1), lambda i, j: (i, 0)),
        pl.BlockSpec((tm, k1), lambda i, j: (i, 0)),
        pl.BlockSpec((k1, tn), lambda i, j: (0, j)),
        pl.BlockSpec((k1, tn), lambda i, j: (1, j)),
    ]
    out_specs = [pl.BlockSpec((tm, tn), lambda i, j: (i, j))]
    out_shape = [jax.ShapeDtypeStruct((m, n), out_dtype)]
    body, args = _add_riders(_mm2_kernel, [a1, a2, w, w], in_specs, out_specs, out_shape, grid, ride, plans)
    outs = pl.pallas_call(
        body,
        grid=grid,
        in_specs=in_specs,
        out_specs=out_specs,
        out_shape=out_shape,
        compiler_params=_params(("parallel", "arbitrary"), 56),
        name="proj_out",
    )(*args)
    return _ridden(outs, ride, plans)


def _split3_const(x):
    parts = []
    r = np.float32(x)
    for _ in range(3):
        p = np.float32(np.asarray(r, dtype=jnp.bfloat16))
        parts.append(float(p))
        r = np.float32(r - p)
    return parts


def _swap_halves(x):
    half = x.shape[1] // 2
    return jnp.concatenate([x[:, half:], x[:, :half]], axis=1)


def _attn_tables(slopes):
    n_heads = len(slopes)
    qx = np.zeros((n_heads, BLOCK, LANES), np.float32)
    for h, s in enumerate(slopes):
        qx[h, :, HEAD_DIM:HEAD_DIM + 3] = _split3_const(s)
    kx = np.zeros((2 * BLOCK, LANES), np.float32)
    kx[:, HEAD_DIM:HEAD_DIM + 3] = np.arange(2 * BLOCK, dtype=np.float32)[:, None]
    return jnp.asarray(qx, BF16), jnp.asarray(kx, BF16)


def _attn_kernel(slopes, sinks_ref, qx_ref, kx_ref, q_ref, kp_ref, kc_ref, vp_ref, vc_ref, o_ref):
    n = pl.program_id(1)
    n_heads = len(slopes)
    k = jnp.concatenate([kp_ref[...], kc_ref[...]], axis=0)
    v = jnp.concatenate([vp_ref[...], vc_ref[...]], axis=0)
    c_idx = lax.broadcasted_iota(I32, (BLOCK, BLOCK), 0)
    i_idx = lax.broadcasted_iota(I32, (BLOCK, BLOCK), 1)
    from_prev = c_idx > i_idx
    prev_bias = jnp.where(n > 0, 0.0, NEG_INF)
    qpos = (lax.broadcasted_iota(I32, (1, BLOCK), 1) + BLOCK).astype(F32)
    qlane = lax.broadcasted_iota(I32, (BLOCK, LANES), 1) < HEAD_DIM
    klane = lax.broadcasted_iota(I32, (2 * BLOCK, LANES), 1) < HEAD_DIM
    scale = jnp.asarray(HEAD_DIM ** -0.5, BF16)
    outs = []
    for kv in range(n_heads // GQA_GROUP):
        heads = range(kv * GQA_GROUP, (kv + 1) * GQA_GROUP)
        kt = k[:, (kv // 2) * LANES:(kv // 2 + 1) * LANES]
        vt = v[:, (kv // 2) * LANES:(kv // 2 + 1) * LANES]
        if kv % 2:
            kt, vt = _swap_halves(kt), _swap_halves(vt)
        k_aug = jnp.where(klane, kt, kx_ref[...])
        pieces = []
        for h in heads:
            qt = q_ref[:, (h // 2) * LANES:(h // 2 + 1) * LANES]
            if h % 2:
                qt = _swap_halves(qt)
            pieces.append(jnp.where(qlane, qt * scale, qx_ref[h]))
        qg = jnp.concatenate(pieces, axis=0)
        st = lax.dot_general(k_aug, qg, (((1,), (1,)), ((), ())), preferred_element_type=F32)
        pts = []
        for g, h in enumerate(heads):
            s_prev = st[:BLOCK, g * BLOCK:(g + 1) * BLOCK] + prev_bias
            s_cur = st[BLOCK:, g * BLOCK:(g + 1) * BLOCK]
            s = jnp.where(from_prev, s_prev, s_cur)
            sink = sinks_ref[h] + np.float32(slopes[h]) * qpos
            m = jnp.maximum(jnp.max(s, axis=0, keepdims=True), sink)
            e = jnp.exp(s - m)
            denom = jnp.sum(e, axis=0, keepdims=True) + jnp.exp(sink - m)
            p = e * (1.0 / denom)
            pts.append(jnp.concatenate([jnp.where(from_prev, p, 0.0).astype(BF16),
                                        jnp.where(from_prev, 0.0, p).astype(BF16)], axis=0))
        pt = jnp.concatenate(pts, axis=1)
        og = lax.dot_general(pt, vt, (((0,), (0,)), ((), ())), preferred_element_type=F32)
        for g in range(GQA_GROUP):
            outs.append(og[g * BLOCK:(g + 1) * BLOCK, :HEAD_DIM].astype(BF16))
    o_ref[...] = jnp.concatenate(outs, axis=-1)


def _attention(proj, sinks, batch, seq, d_attn, d_kv, ride=()):
    n = proj.shape[0]
    nb = seq // BLOCK
    n_heads = d_attn // HEAD_DIM
    slopes = tuple(2.0 ** (-8.0 * (h + 1) / n_heads) for h in range(n_heads))
    kcol = d_attn // d_kv
    vcol = kcol + 1

    def cur(col):
        return lambda b, i: (b * nb + i, col)

    def prev(col):
        return lambda b, i: (b * nb + jnp.maximum(i - 1, 0), col)

    grid = (batch, nb)
    plans = _plan_riders(ride, grid)
    qx, kx = _attn_tables(slopes)
    in_specs = [
        pl.BlockSpec(memory_space=pltpu.SMEM),
        pl.BlockSpec(qx.shape, lambda b, i: (0, 0, 0)),
        pl.BlockSpec(kx.shape, lambda b, i: (0, 0)),
        pl.BlockSpec((BLOCK, d_attn), cur(0)),
        pl.BlockSpec((BLOCK, d_kv), prev(kcol)),
        pl.BlockSpec((BLOCK, d_kv), cur(kcol)),
        pl.BlockSpec((BLOCK, d_kv), prev(vcol)),
        pl.BlockSpec((BLOCK, d_kv), cur(vcol)),
    ]
    out_specs = [pl.BlockSpec((BLOCK, d_attn), cur(0))]
    out_shape = [jax.ShapeDtypeStruct((n, d_attn), BF16)]
    body, args = _add_riders(functools.partial(_attn_kernel, slopes), [sinks, qx, kx, proj, proj, proj, proj, proj],
                             in_specs, out_specs, out_shape, grid, ride, plans)
    outs = pl.pallas_call(
        body,
        grid=grid,
        in_specs=in_specs,
        out_specs=out_specs,
        out_shape=out_shape,
        compiler_params=_params(("parallel", "arbitrary"), 40),
        name="swa_attention",
    )(*args)
    return _ridden(outs, ride, plans)


def _pool_kernel(u_ref, w_ref, ps_ref, o_ref):
    g = pl.program_id(1)
    u = u_ref[...].astype(F32)
    row = lax.broadcasted_iota(I32, u.shape, 0)

    def shifted(x, k):
        return jnp.where(row >= k, pltpu.roll(x, k, 0), 0.0)

    s2 = u + shifted(u, 1)
    s4 = s2 + shifted(s2, 2)
    s8 = s4 + shifted(s4, 4)
    s16 = s8 + shifted(s8, 8)
    wsum = jnp.where(g == 0, s2, jnp.where(g == 1, s4, jnp.where(g == 2, s8, s16)))
    win = jnp.left_shift(jnp.int32(POOL_WINDOWS[0]), g)
    cnt = jnp.minimum(row + 1, win).astype(F32)
    pooled = wsum / cnt - u
    y = jnp.dot(pooled.astype(BF16), w_ref[0], preferred_element_type=F32)
    o_ref[...] = (y * ps_ref[...]).astype(BF16)


def _pool(proj, w_pool, pool_scale, batch, seq, u_col0):
    n = proj.shape[0]
    ng, c, _ = w_pool.shape
    assert POOL_WINDOWS == tuple(POOL_WINDOWS[0] << g for g in range(ng))
    col0 = u_col0 // c
    return pl.pallas_call(
        _pool_kernel,
        grid=(batch, ng),
        in_specs=[
            pl.BlockSpec((seq, c), lambda b, g: (b, col0 + g)),
            pl.BlockSpec((1, c, c), lambda b, g: (g, 0, 0)),
            pl.BlockSpec((1, c), lambda b, g: (0, g)),
        ],
        out_specs=pl.BlockSpec((seq, c), lambda b, g: (b, g)),
        out_shape=jax.ShapeDtypeStruct((n, ng * c), BF16),
        compiler_params=_params(("parallel", "arbitrary"), 48),
        name="pool_mixer",
    )(proj, w_pool, pool_scale.reshape(1, ng * c))


def _top2_route(logits):
    lane = lax.broadcasted_iota(I32, logits.shape, 1).astype(F32)
    m1 = jnp.max(logits, axis=-1, keepdims=True)
    i1 = jnp.min(jnp.where(logits == m1, lane, float(LANES)), axis=-1, keepdims=True)
    l2 = jnp.where(lane == i1, -jnp.inf, logits)
    m2 = jnp.max(l2, axis=-1, keepdims=True)
    i2 = jnp.min(jnp.where(l2 == m2, lane, float(LANES)), axis=-1, keepdims=True)
    e = jnp.exp(m2 - m1)
    w1 = 1.0 / (1.0 + e)
    w2 = e / (1.0 + e)
    return jnp.where(lane == 0.0, i1, jnp.where(lane == 1.0, i2, jnp.where(lane == 2.0, w1, jnp.where(lane == 3.0, w2, 0.0))))


def _post_kernel(mode, gathered, *refs):
    refs = list(refs)
    if gathered:
        pos_ref, pos_next_ref, y_hbm, route_in = refs[:4]
        refs = refs[4:]
    else:
        f_ref = refs.pop(0)
    x_ref, gate_ref, gpost_ref = refs[:3]
    refs = refs[3:]
    if mode != "last":
        gpre_ref, sh_ref, sc_ref = refs[:3]
        refs = refs[3:]
    if mode == "route":
        rw_ref, rb_ref = refs[:2]
        refs = refs[2:]
    xo_ref = refs.pop(0)
    if mode != "last":
        ho_ref = refs.pop(0)
    if mode == "route":
        ro_ref = refs.pop(0)

    if gathered:
        ybuf, sems = refs
        tr = x_ref.shape[0]
        i = pl.program_id(0)
        slot = i % 2

        def gather(idx_ref, s):
            def start(r, carry):
                for kk in range(2):
                    pltpu.make_async_copy(y_hbm.at[pl.ds(idx_ref[0, 0, 2 * r + kk], 1)],
                                          ybuf.at[s, kk, pl.ds(r, 1)], sems.at[s]).start()
                return carry
            lax.fori_loop(0, tr, start, 0, unroll=8)

        @pl.when(i == 0)
        def _():
            gather(pos_ref, 0)

        @pl.when(i + 1 < pl.num_programs(0))
        def _():
            gather(pos_next_ref, 1 - slot)

        pltpu.make_async_copy(ybuf.at[slot], ybuf.at[slot], sems.at[slot]).wait()
        rt = route_in[...]
        a_hi, a_lo = _unpack_pair_f32(ybuf[slot, 0])
        b_hi, b_lo = _unpack_pair_f32(ybuf[slot, 1])
        w_a, w_b = rt[:, 2:3], rt[:, 3:4]
        f = jnp.concatenate([a_hi * w_a + b_hi * w_b, a_lo * w_a + b_lo * w_b], axis=1)
    else:
        f = f_ref[...].astype(F32)

    xn = x_ref[...] + gate_ref[0] * _rms(f, gpost_ref[...])
    xo_ref[...] = xn
    if mode == "last":
        return
    h = _rms(xn, gpre_ref[...]) * (1.0 + sc_ref[0]) + sh_ref[0]
    if mode == "next":
        ho_ref[...] = h.astype(BF16)
        return
    half = h.shape[1] // 2
    ho_ref[...] = _pack_pair(h[:, :half], h[:, half:])
    logits = jnp.dot(h.astype(BF16), rw_ref[...], preferred_element_type=F32) + rb_ref[...]
    ro_ref[...] = _top2_route(logits)


def _post(mode, f, x2, gate, g_post, seq, nxt=None, router=None, gather=None):
    n, d = x2.shape
    tr = min(256, seq)
    per = seq // tr
    row = lambda i: (i, 0)
    fix = lambda i: (0, 0)
    bat = lambda i: (i // per, 0, 0)
    in_specs, args, scratch = [], [], []
    if gather is not None:
        pos, y, route = gather
        last = n // tr - 1
        pos3 = pos.reshape(n // tr, 1, 2 * tr)
        in_specs += [
            pl.BlockSpec((1, 1, 2 * tr), lambda i: (i, 0, 0), memory_space=pltpu.SMEM),
            pl.BlockSpec((1, 1, 2 * tr), lambda i: (jnp.minimum(i + 1, last), 0, 0), memory_space=pltpu.SMEM),
            pl.BlockSpec(memory_space=pl.ANY),
            pl.BlockSpec((tr, LANES), row),
        ]
        args += [pos3, pos3, y, route]
        scratch = [pltpu.VMEM((2, 2, tr, d // 2), U32), pltpu.SemaphoreType.DMA((2,))]
    else:
        in_specs.append(pl.BlockSpec((tr, d), row))
        args.append(f)
    in_specs += [pl.BlockSpec((tr, d), row), pl.BlockSpec((1, 1, d), bat), pl.BlockSpec((1, d), fix)]
    args += [x2, gate, g_post.reshape(1, d)]
    out_specs = [pl.BlockSpec((tr, d), row)]
    out_shape = [jax.ShapeDtypeStruct((n, d), F32)]
    if mode != "last":
        g_pre, shift, scale = nxt
        in_specs += [pl.BlockSpec((1, d), fix), pl.BlockSpec((1, 1, d), bat), pl.BlockSpec((1, 1, d), bat)]
        args += [g_pre.reshape(1, d), shift, scale]
    if mode == "next":
        out_specs.append(pl.BlockSpec((tr, d), row))
        out_shape.append(jax.ShapeDtypeStruct((n, d), BF16))
    if mode == "route":
        rw, rb = router
        in_specs += [pl.BlockSpec((d, LANES), fix), pl.BlockSpec((1, LANES), fix)]
        args += [rw, rb]
        out_specs += [pl.BlockSpec((tr, d // 2), row), pl.BlockSpec((tr, LANES), row)]
        out_shape += [jax.ShapeDtypeStruct((n, d // 2), U32), jax.ShapeDtypeStruct((n, LANES), F32)]
    return pl.pallas_call(
        functools.partial(_post_kernel, mode, gather is not None),
        grid=(n // tr,),
        in_specs=in_specs,
        out_specs=out_specs,
        out_shape=out_shape,
        scratch_shapes=scratch,
        compiler_params=_params(("arbitrary",), 56),
        name="post_" + mode + ("_gather" if gather is not None else ""),
    )(*args)


def _swiglu_step(h_ref, wg_ref, wu_ref, wd_ref, o_ref, j, n_chunk, skip=None, init=True):
    if init:
        @pl.when(j == 0)
        def _():
            o_ref[...] = jnp.zeros_like(o_ref)

    h = h_ref[...]
    g = jnp.dot(h, wg_ref[...], preferred_element_type=F32)
    u = jnp.dot(h, wu_ref[...], preferred_element_type=F32)
    a = g * _sigmoid(g) * u
    if skip is not None:
        a = jnp.where(lax.broadcasted_iota(I32, a.shape, 1) >= skip, a, 0.0)
    a = a.astype(BF16)
    d = o_ref.shape[1]
    cw = d // n_chunk
    for c in range(n_chunk):
        o_ref[:, c * cw:(c + 1) * cw] += jnp.dot(a, wd_ref[:, c * cw:(c + 1) * cw], preferred_element_type=F32)


def _ffn_kernel(dff, h_ref, wg_ref, wu_ref, wd_ref, o_ref):
    j = pl.program_id(1)
    tf = wg_ref.shape[1]
    skip = j * tf - jnp.minimum(j * tf, dff - tf)
    _swiglu_step(h_ref, wg_ref, wu_ref, wd_ref, o_ref, j, 4, skip=skip)


def _ffn(h, wg, wu, wd, tm, tf, ride=()):
    n, d = h.shape
    dff = wg.shape[1]
    tm = min(tm, n)
    grid = (n // tm, pl.cdiv(dff, tf))
    plans = _plan_riders(ride, grid)
    assert (dff - tf) % LANES == 0
    start = lambda j: pl.multiple_of(jnp.minimum(j * tf, dff - tf), LANES)
    in_specs = [
        pl.BlockSpec((tm, d), lambda i, j: (i, 0)),
        pl.BlockSpec((pl.Element(d), pl.Element(tf)), lambda i, j: (0, start(j))),
        pl.BlockSpec((pl.Element(d), pl.Element(tf)), lambda i, j: (0, start(j))),
        pl.BlockSpec((pl.Element(tf), pl.Element(d)), lambda i, j: (start(j), 0)),
    ]
    out_specs = [pl.BlockSpec((tm, d), lambda i, j: (i, 0))]
    out_shape = [jax.ShapeDtypeStruct((n, d), F32)]
    body, args = _add_riders(functools.partial(_ffn_kernel, dff), [h, wg, wu, wd], in_specs, out_specs, out_shape,
                             grid, ride, plans)
    outs = pl.pallas_call(
        body,
        grid=grid,
        in_specs=in_specs,
        out_specs=out_specs,
        out_shape=out_shape,
        compiler_params=_params(("parallel", "arbitrary"), 58),
        name="dense_swiglu",
    )(*args)
    return _ridden(outs, ride, plans)


def _ffn_loop_kernel(dff, tf, h_ref, wg_hbm, wu_hbm, wd_hbm, o_ref, wgb, wub, wdb, sems):
    i = pl.program_id(0)
    n_full = dff // tf
    tail = dff - n_full * tf

    def copies(c, slot, width):
        start = pl.multiple_of(c * tf, LANES)
        return (pltpu.make_async_copy(wg_hbm.at[:, pl.ds(start, width)], wgb.at[slot, :, pl.ds(0, width)],
                                      sems.at[slot, 0]),
                pltpu.make_async_copy(wu_hbm.at[:, pl.ds(start, width)], wub.at[slot, :, pl.ds(0, width)],
                                      sems.at[slot, 1]),
                pltpu.make_async_copy(wd_hbm.at[pl.ds(start, width), :], wdb.at[slot, pl.ds(0, width), :],
                                      sems.at[slot, 2]))

    def start(c, slot, width=tf):
        for cp in copies(c, slot, width):
            cp.start()

    def wait(c, slot, width=tf):
        for cp in copies(c, slot, width):
            cp.wait()

    def multiply(slot, width=tf):
        h = h_ref[...]
        g = jnp.dot(h, wgb[slot, :, :width], preferred_element_type=F32)
        u = jnp.dot(h, wub[slot, :, :width], preferred_element_type=F32)
        a = (g * _sigmoid(g) * u).astype(BF16)
        cw = o_ref.shape[1] // 4
        for q in range(4):
            o_ref[:, q * cw:(q + 1) * cw] += jnp.dot(a, wdb[slot, :width, q * cw:(q + 1) * cw],
                                                     preferred_element_type=F32)

    @pl.when(i == 0)
    def _():
        start(0, 0)

    o_ref[...] = jnp.zeros_like(o_ref)

    def step(c, carry):
        slot = c % 2
        wait(c, slot)
        start(c + 1, 1 - slot)
        multiply(slot)
        return carry

    lax.fori_loop(0, n_full - 1, step, 0)
    last, last_slot = n_full - 1, (n_full - 1) % 2
    wait(last, last_slot)
    if tail:
        start(n_full, 1 - last_slot, tail)
    else:
        @pl.when(i + 1 < pl.num_programs(0))
        def _():
            start(0, 1 - last_slot)
    multiply(last_slot)
    if tail:
        wait(n_full, 1 - last_slot, tail)

        @pl.when(i + 1 < pl.num_programs(0))
        def _():
            start(0, 0)
        multiply(1 - last_slot, tail)


def _ffn_loop(h, wg, wu, wd, tm, tf):
    n, d = h.shape
    dff = wg.shape[1]
    tm = min(tm, n)
    n_full = dff // tf
    assert n_full >= 2 and (n_full + (1 if dff % tf else 0)) % 2 == 0 and (dff % tf) % LANES == 0
    return pl.pallas_call(
        functools.partial(_ffn_loop_kernel, dff, tf),
        grid=(n // tm,),
        in_specs=[
            pl.BlockSpec((tm, d), lambda i: (i, 0)),
            pl.BlockSpec(memory_space=pl.ANY),
            pl.BlockSpec(memory_space=pl.ANY),
            pl.BlockSpec(memory_space=pl.ANY),
        ],
        out_specs=pl.BlockSpec((tm, d), lambda i: (i, 0)),
        out_shape=jax.ShapeDtypeStruct((n, d), F32),
        scratch_shapes=[pltpu.VMEM((2, d, tf), BF16), pltpu.VMEM((2, d, tf), BF16), pltpu.VMEM((2, tf, d), BF16),
                        pltpu.SemaphoreType.DMA((2, 3))],
        compiler_params=_params(("arbitrary",), 58),
        name="dense_swiglu",
    )(h, wg, wu, wd)


def _expert_kernel(te_ref, nt_ref, rows_ref, idx_ref, idx_next_ref, src_hbm, wg_ref, wu_ref, wd_ref, y_ref,
                   xs_buf, h_scr, acc, sem):
    t = pl.program_id(0)
    j = pl.program_id(1)
    n_live = nt_ref[0]
    live = t < n_live
    tm, half = xs_buf.shape
    top = tm // 2
    mostly_padding = rows_ref[t] <= top

    def gather(iref):
        def start(r, carry):
            pltpu.make_async_copy(src_hbm.at[pl.ds(iref[0, 0, r], 1)], xs_buf.at[pl.ds(r, 1)], sem).start()
            return carry
        lax.fori_loop(0, tm, start, 0, unroll=8)

    @pl.when(j == 0)
    def _():
        @pl.when(t == 0)
        def _():
            gather(idx_ref)

        @pl.when(live)
        def _():
            pltpu.make_async_copy(xs_buf, xs_buf, sem).wait()
            hi, lo = _unpack_pair(xs_buf[...])
            h_scr[:, :half] = hi
            h_scr[:, half:] = lo
            acc[...] = jnp.zeros_like(acc)

        @pl.when(t + 1 < n_live)
        def _():
            gather(idx_next_ref)

        @pl.when(jnp.logical_not(live))
        def _():
            y_ref[...] = jnp.zeros_like(y_ref)

    weights = (wg_ref.at[0], wu_ref.at[0], wd_ref.at[0])

    @pl.when(live & jnp.logical_not(mostly_padding))
    def _():
        _swiglu_step(h_scr, *weights, acc, j, 4, init=False)

    @pl.when(live & mostly_padding)
    def _():
        _swiglu_step(h_scr.at[pl.ds(0, top)], *weights, acc.at[pl.ds(0, top)], j, 4, init=False)

    @pl.when(live & (j == pl.num_programs(1) - 1))
    def _():
        y_ref[...] = _pack_pair(acc[:, :half], acc[:, half:])


def _experts(src, src_tok, tile_expert, n_tiles, tile_rows, wg, wu, wd, tm, tf):
    half = src.shape[1]
    d = 2 * half
    rows = src_tok.shape[0]
    dff = wg.shape[2]
    nj = dff // tf
    n_t = rows // tm
    idx3 = src_tok.reshape(n_t, 1, tm)

    def chunk(t, j, nt):
        return jnp.where(t < nt[0], j, nj - 1)

    return pl.pallas_call(
        _expert_kernel,
        grid_spec=pltpu.PrefetchScalarGridSpec(
            num_scalar_prefetch=3,
            grid=(n_t, nj),
            in_specs=[
                pl.BlockSpec((1, 1, tm), lambda t, j, te, nt, tr: (t, 0, 0), memory_space=pltpu.SMEM),
                pl.BlockSpec((1, 1, tm), lambda t, j, te, nt, tr: (jnp.minimum(t + 1, n_t - 1), 0, 0),
                             memory_space=pltpu.SMEM),
                pl.BlockSpec(memory_space=pl.ANY),
                pl.BlockSpec((1, d, tf), lambda t, j, te, nt, tr: (te[t], 0, chunk(t, j, nt))),
                pl.BlockSpec((1, d, tf), lambda t, j, te, nt, tr: (te[t], 0, chunk(t, j, nt))),
                pl.BlockSpec((1, tf, d), lambda t, j, te, nt, tr: (te[t], chunk(t, j, nt), 0)),
            ],
            out_specs=pl.BlockSpec((tm, half), lambda t, j, te, nt, tr: (t, 0)),
            scratch_shapes=[pltpu.VMEM((tm, half), U32), pltpu.VMEM((tm, d), BF16), pltpu.VMEM((tm, d), F32),
                            pltpu.SemaphoreType.DMA(())],
        ),
        out_shape=jax.ShapeDtypeStruct((rows, half), U32),
        compiler_params=_params(("arbitrary", "arbitrary"), 58),
        name="expert_swiglu",
    )(tile_expert, n_tiles, tile_rows, idx3, idx3, src, wg, wu, wd)


def _moe_plan(top_idx, tm):
    n = top_idx.shape[0]
    p = 2 * n
    flat = top_idx.reshape(p)
    experts = jnp.arange(N_EXPERTS, dtype=I32)
    onehot = (flat[:, None] == experts[None, :]).astype(I32)
    csum = jnp.cumsum(onehot, axis=0)
    rank = jnp.sum(onehot * csum, axis=1) - 1
    counts = csum[-1]
    tiles_per = (counts + (tm - 1)) // tm
    tile_end = jnp.cumsum(tiles_per)
    tile_start = tile_end - tiles_per
    pos = (tile_start * tm)[flat] + rank
    n_tiles = tile_end[-1:]
    t_max = p // tm + N_EXPERTS
    t_ids = jnp.arange(t_max, dtype=I32)
    te = jnp.sum((t_ids[:, None] >= tile_end[None, :]).astype(I32), axis=1)
    last_e = jnp.max(jnp.where(tiles_per > 0, experts, 0))
    te = jnp.minimum(te, last_e)
    tile_rows = jnp.clip(counts[te] - (t_ids - tile_start[te]) * tm, 0, tm)
    src_tok = jnp.zeros((t_max * tm,), I32).at[pos].set(jnp.arange(p, dtype=I32) // 2)
    return pos.astype(I32), src_tok, te.astype(I32), n_tiles.astype(I32), tile_rows.astype(I32)


def kernel(x, c, w_ada, b_ada, norm_pre_mix, norm_post_mix, norm_pre_ffn, norm_post_ffn, w_in, sinks, w_pool,
           pool_scale, w_out, ffn_w_gate, ffn_w_up, ffn_w_down, router_w, router_b, moe_w_gate, moe_w_up, moe_w_down):
    batch, seq, d = x.shape
    depth = w_ada.shape[0]
    n = batch * seq
    d_attn = d // 2
    n_heads = sinks.shape[1]
    d_kv = (n_heads // GQA_GROUP) * HEAD_DIM
    u_col0 = d_attn + 2 * d_kv
    assert n_heads * HEAD_DIM == d_attn and seq % BLOCK == 0

    x2 = x.reshape(n, d)
    mod = _ada(c, w_ada, b_ada)

    def mods(l):
        return [mod[l, :, k * d:(k + 1) * d].reshape(batch, 1, d) for k in range(N_MOD)]

    shift1, scale1, gate1, shift2, scale2, gate2 = mods(0)
    h = _prenorm(x2, norm_pre_mix[0], shift1, scale1, seq)
    flat = lambda w: w.reshape(-1, w.shape[-1])
    ready = {}

    def hosted(host, rides, *a):
        out, casts = host(*a, ride=[(w, lead) for _, w, lead in rides])
        ready.update({key: v for (key, _, _), v in zip(rides, casts)})
        return out

    def take(key, w, lead=None):
        shape = w.shape if lead is None else w.shape[1:]
        return ready.pop(key).reshape(shape) if key in ready else _select(w, lead).astype(BF16)

    for l in range(depth):
        i = l // 2
        dense = l % 2 == 0
        nm = (l + 1) // 2 if (dense and l + 1 < depth) else None
        on_proj_in, on_attn, on_proj_out = [(("w_out", l), w_out, l)], [], []
        if dense:
            on_proj_in += [(("ffn_wg", i), ffn_w_gate, i), (("ffn_wd", i), ffn_w_down, i)]
            on_proj_out.append((("ffn_wu", i), ffn_w_up, i))
            if nm is not None:
                on_attn.append((("moe_wg", nm), flat(moe_w_gate[nm]), None))
            if l + 1 < depth:
                on_proj_out.append((("w_in", l + 1), w_in, l + 1))
        else:
            on_attn.append((("moe_wu", i), flat(moe_w_up[i]), None))
            on_proj_in.append((("moe_wd", i), flat(moe_w_down[i]), None))

        proj = hosted(_mm, on_proj_in, h, take(("w_in", l), w_in, l), BF16, PROJ_ROWS,
                      PROJ_IN_COLS if dense else PROJ_IN_COLS_HOST)
        attn = hosted(_attention, on_attn, proj, sinks[l], batch, seq, d_attn, d_kv)
        pool = _pool(proj, w_pool[l].astype(BF16), pool_scale[l], batch, seq, u_col0)
        mix = hosted(_mm2, on_proj_out, attn, pool, take(("w_out", l), w_out, l), BF16, PROJ_ROWS, PROJ_OUT_COLS)
        nxt = (norm_pre_ffn[l], shift2, scale2)
        if dense:
            x2, h2 = _post("next", mix, x2, gate1, norm_post_mix[l], seq, nxt=nxt)
            f = _ffn_loop(h2, take(("ffn_wg", i), ffn_w_gate, i), take(("ffn_wu", i), ffn_w_up, i),
                          take(("ffn_wd", i), ffn_w_down, i), FFN_ROWS, FFN_CHUNK)
            gather = None
        else:
            rw = jnp.zeros((d, LANES), BF16).at[:, :N_EXPERTS].set(router_w[i].astype(BF16))
            rb = jnp.full((1, LANES), NEG_INF, F32).at[0, :N_EXPERTS].set(router_b[i])
            x2, h2p, route = _post("route", mix, x2, gate1, norm_post_mix[l], seq, nxt=nxt, router=(rw, rb))
            pos, src_tok, tile_expert, n_tiles, tile_rows = _moe_plan(route[:, :2].astype(I32), FFN_ROWS)
            y = _experts(h2p, src_tok, tile_expert, n_tiles, tile_rows, take(("moe_wg", i), moe_w_gate[i]),
                         take(("moe_wu", i), moe_w_up[i]), take(("moe_wd", i), moe_w_down[i]), FFN_ROWS, FFN_CHUNK)
            f, gather = None, (pos, y, route)
        if l + 1 < depth:
            shift1, scale1, gate1n, shift2n, scale2n, gate2n = mods(l + 1)
            x2, h = _post("next", f, x2, gate2, norm_post_ffn[l], seq,
                          nxt=(norm_pre_mix[l + 1], shift1, scale1), gather=gather)
            gate1, shift2, scale2, gate2 = gate1n, shift2n, scale2n, gate2n
        else:
            (x2,) = _post("last", f, x2, gate2, norm_post_ffn[l], seq, gather=gather)
    return x2.reshape(batch, seq, d)
```

```python
import functools

import jax
import jax.numpy as jnp
import numpy as np
from jax import lax
from jax.experimental import pallas as pl
from jax.experimental.pallas import tpu as pltpu

F32 = jnp.float32
BF16 = jnp.bfloat16
U32 = jnp.uint32
I32 = jnp.int32

EPS = 1e-6
NEG_INF = -1e30
BLOCK = 128
HEAD_DIM = 64
GQA_GROUP = 8
POOL_WINDOWS = (2, 4, 8, 16)
N_EXPERTS = 8
N_MOD = 6
LANES = 128
PROJ_ROWS = 1024
PROJ_IN_COLS = 768
PROJ_IN_COLS_HOST = 512
PROJ_OUT_COLS = 1024
FFN_ROWS = 512
FFN_CHUNK = 512
BF16_SUBLANES = 16
RIDER_BLOCK_BYTES = 4 << 20

MIB = 1 << 20


def _params(sem, vmem_mib):
    return pltpu.CompilerParams(dimension_semantics=sem, vmem_limit_bytes=vmem_mib * MIB)


def _sigmoid(x):
    return 1.0 / (1.0 + jnp.exp(-x))


def _rms(x, g):
    return x * lax.rsqrt(jnp.mean(x * x, axis=-1, keepdims=True) + EPS) * g


def _pack_pair(a, b):
    ua = pltpu.bitcast(a.astype(BF16).astype(F32), U32)
    ub = pltpu.bitcast(b.astype(BF16).astype(F32), U32)
    return ua | (ub >> 16)


def _unpack_pair_f32(p):
    return pltpu.bitcast(p & jnp.uint32(0xFFFF0000), F32), pltpu.bitcast(p << 16, F32)


def _unpack_pair(p):
    hi, lo = _unpack_pair_f32(p)
    return hi.astype(BF16), lo.astype(BF16)


def _select(w, lead):
    return w if lead is None else w[lead]


def _plan_riders(ride, grid):
    steps = grid[0] * grid[1]
    plans = []
    for w, _ in ride:
        rows, cols = w.shape[-2:]
        plan = None
        for rb in range(BF16_SUBLANES, rows + 1, BF16_SUBLANES):
            if rows % rb == 0 and rows // rb <= steps:
                plan = rb if rb * cols * 4 <= RIDER_BLOCK_BYTES else None
                break
        plans.append(plan)
    return plans


def _ridden(outs, ride, plans):
    extra = iter(outs[1:])
    return outs[0], tuple(next(extra) if rb is not None else _select(w, lead).astype(BF16)
                          for (w, lead), rb in zip(ride, plans))


def _add_riders(kernel_fn, args, in_specs, out_specs, out_shape, grid, ride, plans):
    riders = [(w, lead, rb) for (w, lead), rb in zip(ride, plans) if rb is not None]
    if not riders:
        return kernel_fn, list(args)
    n1 = grid[1]
    n_in, n_out, n_r = len(in_specs), len(out_specs), len(riders)
    for w, lead, rb in riders:
        rows, cols = w.shape[-2:]
        last = rows // rb - 1
        blk = functools.partial(lambda a, b, last: (jnp.minimum(a * n1 + b, last), 0), last=last)
        if lead is None:
            in_specs.append(pl.BlockSpec((rb, cols), blk))
        else:
            in_specs.append(pl.BlockSpec((None, rb, cols), functools.partial(
                lambda a, b, last, lead: (lead, jnp.minimum(a * n1 + b, last), 0), last=last, lead=lead)))
        out_specs.append(pl.BlockSpec((rb, cols), blk))
        out_shape.append(jax.ShapeDtypeStruct((rows, cols), BF16))

    def body(*refs):
        ins, srcs = refs[:n_in], refs[n_in:n_in + n_r]
        refs = refs[n_in + n_r:]
        outs, dsts = refs[:n_out], refs[n_out:n_out + n_r]
        kernel_fn(*ins, *outs, *refs[n_out + n_r:])
        for src, dst in zip(srcs, dsts):
            dst[...] = src[...].astype(BF16)

    return body, list(args) + [w for w, _, _ in riders]


def _ada_kernel(c_ref, w_ref, b_ref, o_ref):
    c = c_ref[...]
    ca = (c * _sigmoid(c)).astype(BF16)
    o_ref[0] = jnp.dot(ca, w_ref[0].astype(BF16), preferred_element_type=F32) + b_ref[0]


def _ada(c, w_ada, b_ada):
    depth, d, nm = w_ada.shape
    b = c.shape[0]
    tn = 512
    return pl.pallas_call(
        _ada_kernel,
        grid=(depth, nm // tn),
        in_specs=[
            pl.BlockSpec((b, d), lambda l, j: (0, 0)),
            pl.BlockSpec((1, d, tn), lambda l, j: (l, 0, j)),
            pl.BlockSpec((1, 1, tn), lambda l, j: (l, 0, j)),
        ],
        out_specs=pl.BlockSpec((1, b, tn), lambda l, j: (l, 0, j)),
        out_shape=jax.ShapeDtypeStruct((depth, b, nm), F32),
        compiler_params=_params(("parallel", "parallel"), 40),
        name="ada_mod",
    )(c, w_ada, b_ada.reshape(depth, 1, nm))


def _prenorm_kernel(x_ref, g_ref, sh_ref, sc_ref, o_ref):
    h = _rms(x_ref[...], g_ref[...]) * (1.0 + sc_ref[0]) + sh_ref[0]
    o_ref[...] = h.astype(BF16)


def _prenorm(x2, g, shift, scale, seq):
    n, d = x2.shape
    tr = min(256, seq)
    per = seq // tr
    return pl.pallas_call(
        _prenorm_kernel,
        grid=(n // tr,),
        in_specs=[
            pl.BlockSpec((tr, d), lambda i: (i, 0)),
            pl.BlockSpec((1, d), lambda i: (0, 0)),
            pl.BlockSpec((1, 1, d), lambda i: (i // per, 0, 0)),
            pl.BlockSpec((1, 1, d), lambda i: (i // per, 0, 0)),
        ],
        out_specs=pl.BlockSpec((tr, d), lambda i: (i, 0)),
        out_shape=jax.ShapeDtypeStruct((n, d), BF16),
        compiler_params=_params(("parallel",), 40),
        name="prenorm",
    )(x2, g.reshape(1, d), shift, scale)


def _mm_kernel(a_ref, w_ref, o_ref):
    o_ref[...] = jnp.dot(a_ref[...], w_ref[...], preferred_element_type=F32).astype(o_ref.dtype)


def _mm(a, w, out_dtype, tm, tn, ride=()):
    m, k = a.shape
    n = w.shape[1]
    tm = min(tm, m)
    grid = (m // tm, n // tn)
    plans = _plan_riders(ride, grid)
    in_specs = [
        pl.BlockSpec((tm, k), lambda i, j: (i, 0)),
        pl.BlockSpec((k, tn), lambda i, j: (0, j)),
    ]
    out_specs = [pl.BlockSpec((tm, tn), lambda i, j: (i, j))]
    out_shape = [jax.ShapeDtypeStruct((m, n), out_dtype)]
    body, args = _add_riders(_mm_kernel, [a, w], in_specs, out_specs, out_shape, grid, ride, plans)
    outs = pl.pallas_call(
        body,
        grid=grid,
        in_specs=in_specs,
        out_specs=out_specs,
        out_shape=out_shape,
        compiler_params=_params(("parallel", "arbitrary"), 56),
        name="proj_in",
    )(*args)
    return _ridden(outs, ride, plans)


def _mm2_kernel(a1_ref, a2_ref, w1_ref, w2_ref, o_ref):
    acc = jnp.dot(a1_ref[...], w1_ref[...], preferred_element_type=F32)
    acc = acc + jnp.dot(a2_ref[...], w2_ref[...], preferred_element_type=F32)
    o_ref[...] = acc.astype(o_ref.dtype)


def _mm2(a1, a2, w, out_dtype, tm, tn, ride=()):
    m, k1 = a1.shape
    n = w.shape[1]
    tm = min(tm, m)
    grid = (m // tm, n // tn)
    plans = _plan_riders(ride, grid)
    in_specs = [
        pl.BlockSpec((tm, k1), lambda i, j: (i, 0)),
        pl.BlockSpec((tm, k1), lambda i, j: (i, 0)),
        pl.BlockSpec((k1, tn), lambda i, j: (0, j)),
        pl.BlockSpec((k1, tn), lambda i, j: (1, j)),
    ]
    out_specs = [pl.BlockSpec((tm, tn), lambda i, j: (i, j))]
    out_shape = [jax.ShapeDtypeStruct((m, n), out_dtype)]
    body, args = _add_riders(_mm2_kernel, [a1, a2, w, w], in_specs, out_specs, out_shape, grid, ride, plans)
    outs = pl.pallas_call(
        body,
        grid=grid,
        in_specs=in_specs,
        out_specs=out_specs,
        out_shape=out_shape,
        compiler_params=_params(("parallel", "arbitrary"), 56),
        name="proj_out",
    )(*args)
    return _ridden(outs, ride, plans)


def _split3_const(x):
    parts = []
    r = np.float32(x)
    for _ in range(3):
        p = np.float32(np.asarray(r, dtype=jnp.bfloat16))
        parts.append(float(p))
        r = np.float32(r - p)
    return parts


def _swap_halves(x):
    half = x.shape[1] // 2
    return jnp.concatenate([x[:, half:], x[:, :half]], axis=1)


def _attn_tables(slopes):
    n_heads = len(slopes)
    qx = np.zeros((n_heads, BLOCK, LANES), np.float32)
    for h, s in enumerate(slopes):
        qx[h, :, HEAD_DIM:HEAD_DIM + 3] = _split3_const(s)
    kx = np.zeros((2 * BLOCK, LANES), np.float32)
    kx[:, HEAD_DIM:HEAD_DIM + 3] = np.arange(2 * BLOCK, dtype=np.float32)[:, None]
    return jnp.asarray(qx, BF16), jnp.asarray(kx, BF16)


def _attn_kernel(slopes, sinks_ref, qx_ref, kx_ref, q_ref, kp_ref, kc_ref, vp_ref, vc_ref, o_ref):
    n = pl.program_id(1)
    n_heads = len(slopes)
    k = jnp.concatenate([kp_ref[...], kc_ref[...]], axis=0)
    v = jnp.concatenate([vp_ref[...], vc_ref[...]], axis=0)
    c_idx = lax.broadcasted_iota(I32, (BLOCK, BLOCK), 0)
    i_idx = lax.broadcasted_iota(I32, (BLOCK, BLOCK), 1)
    from_prev = c_idx > i_idx
    prev_bias = jnp.where(n > 0, 0.0, NEG_INF)
    qpos = (lax.broadcasted_iota(I32, (1, BLOCK), 1) + BLOCK).astype(F32)
    qlane = lax.broadcasted_iota(I32, (BLOCK, LANES), 1) < HEAD_DIM
    klane = lax.broadcasted_iota(I32, (2 * BLOCK, LANES), 1) < HEAD_DIM
    scale = jnp.asarray(HEAD_DIM ** -0.5, BF16)
    outs = []
    for kv in range(n_heads // GQA_GROUP):
        heads = range(kv * GQA_GROUP, (kv + 1) * GQA_GROUP)
        kt = k[:, (kv // 2) * LANES:(kv // 2 + 1) * LANES]
        vt = v[:, (kv // 2) * LANES:(kv // 2 + 1) * LANES]
        if kv % 2:
            kt, vt = _swap_halves(kt), _swap_halves(vt)
        k_aug = jnp.where(klane, kt, kx_ref[...])
        pieces = []
        for h in heads:
            qt = q_ref[:, (h // 2) * LANES:(h // 2 + 1) * LANES]
            if h % 2:
                qt = _swap_halves(qt)
            pieces.append(jnp.where(qlane, qt * scale, qx_ref[h]))
        qg = jnp.concatenate(pieces, axis=0)
        st = lax.dot_general(k_aug, qg, (((1,), (1,)), ((), ())), preferred_element_type=F32)
        pts = []
        for g, h in enumerate(heads):
            s_prev = st[:BLOCK, g * BLOCK:(g + 1) * BLOCK] + prev_bias
            s_cur = st[BLOCK:, g * BLOCK:(g + 1) * BLOCK]
            s = jnp.where(from_prev, s_prev, s_cur)
            sink = sinks_ref[h] + np.float32(slopes[h]) * qpos
            m = jnp.maximum(jnp.max(s, axis=0, keepdims=True), sink)
            e = jnp.exp(s - m)
            denom = jnp.sum(e, axis=0, keepdims=True) + jnp.exp(sink - m)
            p = e * (1.0 / denom)
            pts.append(jnp.concatenate([jnp.where(from_prev, p, 0.0).astype(BF16),
                                        jnp.where(from_prev, 0.0, p).astype(BF16)], axis=0))
        pt = jnp.concatenate(pts, axis=1)
        og = lax.dot_general(pt, vt, (((0,), (0,)), ((), ())), preferred_element_type=F32)
        for g in range(GQA_GROUP):
            outs.append(og[g * BLOCK:(g + 1) * BLOCK, :HEAD_DIM].astype(BF16))
    o_ref[...] = jnp.concatenate(outs, axis=-1)


def _attention(proj, sinks, batch, seq, d_attn, d_kv, ride=()):
    n = proj.shape[0]
    nb = seq // BLOCK
    n_heads = d_attn // HEAD_DIM
    slopes = tuple(2.0 ** (-8.0 * (h + 1) / n_heads) for h in range(n_heads))
    kcol = d_attn // d_kv
    vcol = kcol + 1

    def cur(col):
        return lambda b, i: (b * nb + i, col)

    def prev(col):
        return lambda b, i: (b * nb + jnp.maximum(i - 1, 0), col)

    grid = (batch, nb)
    plans = _plan_riders(ride, grid)
    qx, kx = _attn_tables(slopes)
    in_specs = [
        pl.BlockSpec(memory_space=pltpu.SMEM),
        pl.BlockSpec(qx.shape, lambda b, i: (0, 0, 0)),
        pl.BlockSpec(kx.shape, lambda b, i: (0, 0)),
        pl.BlockSpec((BLOCK, d_attn), cur(0)),
        pl.BlockSpec((BLOCK, d_kv), prev(kcol)),
        pl.BlockSpec((BLOCK, d_kv), cur(kcol)),
        pl.BlockSpec((BLOCK, d_kv), prev(vcol)),
        pl.BlockSpec((BLOCK, d_kv), cur(vcol)),
    ]
    out_specs = [pl.BlockSpec((BLOCK, d_attn), cur(0))]
    out_shape = [jax.ShapeDtypeStruct((n, d_attn), BF16)]
    body, args = _add_riders(functools.partial(_attn_kernel, slopes), [sinks, qx, kx, proj, proj, proj, proj, proj],
                             in_specs, out_specs, out_shape, grid, ride, plans)
    outs = pl.pallas_call(
        body,
        grid=grid,
        in_specs=in_specs,
        out_specs=out_specs,
        out_shape=out_shape,
        compiler_params=_params(("parallel", "arbitrary"), 40),
        name="swa_attention",
    )(*args)
    return _ridden(outs, ride, plans)


def _pool_kernel(u_ref, w_ref, ps_ref, o_ref):
    g = pl.program_id(1)
    u = u_ref[...].astype(F32)
    row = lax.broadcasted_iota(I32, u.shape, 0)

    def shifted(x, k):
        return jnp.where(row >= k, pltpu.roll(x, k, 0), 0.0)

    s2 = u + shifted(u, 1)
    s4 = s2 + shifted(s2, 2)
    s8 = s4 + shifted(s4, 4)
    s16 = s8 + shifted(s8, 8)
    wsum = jnp.where(g == 0, s2, jnp.where(g == 1, s4, jnp.where(g == 2, s8, s16)))
    win = jnp.left_shift(jnp.int32(POOL_WINDOWS[0]), g)
    cnt = jnp.minimum(row + 1, win).astype(F32)
    pooled = wsum / cnt - u
    y = jnp.dot(pooled.astype(BF16), w_ref[0], preferred_element_type=F32)
    o_ref[...] = (y * ps_ref[...]).astype(BF16)


def _pool(proj, w_pool, pool_scale, batch, seq, u_col0):
    n = proj.shape[0]
    ng, c, _ = w_pool.shape
    assert POOL_WINDOWS == tuple(POOL_WINDOWS[0] << g for g in range(ng))
    col0 = u_col0 // c
    return pl.pallas_call(
        _pool_kernel,
        grid=(batch, ng),
        in_specs=[
            pl.BlockSpec((seq, c), lambda b, g: (b, col0 + g)),
            pl.BlockSpec((1, c, c), lambda b, g: (g, 0, 0)),
            pl.BlockSpec((1, c), lambda b, g: (0, g)),
        ],
        out_specs=pl.BlockSpec((seq, c), lambda b, g: (b, g)),
        out_shape=jax.ShapeDtypeStruct((n, ng * c), BF16),
        compiler_params=_params(("parallel", "arbitrary"), 48),
        name="pool_mixer",
    )(proj, w_pool, pool_scale.reshape(1, ng * c))


def _top2_route(logits):
    lane = lax.broadcasted_iota(I32, logits.shape, 1).astype(F32)
    m1 = jnp.max(logits, axis=-1, keepdims=True)
    i1 = jnp.min(jnp.where(logits == m1, lane, float(LANES)), axis=-1, keepdims=True)
    l2 = jnp.where(lane == i1, -jnp.inf, logits)
    m2 = jnp.max(l2, axis=-1, keepdims=True)
    i2 = jnp.min(jnp.where(l2 == m2, lane, float(LANES)), axis=-1, keepdims=True)
    e = jnp.exp(m2 - m1)
    w1 = 1.0 / (1.0 + e)
    w2 = e / (1.0 + e)
    return jnp.where(lane == 0.0, i1, jnp.where(lane == 1.0, i2, jnp.where(lane == 2.0, w1, jnp.where(lane == 3.0, w2, 0.0))))


def _post_kernel(mode, gathered, *refs):
    refs = list(refs)
    if gathered:
        pos_ref, pos_next_ref, y_hbm, route_in = refs[:4]
        refs = refs[4:]
    else:
        f_ref = refs.pop(0)
    x_ref, gate_ref, gpost_ref = refs[:3]
    refs = refs[3:]
    if mode != "last":
        gpre_ref, sh_ref, sc_ref = refs[:3]
        refs = refs[3:]
    if mode == "route":
        rw_ref, rb_ref = refs[:2]
        refs = refs[2:]
    xo_ref = refs.pop(0)
    if mode != "last":
        ho_ref = refs.pop(0)
    if mode == "route":
        ro_ref = refs.pop(0)

    if gathered:
        ybuf, sems = refs
        tr = x_ref.shape[0]
        i = pl.program_id(0)
        slot = i % 2

        def gather(idx_ref, s):
            def start(r, carry):
                for kk in range(2):
                    pltpu.make_async_copy(y_hbm.at[pl.ds(idx_ref[0, 0, 2 * r + kk], 1)],
                                          ybuf.at[s, kk, pl.ds(r, 1)], sems.at[s]).start()
                return carry
            lax.fori_loop(0, tr, start, 0, unroll=8)

        @pl.when(i == 0)
        def _():
            gather(pos_ref, 0)

        @pl.when(i + 1 < pl.num_programs(0))
        def _():
            gather(pos_next_ref, 1 - slot)

        pltpu.make_async_copy(ybuf.at[slot], ybuf.at[slot], sems.at[slot]).wait()
        rt = route_in[...]
        a_hi, a_lo = _unpack_pair_f32(ybuf[slot, 0])
        b_hi, b_lo = _unpack_pair_f32(ybuf[slot, 1])
        w_a, w_b = rt[:, 2:3], rt[:, 3:4]
        f = jnp.concatenate([a_hi * w_a + b_hi * w_b, a_lo * w_a + b_lo * w_b], axis=1)
    else:
        f = f_ref[...].astype(F32)

    xn = x_ref[...] + gate_ref[0] * _rms(f, gpost_ref[...])
    xo_ref[...] = xn
    if mode == "last":
        return
    h = _rms(xn, gpre_ref[...]) * (1.0 + sc_ref[0]) + sh_ref[0]
    if mode == "next":
        ho_ref[...] = h.astype(BF16)
        return
    half = h.shape[1] // 2
    ho_ref[...] = _pack_pair(h[:, :half], h[:, half:])
    logits = jnp.dot(h.astype(BF16), rw_ref[...], preferred_element_type=F32) + rb_ref[...]
    ro_ref[...] = _top2_route(logits)


def _post(mode, f, x2, gate, g_post, seq, nxt=None, router=None, gather=None):
    n, d = x2.shape
    tr = min(256, seq)
    per = seq // tr
    row = lambda i: (i, 0)
    fix = lambda i: (0, 0)
    bat = lambda i: (i // per, 0, 0)
    in_specs, args, scratch = [], [], []
    if gather is not None:
        pos, y, route = gather
        last = n // tr - 1
        pos3 = pos.reshape(n // tr, 1, 2 * tr)
        in_specs += [
            pl.BlockSpec((1, 1, 2 * tr), lambda i: (i, 0, 0), memory_space=pltpu.SMEM),
            pl.BlockSpec((1, 1, 2 * tr), lambda i: (jnp.minimum(i + 1, last), 0, 0), memory_space=pltpu.SMEM),
            pl.BlockSpec(memory_space=pl.ANY),
            pl.BlockSpec((tr, LANES), row),
        ]
        args += [pos3, pos3, y, route]
        scratch = [pltpu.VMEM((2, 2, tr, d // 2), U32), pltpu.SemaphoreType.DMA((2,))]
    else:
        in_specs.append(pl.BlockSpec((tr, d), row))
        args.append(f)
    in_specs += [pl.BlockSpec((tr, d), row), pl.BlockSpec((1, 1, d), bat), pl.BlockSpec((1, d), fix)]
    args += [x2, gate, g_post.reshape(1, d)]
    out_specs = [pl.BlockSpec((tr, d), row)]
    out_shape = [jax.ShapeDtypeStruct((n, d), F32)]
    if mode != "last":
        g_pre, shift, scale = nxt
        in_specs += [pl.BlockSpec((1, d), fix), pl.BlockSpec((1, 1, d), bat), pl.BlockSpec((1, 1, d), bat)]
        args += [g_pre.reshape(1, d), shift, scale]
    if mode == "next":
        out_specs.append(pl.BlockSpec((tr, d), row))
        out_shape.append(jax.ShapeDtypeStruct((n, d), BF16))
    if mode == "route":
        rw, rb = router
        in_specs += [pl.BlockSpec((d, LANES), fix), pl.BlockSpec((1, LANES), fix)]
        args += [rw, rb]
        out_specs += [pl.BlockSpec((tr, d // 2), row), pl.BlockSpec((tr, LANES), row)]
        out_shape += [jax.ShapeDtypeStruct((n, d // 2), U32), jax.ShapeDtypeStruct((n, LANES), F32)]
    return pl.pallas_call(
        functools.partial(_post_kernel, mode, gather is not None),
        grid=(n // tr,),
        in_specs=in_specs,
        out_specs=out_specs,
        out_shape=out_shape,
        scratch_shapes=scratch,
        compiler_params=_params(("arbitrary",), 56),
        name="post_" + mode + ("_gather" if gather is not None else ""),
    )(*args)


def _swiglu_step(h_ref, wg_ref, wu_ref, wd_ref, o_ref, j, n_chunk, skip=None, init=True):
    if init:
        @pl.when(j == 0)
        def _():
            o_ref[...] = jnp.zeros_like(o_ref)

    h = h_ref[...]
    g = jnp.dot(h, wg_ref[...], preferred_element_type=F32)
    u = jnp.dot(h, wu_ref[...], preferred_element_type=F32)
    a = g * _sigmoid(g) * u
    if skip is not None:
        a = jnp.where(lax.broadcasted_iota(I32, a.shape, 1) >= skip, a, 0.0)
    a = a.astype(BF16)
    d = o_ref.shape[1]
    cw = d // n_chunk
    for c in range(n_chunk):
        o_ref[:, c * cw:(c + 1) * cw] += jnp.dot(a, wd_ref[:, c * cw:(c + 1) * cw], preferred_element_type=F32)


def _ffn_kernel(dff, h_ref, wg_ref, wu_ref, wd_ref, o_ref):
    j = pl.program_id(1)
    tf = wg_ref.shape[1]
    skip = j * tf - jnp.minimum(j * tf, dff - tf)
    _swiglu_step(h_ref, wg_ref, wu_ref, wd_ref, o_ref, j, 4, skip=skip)


def _ffn(h, wg, wu, wd, tm, tf, ride=()):
    n, d = h.shape
    dff = wg.shape[1]
    tm = min(tm, n)
    grid = (n // tm, pl.cdiv(dff, tf))
    plans = _plan_riders(ride, grid)
    assert (dff - tf) % LANES == 0
    start = lambda j: pl.multiple_of(jnp.minimum(j * tf, dff - tf), LANES)
    in_specs = [
        pl.BlockSpec((tm, d), lambda i, j: (i, 0)),
        pl.BlockSpec((pl.Element(d), pl.Element(tf)), lambda i, j: (0, start(j))),
        pl.BlockSpec((pl.Element(d), pl.Element(tf)), lambda i, j: (0, start(j))),
        pl.BlockSpec((pl.Element(tf), pl.Element(d)), lambda i, j: (start(j), 0)),
    ]
    out_specs = [pl.BlockSpec((tm, d), lambda i, j: (i, 0))]
    out_shape = [jax.ShapeDtypeStruct((n, d), F32)]
    body, args = _add_riders(functools.partial(_ffn_kernel, dff), [h, wg, wu, wd], in_specs, out_specs, out_shape,
                             grid, ride, plans)
    outs = pl.pallas_call(
        body,
        grid=grid,
        in_specs=in_specs,
        out_specs=out_specs,
        out_shape=out_shape,
        compiler_params=_params(("parallel", "arbitrary"), 58),
        name="dense_swiglu",
    )(*args)
    return _ridden(outs, ride, plans)


def _ffn_loop_kernel(dff, tf, h_ref, wg_hbm, wu_hbm, wd_hbm, o_ref, wgb, wub, wdb, sems):
    i = pl.program_id(0)
    n_full = dff // tf
    tail = dff - n_full * tf

    def copies(c, slot, width):
        start = pl.multiple_of(c * tf, LANES)
        return (pltpu.make_async_copy(wg_hbm.at[:, pl.ds(start, width)], wgb.at[slot, :, pl.ds(0, width)],
                                      sems.at[slot, 0]),
                pltpu.make_async_copy(wu_hbm.at[:, pl.ds(start, width)], wub.at[slot, :, pl.ds(0, width)],
                                      sems.at[slot, 1]),
                pltpu.make_async_copy(wd_hbm.at[pl.ds(start, width), :], wdb.at[slot, pl.ds(0, width), :],
                                      sems.at[slot, 2]))

    def start(c, slot, width=tf):
        for cp in copies(c, slot, width):
            cp.start()

    def wait(c, slot, width=tf):
        for cp in copies(c, slot, width):
            cp.wait()

    def multiply(slot, width=tf):
        h = h_ref[...]
        g = jnp.dot(h, wgb[slot, :, :width], preferred_element_type=F32)
        u = jnp.dot(h, wub[slot, :, :width], preferred_element_type=F32)
        a = (g * _sigmoid(g) * u).astype(BF16)
        cw = o_ref.shape[1] // 4
        for q in range(4):
            o_ref[:, q * cw:(q + 1) * cw] += jnp.dot(a, wdb[slot, :width, q * cw:(q + 1) * cw],
                                                     preferred_element_type=F32)

    @pl.when(i == 0)
    def _():
        start(0, 0)

    o_ref[...] = jnp.zeros_like(o_ref)

    def step(c, carry):
        slot = c % 2
        wait(c, slot)
        start(c + 1, 1 - slot)
        multiply(slot)
        return carry

    lax.fori_loop(0, n_full - 1, step, 0)
    last, last_slot = n_full - 1, (n_full - 1) % 2
    wait(last, last_slot)
    if tail:
        start(n_full, 1 - last_slot, tail)
    else:
        @pl.when(i + 1 < pl.num_programs(0))
        def _():
            start(0, 1 - last_slot)
    multiply(last_slot)
    if tail:
        wait(n_full, 1 - last_slot, tail)

        @pl.when(i + 1 < pl.num_programs(0))
        def _():
            start(0, 0)
        multiply(1 - last_slot, tail)


def _ffn_loop(h, wg, wu, wd, tm, tf):
    n, d = h.shape
    dff = wg.shape[1]
    tm = min(tm, n)
    n_full = dff // tf
    assert n_full >= 2 and (n_full + (1 if dff % tf else 0)) % 2 == 0 and (dff % tf) % LANES == 0
    return pl.pallas_call(
        functools.partial(_ffn_loop_kernel, dff, tf),
        grid=(n // tm,),
        in_specs=[
            pl.BlockSpec((tm, d), lambda i: (i, 0)),
            pl.BlockSpec(memory_space=pl.ANY),
            pl.BlockSpec(memory_space=pl.ANY),
            pl.BlockSpec(memory_space=pl.ANY),
        ],
        out_specs=pl.BlockSpec((tm, d), lambda i: (i, 0)),
        out_shape=jax.ShapeDtypeStruct((n, d), F32),
        scratch_shapes=[pltpu.VMEM((2, d, tf), BF16), pltpu.VMEM((2, d, tf), BF16), pltpu.VMEM((2, tf, d), BF16),
                        pltpu.SemaphoreType.DMA((2, 3))],
        compiler_params=_params(("arbitrary",), 58),
        name="dense_swiglu",
    )(h, wg, wu, wd)


def _expert_kernel(te_ref, nt_ref, rows_ref, idx_ref, idx_next_ref, src_hbm, wg_hbm, wu_hbm, wd_hbm, y_ref,
                   xs_buf, h_scr, acc, wgb, wub, wdb, sem, wsems):
    t = pl.program_id(0)
    n_live = nt_ref[0]
    live = t < n_live
    tm, half = xs_buf.shape
    tf = wgb.shape[2]
    n_chunks = wg_hbm.shape[2] // tf
    top = tm // 2
    mostly_padding = rows_ref[t] <= top

    def gather(iref):
        def start(r, carry):
            pltpu.make_async_copy(src_hbm.at[pl.ds(iref[0, 0, r], 1)], xs_buf.at[pl.ds(r, 1)], sem).start()
            return carry
        lax.fori_loop(0, tm, start, 0, unroll=8)

    def copies(e, c, slot):
        start = pl.multiple_of(c * tf, LANES)
        return (pltpu.make_async_copy(wg_hbm.at[e, :, pl.ds(start, tf)], wgb.at[slot], wsems.at[slot, 0]),
                pltpu.make_async_copy(wu_hbm.at[e, :, pl.ds(start, tf)], wub.at[slot], wsems.at[slot, 1]),
                pltpu.make_async_copy(wd_hbm.at[e, pl.ds(start, tf), :], wdb.at[slot], wsems.at[slot, 2]))

    def start(e, c, slot):
        for cp in copies(e, c, slot):
            cp.start()

    def wait(e, c, slot):
        for cp in copies(e, c, slot):
            cp.wait()

    def multiply(slot, rows):
        h = h_scr[:rows]
        g = jnp.dot(h, wgb[slot], preferred_element_type=F32)
        u = jnp.dot(h, wub[slot], preferred_element_type=F32)
        a = (g * _sigmoid(g) * u).astype(BF16)
        cw = acc.shape[1] // 4
        for q in range(4):
            acc[:rows, q * cw:(q + 1) * cw] += jnp.dot(a, wdb[slot, :, q * cw:(q + 1) * cw],
                                                       preferred_element_type=F32)

    @pl.when(t == 0)
    def _():
        gather(idx_ref)
        start(te_ref[0], 0, 0)

    @pl.when(live)
    def _():
        e = te_ref[t]
        pltpu.make_async_copy(xs_buf, xs_buf, sem).wait()
        hi, lo = _unpack_pair(xs_buf[...])
        h_scr[:, :half] = hi
        h_scr[:, half:] = lo
        acc[...] = jnp.zeros_like(acc)

        @pl.when(t + 1 < n_live)
        def _():
            gather(idx_next_ref)

        def run(rows):
            def step(c, carry):
                slot = c % 2
                wait(e, c, slot)
                start(e, c + 1, 1 - slot)
                multiply(slot, rows)
                return carry

            lax.fori_loop(0, n_chunks - 1, step, 0)
            last_slot = (n_chunks - 1) % 2
            wait(e, n_chunks - 1, last_slot)

            @pl.when(t + 1 < n_live)
            def _():
                start(te_ref[jnp.minimum(t + 1, pl.num_programs(0) - 1)], 0, 1 - last_slot)
            multiply(last_slot, rows)

        @pl.when(jnp.logical_not(mostly_padding))
        def _():
            run(tm)

        @pl.when(mostly_padding)
        def _():
            run(top)

        y_ref[...] = _pack_pair(acc[:, :half], acc[:, half:])

    @pl.when(jnp.logical_not(live))
    def _():
        y_ref[...] = jnp.zeros_like(y_ref)


def _experts(src, src_tok, tile_expert, n_tiles, tile_rows, wg, wu, wd, tm, tf):
    half = src.shape[1]
    d = 2 * half
    rows = src_tok.shape[0]
    dff = wg.shape[2]
    n_t = rows // tm
    idx3 = src_tok.reshape(n_t, 1, tm)
    assert dff % tf == 0 and (dff // tf) % 2 == 0
    return pl.pallas_call(
        _expert_kernel,
        grid_spec=pltpu.PrefetchScalarGridSpec(
            num_scalar_prefetch=3,
            grid=(n_t,),
            in_specs=[
                pl.BlockSpec((1, 1, tm), lambda t, te, nt, tr: (t, 0, 0), memory_space=pltpu.SMEM),
                pl.BlockSpec((1, 1, tm), lambda t, te, nt, tr: (jnp.minimum(t + 1, n_t - 1), 0, 0),
                             memory_space=pltpu.SMEM),
                pl.BlockSpec(memory_space=pl.ANY),
                pl.BlockSpec(memory_space=pl.ANY),
                pl.BlockSpec(memory_space=pl.ANY),
                pl.BlockSpec(memory_space=pl.ANY),
            ],
            out_specs=pl.BlockSpec((tm, half), lambda t, te, nt, tr: (t, 0)),
            scratch_shapes=[pltpu.VMEM((tm, half), U32), pltpu.VMEM((tm, d), BF16), pltpu.VMEM((tm, d), F32),
                            pltpu.VMEM((2, d, tf), BF16), pltpu.VMEM((2, d, tf), BF16), pltpu.VMEM((2, tf, d), BF16),
                            pltpu.SemaphoreType.DMA(()), pltpu.SemaphoreType.DMA((2, 3))],
        ),
        out_shape=jax.ShapeDtypeStruct((rows, half), U32),
        compiler_params=_params(("arbitrary",), 58),
        name="expert_swiglu",
    )(tile_expert, n_tiles, tile_rows, idx3, idx3, src, wg, wu, wd)


def _moe_plan(top_idx, tm):
    n = top_idx.shape[0]
    p = 2 * n
    flat = top_idx.reshape(p)
    experts = jnp.arange(N_EXPERTS, dtype=I32)
    onehot = (flat[:, None] == experts[None, :]).astype(I32)
    csum = jnp.cumsum(onehot, axis=0)
    rank = jnp.sum(onehot * csum, axis=1) - 1
    counts = csum[-1]
    tiles_per = (counts + (tm - 1)) // tm
    tile_end = jnp.cumsum(tiles_per)
    tile_start = tile_end - tiles_per
    pos = (tile_start * tm)[flat] + rank
    n_tiles = tile_end[-1:]
    t_max = p // tm + N_EXPERTS
    t_ids = jnp.arange(t_max, dtype=I32)
    te = jnp.sum((t_ids[:, None] >= tile_end[None, :]).astype(I32), axis=1)
    last_e = jnp.max(jnp.where(tiles_per > 0, experts, 0))
    te = jnp.minimum(te, last_e)
    tile_rows = jnp.clip(counts[te] - (t_ids - tile_start[te]) * tm, 0, tm)
    src_tok = jnp.zeros((t_max * tm,), I32).at[pos].set(jnp.arange(p, dtype=I32) // 2)
    return pos.astype(I32), src_tok, te.astype(I32), n_tiles.astype(I32), tile_rows.astype(I32)


def kernel(x, c, w_ada, b_ada, norm_pre_mix, norm_post_mix, norm_pre_ffn, norm_post_ffn, w_in, sinks, w_pool,
           pool_scale, w_out, ffn_w_gate, ffn_w_up, ffn_w_down, router_w, router_b, moe_w_gate, moe_w_up, moe_w_down):
    batch, seq, d = x.shape
    depth = w_ada.shape[0]
    n = batch * seq
    d_attn = d // 2
    n_heads = sinks.shape[1]
    d_kv = (n_heads // GQA_GROUP) * HEAD_DIM
    u_col0 = d_attn + 2 * d_kv
    assert n_heads * HEAD_DIM == d_attn and seq % BLOCK == 0

    x2 = x.reshape(n, d)
    mod = _ada(c, w_ada, b_ada)

    def mods(l):
        return [mod[l, :, k * d:(k + 1) * d].reshape(batch, 1, d) for k in range(N_MOD)]

    shift1, scale1, gate1, shift2, scale2, gate2 = mods(0)
    h = _prenorm(x2, norm_pre_mix[0], shift1, scale1, seq)
    flat = lambda w: w.reshape(-1, w.shape[-1])
    ready = {}

    def hosted(host, rides, *a):
        out, casts = host(*a, ride=[(w, lead) for _, w, lead in rides])
        ready.update({key: v for (key, _, _), v in zip(rides, casts)})
        return out

    def take(key, w, lead=None):
        shape = w.shape if lead is None else w.shape[1:]
        return ready.pop(key).reshape(shape) if key in ready else _select(w, lead).astype(BF16)

    for l in range(depth):
        i = l // 2
        dense = l % 2 == 0
        nm = (l + 1) // 2 if (dense and l + 1 < depth) else None
        on_proj_in, on_attn, on_proj_out = [(("w_out", l), w_out, l)], [], []
        if dense:
            on_proj_in += [(("ffn_wg", i), ffn_w_gate, i), (("ffn_wd", i), ffn_w_down, i)]
            on_proj_out.append((("ffn_wu", i), ffn_w_up, i))
            if nm is not None:
                on_attn.append((("moe_wg", nm), flat(moe_w_gate[nm]), None))
            if l + 1 < depth:
                on_proj_out.append((("w_in", l + 1), w_in, l + 1))
        else:
            on_attn.append((("moe_wu", i), flat(moe_w_up[i]), None))
            on_proj_in.append((("moe_wd", i), flat(moe_w_down[i]), None))

        proj = hosted(_mm, on_proj_in, h, take(("w_in", l), w_in, l), BF16, PROJ_ROWS,
                      PROJ_IN_COLS if dense else PROJ_IN_COLS_HOST)
        attn = hosted(_attention, on_attn, proj, sinks[l], batch, seq, d_attn, d_kv)
        pool = _pool(proj, w_pool[l].astype(BF16), pool_scale[l], batch, seq, u_col0)
        mix = hosted(_mm2, on_proj_out, attn, pool, take(("w_out", l), w_out, l), BF16, PROJ_ROWS, PROJ_OUT_COLS)
        nxt = (norm_pre_ffn[l], shift2, scale2)
        if dense:
            x2, h2 = _post("next", mix, x2, gate1, norm_post_mix[l], seq, nxt=nxt)
            f = _ffn_loop(h2, take(("ffn_wg", i), ffn_w_gate, i), take(("ffn_wu", i), ffn_w_up, i),
                          take(("ffn_wd", i), ffn_w_down, i), FFN_ROWS, FFN_CHUNK)
            gather = None
        else:
            rw = jnp.zeros((d, LANES), BF16).at[:, :N_EXPERTS].set(router_w[i].astype(BF16))
            rb = jnp.full((1, LANES), NEG_INF, F32).at[0, :N_EXPERTS].set(router_b[i])
            x2, h2p, route = _post("route", mix, x2, gate1, norm_post_mix[l], seq, nxt=nxt, router=(rw, rb))
            pos, src_tok, tile_expert, n_tiles, tile_rows = _moe_plan(route[:, :2].astype(I32), FFN_ROWS)
            y = _experts(h2p, src_tok, tile_expert, n_tiles, tile_rows, take(("moe_wg", i), moe_w_gate[i]),
                         take(("moe_wu", i), moe_w_up[i]), take(("moe_wd", i), moe_w_down[i]), FFN_ROWS, FFN_CHUNK)
            f, gather = None, (pos, y, route)
        if l + 1 < depth:
            shift1, scale1, gate1n, shift2n, scale2n, gate2n = mods(l + 1)
            x2, h = _post("next", f, x2, gate2, norm_post_ffn[l], seq,
                          nxt=(norm_pre_mix[l + 1], shift1, scale1), gather=gather)
            gate1, shift2, scale2, gate2 = gate1n, shift2n, scale2n, gate2n
        else:
            (x2,) = _post("last", f, x2, gate2, norm_post_ffn[l], seq, gather=gather)
    return x2.reshape(batch, seq, d)
```

```python
import functools

import jax
import jax.numpy as jnp
import numpy as np
from jax import lax
from jax.experimental import pallas as pl
from jax.experimental.pallas import tpu as pltpu

F32 = jnp.float32
BF16 = jnp.bfloat16
U32 = jnp.uint32
I32 = jnp.int32

EPS = 1e-6
NEG_INF = -1e30
BLOCK = 128
HEAD_DIM = 64
GQA_GROUP = 8
POOL_WINDOWS = (2, 4, 8, 16)
N_EXPERTS = 8
N_MOD = 6
LANES = 128
PROJ_ROWS = 1024
PROJ_IN_COLS = 768
PROJ_OUT_COLS = 1024
FFN_ROWS = 512
FFN_CHUNK = 512
BF16_SUBLANES = 16
RIDER_BLOCK_BYTES = 4 << 20

MIB = 1 << 20


def _params(sem, vmem_mib):
    return pltpu.CompilerParams(dimension_semantics=sem, vmem_limit_bytes=vmem_mib * MIB)


def _sigmoid(x):
    return 1.0 / (1.0 + jnp.exp(-x))


def _rms(x, g):
    return x * lax.rsqrt(jnp.mean(x * x, axis=-1, keepdims=True) + EPS) * g


def _pack_pair(a, b):
    ua = pltpu.bitcast(a.astype(BF16).astype(F32), U32)
    ub = pltpu.bitcast(b.astype(BF16).astype(F32), U32)
    return ua | (ub >> 16)


def _unpack_pair_f32(p):
    return pltpu.bitcast(p & jnp.uint32(0xFFFF0000), F32), pltpu.bitcast(p << 16, F32)


def _unpack_pair(p):
    hi, lo = _unpack_pair_f32(p)
    return hi.astype(BF16), lo.astype(BF16)


def _select(w, lead):
    return w if lead is None else w[lead]


def _plan_riders(ride, grid):
    steps = grid[0] * grid[1]
    plans = []
    for w, _ in ride:
        rows, cols = w.shape[-2:]
        plan = None
        for rb in range(BF16_SUBLANES, rows + 1, BF16_SUBLANES):
            if rows % rb == 0 and rows // rb <= steps:
                plan = rb if rb * cols * 4 <= RIDER_BLOCK_BYTES else None
                break
        plans.append(plan)
    return plans


def _ridden(outs, ride, plans):
    extra = iter(outs[1:])
    return outs[0], tuple(next(extra) if rb is not None else _select(w, lead).astype(BF16)
                          for (w, lead), rb in zip(ride, plans))


def _add_riders(kernel_fn, args, in_specs, out_specs, out_shape, grid, ride, plans):
    riders = [(w, lead, rb) for (w, lead), rb in zip(ride, plans) if rb is not None]
    if not riders:
        return kernel_fn, list(args)
    n1 = grid[1]
    n_in, n_out, n_r = len(in_specs), len(out_specs), len(riders)
    for w, lead, rb in riders:
        rows, cols = w.shape[-2:]
        last = rows // rb - 1
        blk = functools.partial(lambda a, b, last: (jnp.minimum(a * n1 + b, last), 0), last=last)
        if lead is None:
            in_specs.append(pl.BlockSpec((rb, cols), blk))
        else:
            in_specs.append(pl.BlockSpec((None, rb, cols), functools.partial(
                lambda a, b, last, lead: (lead, jnp.minimum(a * n1 + b, last), 0), last=last, lead=lead)))
        out_specs.append(pl.BlockSpec((rb, cols), blk))
        out_shape.append(jax.ShapeDtypeStruct((rows, cols), BF16))

    def body(*refs):
        ins, srcs = refs[:n_in], refs[n_in:n_in + n_r]
        refs = refs[n_in + n_r:]
        outs, dsts = refs[:n_out], refs[n_out:n_out + n_r]
        kernel_fn(*ins, *outs, *refs[n_out + n_r:])
        for src, dst in zip(srcs, dsts):
            dst[...] = src[...].astype(BF16)

    return body, list(args) + [w for w, _, _ in riders]


def _ada_kernel(c_ref, w_ref, b_ref, o_ref):
    c = c_ref[...]
    ca = (c * _sigmoid(c)).astype(BF16)
    o_ref[0] = jnp.dot(ca, w_ref[0].astype(BF16), preferred_element_type=F32) + b_ref[0]


def _ada(c, w_ada, b_ada):
    depth, d, nm = w_ada.shape
    b = c.shape[0]
    tn = 512
    return pl.pallas_call(
        _ada_kernel,
        grid=(depth, nm // tn),
        in_specs=[
            pl.BlockSpec((b, d), lambda l, j: (0, 0)),
            pl.BlockSpec((1, d, tn), lambda l, j: (l, 0, j)),
            pl.BlockSpec((1, 1, tn), lambda l, j: (l, 0, j)),
        ],
        out_specs=pl.BlockSpec((1, b, tn), lambda l, j: (l, 0, j)),
        out_shape=jax.ShapeDtypeStruct((depth, b, nm), F32),
        compiler_params=_params(("parallel", "parallel"), 40),
        name="ada_mod",
    )(c, w_ada, b_ada.reshape(depth, 1, nm))


def _prenorm_kernel(x_ref, g_ref, sh_ref, sc_ref, o_ref):
    h = _rms(x_ref[...], g_ref[...]) * (1.0 + sc_ref[0]) + sh_ref[0]
    o_ref[...] = h.astype(BF16)


def _prenorm(x2, g, shift, scale, seq):
    n, d = x2.shape
    tr = min(256, seq)
    per = seq // tr
    return pl.pallas_call(
        _prenorm_kernel,
        grid=(n // tr,),
        in_specs=[
            pl.BlockSpec((tr, d), lambda i: (i, 0)),
            pl.BlockSpec((1, d), lambda i: (0, 0)),
            pl.BlockSpec((1, 1, d), lambda i: (i // per, 0, 0)),
            pl.BlockSpec((1, 1, d), lambda i: (i // per, 0, 0)),
        ],
        out_specs=pl.BlockSpec((tr, d), lambda i: (i, 0)),
        out_shape=jax.ShapeDtypeStruct((n, d), BF16),
        compiler_params=_params(("parallel",), 40),
        name="prenorm",
    )(x2, g.reshape(1, d), shift, scale)


def _mm_kernel(a_ref, w_ref, o_ref):
    o_ref[...] = jnp.dot(a_ref[...], w_ref[...], preferred_element_type=F32).astype(o_ref.dtype)


def _mm(a, w, out_dtype, tm, tn, ride=()):
    m, k = a.shape
    n = w.shape[1]
    tm = min(tm, m)
    grid = (m // tm, n // tn)
    plans = _plan_riders(ride, grid)
    in_specs = [
        pl.BlockSpec((tm, k), lambda i, j: (i, 0)),
        pl.BlockSpec((k, tn), lambda i, j: (0, j)),
    ]
    out_specs = [pl.BlockSpec((tm, tn), lambda i, j: (i, j))]
    out_shape = [jax.ShapeDtypeStruct((m, n), out_dtype)]
    body, args = _add_riders(_mm_kernel, [a, w], in_specs, out_specs, out_shape, grid, ride, plans)
    outs = pl.pallas_call(
        body,
        grid=grid,
        in_specs=in_specs,
        out_specs=out_specs,
        out_shape=out_shape,
        compiler_params=_params(("parallel", "arbitrary"), 56),
        name="proj_in",
    )(*args)
    return _ridden(outs, ride, plans)


def _mm2_kernel(a1_ref, a2_ref, w1_ref, w2_ref, o_ref):
    acc = jnp.dot(a1_ref[...], w1_ref[...], preferred_element_type=F32)
    acc = acc + jnp.dot(a2_ref[...], w2_ref[...], preferred_element_type=F32)
    o_ref[...] = acc.astype(o_ref.dtype)


def _mm2(a1, a2, w, out_dtype, tm, tn, ride=()):
    m, k1 = a1.shape
    n = w.shape[1]
    tm = min(tm, m)
    grid = (m // tm, n // tn)
    plans = _plan_riders(ride, grid)
    in_specs = [
        pl.BlockSpec((tm, k1), lambda i, j: (i, 0)),
        pl.BlockSpec((tm, k1), lambda i, j: (i, 0)),
        pl.BlockSpec((k1, tn), lambda i, j: (0, j)),
        pl.BlockSpec((k1, tn), lambda i, j: (1, j)),
    ]
    out_specs = [pl.BlockSpec((tm, tn), lambda i, j: (i, j))]
    out_shape = [jax.ShapeDtypeStruct((m, n), out_dtype)]
    body, args = _add_riders(_mm2_kernel, [a1, a2, w, w], in_specs, out_specs, out_shape, grid, ride, plans)
    outs = pl.pallas_call(
        body,
        grid=grid,
        in_specs=in_specs,
        out_specs=out_specs,
        out_shape=out_shape,
        compiler_params=_params(("parallel", "arbitrary"), 56),
        name="proj_out",
    )(*args)
    return _ridden(outs, ride, plans)


def _split3_const(x):
    parts = []
    r = np.float32(x)
    for _ in range(3):
        p = np.float32(np.asarray(r, dtype=jnp.bfloat16))
        parts.append(float(p))
        r = np.float32(r - p)
    return parts


def _swap_halves(x):
    half = x.shape[1] // 2
    return jnp.concatenate([x[:, half:], x[:, :half]], axis=1)


def _attn_tables(slopes):
    n_heads = len(slopes)
    qx = np.zeros((n_heads, BLOCK, LANES), np.float32)
    for h, s in enumerate(slopes):
        qx[h, :, HEAD_DIM:HEAD_DIM + 3] = _split3_const(s)
    kx = np.zeros((2 * BLOCK, LANES), np.float32)
    kx[:, HEAD_DIM:HEAD_DIM + 3] = np.arange(2 * BLOCK, dtype=np.float32)[:, None]
    return jnp.asarray(qx, BF16), jnp.asarray(kx, BF16)


def _attn_kernel(slopes, sinks_ref, qx_ref, kx_ref, q_ref, kp_ref, kc_ref, vp_ref, vc_ref, o_ref):
    n = pl.program_id(1)
    n_heads = len(slopes)
    k = jnp.concatenate([kp_ref[...], kc_ref[...]], axis=0)
    v = jnp.concatenate([vp_ref[...], vc_ref[...]], axis=0)
    c_idx = lax.broadcasted_iota(I32, (BLOCK, BLOCK), 0)
    i_idx = lax.broadcasted_iota(I32, (BLOCK, BLOCK), 1)
    from_prev = c_idx > i_idx
    prev_bias = jnp.where(n > 0, 0.0, NEG_INF)
    qpos = (lax.broadcasted_iota(I32, (1, BLOCK), 1) + BLOCK).astype(F32)
    qlane = lax.broadcasted_iota(I32, (BLOCK, LANES), 1) < HEAD_DIM
    klane = lax.broadcasted_iota(I32, (2 * BLOCK, LANES), 1) < HEAD_DIM
    scale = jnp.asarray(HEAD_DIM ** -0.5, BF16)
    outs = []
    for kv in range(n_heads // GQA_GROUP):
        heads = range(kv * GQA_GROUP, (kv + 1) * GQA_GROUP)
        kt = k[:, (kv // 2) * LANES:(kv // 2 + 1) * LANES]
        vt = v[:, (kv // 2) * LANES:(kv // 2 + 1) * LANES]
        if kv % 2:
            kt, vt = _swap_halves(kt), _swap_halves(vt)
        k_aug = jnp.where(klane, kt, kx_ref[...])
        pieces = []
        for h in heads:
            qt = q_ref[:, (h // 2) * LANES:(h // 2 + 1) * LANES]
            if h % 2:
                qt = _swap_halves(qt)
            pieces.append(jnp.where(qlane, qt * scale, qx_ref[h]))
        qg = jnp.concatenate(pieces, axis=0)
        st = lax.dot_general(k_aug, qg, (((1,), (1,)), ((), ())), preferred_element_type=F32)
        pts = []
        for g, h in enumerate(heads):
            s_prev = st[:BLOCK, g * BLOCK:(g + 1) * BLOCK] + prev_bias
            s_cur = st[BLOCK:, g * BLOCK:(g + 1) * BLOCK]
            s = jnp.where(from_prev, s_prev, s_cur)
            sink = sinks_ref[h] + np.float32(slopes[h]) * qpos
            m = jnp.maximum(jnp.max(s, axis=0, keepdims=True), sink)
            e = jnp.exp(s - m)
            denom = jnp.sum(e, axis=0, keepdims=True) + jnp.exp(sink - m)
            p = e * (1.0 / denom)
            pts.append(jnp.concatenate([jnp.where(from_prev, p, 0.0).astype(BF16),
                                        jnp.where(from_prev, 0.0, p).astype(BF16)], axis=0))
        pt = jnp.concatenate(pts, axis=1)
        og = lax.dot_general(pt, vt, (((0,), (0,)), ((), ())), preferred_element_type=F32)
        for g in range(GQA_GROUP):
            outs.append(og[g * BLOCK:(g + 1) * BLOCK, :HEAD_DIM].astype(BF16))
    o_ref[...] = jnp.concatenate(outs, axis=-1)


def _attention(proj, sinks, batch, seq, d_attn, d_kv, ride=()):
    n = proj.shape[0]
    nb = seq // BLOCK
    n_heads = d_attn // HEAD_DIM
    slopes = tuple(2.0 ** (-8.0 * (h + 1) / n_heads) for h in range(n_heads))
    kcol = d_attn // d_kv
    vcol = kcol + 1

    def cur(col):
        return lambda b, i: (b * nb + i, col)

    def prev(col):
        return lambda b, i: (b * nb + jnp.maximum(i - 1, 0), col)

    grid = (batch, nb)
    plans = _plan_riders(ride, grid)
    qx, kx = _attn_tables(slopes)
    in_specs = [
        pl.BlockSpec(memory_space=pltpu.SMEM),
        pl.BlockSpec(qx.shape, lambda b, i: (0, 0, 0)),
        pl.BlockSpec(kx.shape, lambda b, i: (0, 0)),
        pl.BlockSpec((BLOCK, d_attn), cur(0)),
        pl.BlockSpec((BLOCK, d_kv), prev(kcol)),
        pl.BlockSpec((BLOCK, d_kv), cur(kcol)),
        pl.BlockSpec((BLOCK, d_kv), prev(vcol)),
        pl.BlockSpec((BLOCK, d_kv), cur(vcol)),
    ]
    out_specs = [pl.BlockSpec((BLOCK, d_attn), cur(0))]
    out_shape = [jax.ShapeDtypeStruct((n, d_attn), BF16)]
    body, args = _add_riders(functools.partial(_attn_kernel, slopes), [sinks, qx, kx, proj, proj, proj, proj, proj],
                             in_specs, out_specs, out_shape, grid, ride, plans)
    outs = pl.pallas_call(
        body,
        grid=grid,
        in_specs=in_specs,
        out_specs=out_specs,
        out_shape=out_shape,
        compiler_params=_params(("parallel", "arbitrary"), 40),
        name="swa_attention",
    )(*args)
    return _ridden(outs, ride, plans)


def _pool_kernel(u_ref, w_ref, ps_ref, o_ref):
    g = pl.program_id(1)
    u = u_ref[...].astype(F32)
    row = lax.broadcasted_iota(I32, u.shape, 0)

    def shifted(x, k):
        return jnp.where(row >= k, pltpu.roll(x, k, 0), 0.0)

    s2 = u + shifted(u, 1)
    s4 = s2 + shifted(s2, 2)
    s8 = s4 + shifted(s4, 4)
    s16 = s8 + shifted(s8, 8)
    wsum = jnp.where(g == 0, s2, jnp.where(g == 1, s4, jnp.where(g == 2, s8, s16)))
    win = jnp.left_shift(jnp.int32(POOL_WINDOWS[0]), g)
    cnt = jnp.minimum(row + 1, win).astype(F32)
    pooled = wsum / cnt - u
    y = jnp.dot(pooled.astype(BF16), w_ref[0], preferred_element_type=F32)
    o_ref[...] = (y * ps_ref[...]).astype(BF16)


def _pool(proj, w_pool, pool_scale, batch, seq, u_col0):
    n = proj.shape[0]
    ng, c, _ = w_pool.shape
    assert POOL_WINDOWS == tuple(POOL_WINDOWS[0] << g for g in range(ng))
    col0 = u_col0 // c
    return pl.pallas_call(
        _pool_kernel,
        grid=(batch, ng),
        in_specs=[
            pl.BlockSpec((seq, c), lambda b, g: (b, col0 + g)),
            pl.BlockSpec((1, c, c), lambda b, g: (g, 0, 0)),
            pl.BlockSpec((1, c), lambda b, g: (0, g)),
        ],
        out_specs=pl.BlockSpec((seq, c), lambda b, g: (b, g)),
        out_shape=jax.ShapeDtypeStruct((n, ng * c), BF16),
        compiler_params=_params(("parallel", "arbitrary"), 48),
        name="pool_mixer",
    )(proj, w_pool, pool_scale.reshape(1, ng * c))


def _top2_route(logits):
    lane = lax.broadcasted_iota(I32, logits.shape, 1).astype(F32)
    m1 = jnp.max(logits, axis=-1, keepdims=True)
    i1 = jnp.min(jnp.where(logits == m1, lane, float(LANES)), axis=-1, keepdims=True)
    l2 = jnp.where(lane == i1, -jnp.inf, logits)
    m2 = jnp.max(l2, axis=-1, keepdims=True)
    i2 = jnp.min(jnp.where(l2 == m2, lane, float(LANES)), axis=-1, keepdims=True)
    e = jnp.exp(m2 - m1)
    w1 = 1.0 / (1.0 + e)
    w2 = e / (1.0 + e)
    return jnp.where(lane == 0.0, i1, jnp.where(lane == 1.0, i2, jnp.where(lane == 2.0, w1, jnp.where(lane == 3.0, w2, 0.0))))


def _post_kernel(mode, gathered, *refs):
    refs = list(refs)
    if gathered:
        pos_ref, pos_next_ref, y_hbm, route_in = refs[:4]
        refs = refs[4:]
    else:
        f_ref = refs.pop(0)
    x_ref, gate_ref, gpost_ref = refs[:3]
    refs = refs[3:]
    if mode != "last":
        gpre_ref, sh_ref, sc_ref = refs[:3]
        refs = refs[3:]
    if mode == "route":
        rw_ref, rb_ref = refs[:2]
        refs = refs[2:]
    xo_ref = refs.pop(0)
    if mode != "last":
        ho_ref = refs.pop(0)
    if mode == "route":
        ro_ref = refs.pop(0)

    if gathered:
        ybuf, sems = refs
        tr = x_ref.shape[0]
        i = pl.program_id(0)
        slot = i % 2

        def gather(idx_ref, s):
            def start(r, carry):
                for kk in range(2):
                    pltpu.make_async_copy(y_hbm.at[pl.ds(idx_ref[0, 0, 2 * r + kk], 1)],
                                          ybuf.at[s, kk, pl.ds(r, 1)], sems.at[s]).start()
                return carry
            lax.fori_loop(0, tr, start, 0, unroll=8)

        @pl.when(i == 0)
        def _():
            gather(pos_ref, 0)

        @pl.when(i + 1 < pl.num_programs(0))
        def _():
            gather(pos_next_ref, 1 - slot)

        pltpu.make_async_copy(ybuf.at[slot], ybuf.at[slot], sems.at[slot]).wait()
        rt = route_in[...]
        a_hi, a_lo = _unpack_pair_f32(ybuf[slot, 0])
        b_hi, b_lo = _unpack_pair_f32(ybuf[slot, 1])
        w_a, w_b = rt[:, 2:3], rt[:, 3:4]
        f = jnp.concatenate([a_hi * w_a + b_hi * w_b, a_lo * w_a + b_lo * w_b], axis=1)
    else:
        f = f_ref[...].astype(F32)

    xn = x_ref[...] + gate_ref[0] * _rms(f, gpost_ref[...])
    xo_ref[...] = xn
    if mode == "last":
        return
    h = _rms(xn, gpre_ref[...]) * (1.0 + sc_ref[0]) + sh_ref[0]
    if mode == "next":
        ho_ref[...] = h.astype(BF16)
        return
    half = h.shape[1] // 2
    ho_ref[...] = _pack_pair(h[:, :half], h[:, half:])
    logits = jnp.dot(h.astype(BF16), rw_ref[...], preferred_element_type=F32) + rb_ref[...]
    ro_ref[...] = _top2_route(logits)


def _post(mode, f, x2, gate, g_post, seq, nxt=None, router=None, gather=None):
    n, d = x2.shape
    tr = min(256, seq)
    per = seq // tr
    row = lambda i: (i, 0)
    fix = lambda i: (0, 0)
    bat = lambda i: (i // per, 0, 0)
    in_specs, args, scratch = [], [], []
    if gather is not None:
        pos, y, route = gather
        last = n // tr - 1
        pos3 = pos.reshape(n // tr, 1, 2 * tr)
        in_specs += [
            pl.BlockSpec((1, 1, 2 * tr), lambda i: (i, 0, 0), memory_space=pltpu.SMEM),
            pl.BlockSpec((1, 1, 2 * tr), lambda i: (jnp.minimum(i + 1, last), 0, 0), memory_space=pltpu.SMEM),
            pl.BlockSpec(memory_space=pl.ANY),
            pl.BlockSpec((tr, LANES), row),
        ]
        args += [pos3, pos3, y, route]
        scratch = [pltpu.VMEM((2, 2, tr, d // 2), U32), pltpu.SemaphoreType.DMA((2,))]
    else:
        in_specs.append(pl.BlockSpec((tr, d), row))
        args.append(f)
    in_specs += [pl.BlockSpec((tr, d), row), pl.BlockSpec((1, 1, d), bat), pl.BlockSpec((1, d), fix)]
    args += [x2, gate, g_post.reshape(1, d)]
    out_specs = [pl.BlockSpec((tr, d), row)]
    out_shape = [jax.ShapeDtypeStruct((n, d), F32)]
    if mode != "last":
        g_pre, shift, scale = nxt
        in_specs += [pl.BlockSpec((1, d), fix), pl.BlockSpec((1, 1, d), bat), pl.BlockSpec((1, 1, d), bat)]
        args += [g_pre.reshape(1, d), shift, scale]
    if mode == "next":
        out_specs.append(pl.BlockSpec((tr, d), row))
        out_shape.append(jax.ShapeDtypeStruct((n, d), BF16))
    if mode == "route":
        rw, rb = router
        in_specs += [pl.BlockSpec((d, LANES), fix), pl.BlockSpec((1, LANES), fix)]
        args += [rw, rb]
        out_specs += [pl.BlockSpec((tr, d // 2), row), pl.BlockSpec((tr, LANES), row)]
        out_shape += [jax.ShapeDtypeStruct((n, d // 2), U32), jax.ShapeDtypeStruct((n, LANES), F32)]
    return pl.pallas_call(
        functools.partial(_post_kernel, mode, gather is not None),
        grid=(n // tr,),
        in_specs=in_specs,
        out_specs=out_specs,
        out_shape=out_shape,
        scratch_shapes=scratch,
        compiler_params=_params(("arbitrary",), 56),
        name="post_" + mode + ("_gather" if gather is not None else ""),
    )(*args)


def _swiglu_step(h_ref, wg_ref, wu_ref, wd_ref, o_ref, j, n_chunk, skip=None, init=True):
    if init:
        @pl.when(j == 0)
        def _():
            o_ref[...] = jnp.zeros_like(o_ref)

    h = h_ref[...]
    g = jnp.dot(h, wg_ref[...], preferred_element_type=F32)
    u = jnp.dot(h, wu_ref[...], preferred_element_type=F32)
    a = g * _sigmoid(g) * u
    if skip is not None:
        a = jnp.where(lax.broadcasted_iota(I32, a.shape, 1) >= skip, a, 0.0)
    a = a.astype(BF16)
    d = o_ref.shape[1]
    cw = d // n_chunk
    for c in range(n_chunk):
        o_ref[:, c * cw:(c + 1) * cw] += jnp.dot(a, wd_ref[:, c * cw:(c + 1) * cw], preferred_element_type=F32)


def _ffn_kernel(dff, h_ref, wg_ref, wu_ref, wd_ref, o_ref):
    j = pl.program_id(1)
    tf = wg_ref.shape[1]
    skip = j * tf - jnp.minimum(j * tf, dff - tf)
    _swiglu_step(h_ref, wg_ref, wu_ref, wd_ref, o_ref, j, 4, skip=skip)


def _ffn(h, wg, wu, wd, tm, tf, ride=()):
    n, d = h.shape
    dff = wg.shape[1]
    tm = min(tm, n)
    grid = (n // tm, pl.cdiv(dff, tf))
    plans = _plan_riders(ride, grid)
    assert (dff - tf) % LANES == 0
    start = lambda j: pl.multiple_of(jnp.minimum(j * tf, dff - tf), LANES)
    in_specs = [
        pl.BlockSpec((tm, d), lambda i, j: (i, 0)),
        pl.BlockSpec((pl.Element(d), pl.Element(tf)), lambda i, j: (0, start(j))),
        pl.BlockSpec((pl.Element(d), pl.Element(tf)), lambda i, j: (0, start(j))),
        pl.BlockSpec((pl.Element(tf), pl.Element(d)), lambda i, j: (start(j), 0)),
    ]
    out_specs = [pl.BlockSpec((tm, d), lambda i, j: (i, 0))]
    out_shape = [jax.ShapeDtypeStruct((n, d), F32)]
    body, args = _add_riders(functools.partial(_ffn_kernel, dff), [h, wg, wu, wd], in_specs, out_specs, out_shape,
                             grid, ride, plans)
    outs = pl.pallas_call(
        body,
        grid=grid,
        in_specs=in_specs,
        out_specs=out_specs,
        out_shape=out_shape,
        compiler_params=_params(("parallel", "arbitrary"), 58),
        name="dense_swiglu",
    )(*args)
    return _ridden(outs, ride, plans)


def _ffn_loop_kernel(dff, tf, n_ride, rb, h_ref, wg_hbm, wu_hbm, wd_hbm, *rest):
    src = rest[:n_ride]
    o_hbm = rest[n_ride]
    dst = rest[n_ride + 1:2 * n_ride + 1]
    acc, ostage, wgb, wub, wdb, sems, osem = rest[2 * n_ride + 1:2 * n_ride + 8]
    i = pl.program_id(0)
    n_full = dff // tf
    tail = dff - n_full * tf
    if n_ride:
        rin, rout, rsems = rest[2 * n_ride + 8:]
        per = src[0].shape[0] // rb
        n_blk = n_ride * per

        def rows_of(g, k):
            return pl.ds(pl.multiple_of((g - k * per) * rb, rb), rb)

        def ride_in(g, k):
            return pltpu.make_async_copy(src[k].at[rows_of(g, k)], rin.at[g % 2], rsems.at[0, g % 2])

        def ride_out(g, k):
            return pltpu.make_async_copy(rout.at[g % 2], dst[k].at[rows_of(g, k)], rsems.at[1, g % 2])

        def for_block(g, fn):
            for k in range(n_ride):
                @pl.when((g >= k * per) & (g < (k + 1) * per))
                def _():
                    fn(g, k)

    def copies(c, slot, width):
        start = pl.multiple_of(c * tf, LANES)
        return (pltpu.make_async_copy(wg_hbm.at[:, pl.ds(start, width)], wgb.at[slot, :, pl.ds(0, width)],
                                      sems.at[slot, 0]),
                pltpu.make_async_copy(wu_hbm.at[:, pl.ds(start, width)], wub.at[slot, :, pl.ds(0, width)],
                                      sems.at[slot, 1]),
                pltpu.make_async_copy(wd_hbm.at[pl.ds(start, width), :], wdb.at[slot, pl.ds(0, width), :],
                                      sems.at[slot, 2]))

    def start(c, slot, width=tf):
        for cp in copies(c, slot, width):
            cp.start()

    def wait(c, slot, width=tf):
        for cp in copies(c, slot, width):
            cp.wait()

    def multiply(slot, width=tf):
        h = h_ref[...]
        g = jnp.dot(h, wgb[slot, :, :width], preferred_element_type=F32)
        u = jnp.dot(h, wub[slot, :, :width], preferred_element_type=F32)
        a = (g * _sigmoid(g) * u).astype(BF16)
        cw = acc.shape[1] // 4
        for q in range(4):
            acc[:, q * cw:(q + 1) * cw] += jnp.dot(a, wdb[slot, :width, q * cw:(q + 1) * cw],
                                                   preferred_element_type=F32)

    @pl.when(i == 0)
    def _():
        start(0, 0)
        if n_ride:
            ride_in(0, 0).start()

    acc[...] = jnp.zeros_like(acc)

    def step(c, carry):
        slot = c % 2
        wait(c, slot)
        if n_ride:
            g = i * (n_full - 1) + c
            for_block(g, lambda g, k: ride_in(g, k).wait())
            for_block(g + 1, lambda g, k: ride_in(g, k).start())
            for_block(g - 2, lambda g, k: ride_out(g, k).wait())
        start(c + 1, 1 - slot)
        multiply(slot)
        if n_ride:
            rout[g % 2] = rin[g % 2].astype(BF16)
            for_block(g, lambda g, k: ride_out(g, k).start())
        return carry

    lax.fori_loop(0, n_full - 1, step, 0)
    last, last_slot = n_full - 1, (n_full - 1) % 2
    wait(last, last_slot)
    if tail:
        start(n_full, 1 - last_slot, tail)
    else:
        @pl.when(i + 1 < pl.num_programs(0))
        def _():
            start(0, 1 - last_slot)
    multiply(last_slot)
    if tail:
        wait(n_full, 1 - last_slot, tail)

        @pl.when(i + 1 < pl.num_programs(0))
        def _():
            start(0, 0)
        multiply(1 - last_slot, tail)

    tm = acc.shape[0]

    def store(tile):
        return pltpu.make_async_copy(ostage, o_hbm.at[pl.ds(pl.multiple_of(tile * tm, tm), tm)], osem)

    @pl.when(i > 0)
    def _():
        store(i - 1).wait()

    ostage[...] = acc[...].astype(ostage.dtype)
    store(i).start()

    @pl.when(i + 1 == pl.num_programs(0))
    def _():
        store(i).wait()


def _ffn_loop(h, wg, wu, wd, tm, tf, ride=()):
    n, d = h.shape
    dff = wg.shape[1]
    tm = min(tm, n)
    n_tiles = n // tm
    n_full = dff // tf
    assert n_full >= 2 and (n_full + (1 if dff % tf else 0)) % 2 == 0 and (dff % tf) % LANES == 0
    ride_w = [w for w, _ in ride]
    rb = None
    if ride_w:
        rows, cols = ride_w[0].shape
        assert all(w.shape == (rows, cols) for w in ride_w)
        hosts = n_tiles * (n_full - 1) - 2
        for cand in range(BF16_SUBLANES, rows + 1, BF16_SUBLANES):
            if rows % cand == 0 and len(ride_w) * (rows // cand) <= hosts:
                rb = cand if cand * cols * 12 <= 2 * RIDER_BLOCK_BYTES else None
                break
    if rb is None:
        ride_w = []
    n_ride = len(ride_w)
    any_spec = pl.BlockSpec(memory_space=pl.ANY)
    scratch = [pltpu.VMEM((tm, d), F32), pltpu.VMEM((tm, d), BF16), pltpu.VMEM((2, d, tf), BF16),
               pltpu.VMEM((2, d, tf), BF16), pltpu.VMEM((2, tf, d), BF16), pltpu.SemaphoreType.DMA((2, 3)),
               pltpu.SemaphoreType.DMA(())]
    out_specs = [any_spec]
    out_shape = [jax.ShapeDtypeStruct((n, d), BF16)]
    if n_ride:
        scratch += [pltpu.VMEM((2, rb, cols), F32), pltpu.VMEM((2, rb, cols), BF16), pltpu.SemaphoreType.DMA((2, 2))]
        out_specs += [any_spec] * n_ride
        out_shape += [jax.ShapeDtypeStruct((rows, cols), BF16)] * n_ride
    outs = pl.pallas_call(
        functools.partial(_ffn_loop_kernel, dff, tf, n_ride, rb),
        grid=(n_tiles,),
        in_specs=[pl.BlockSpec((tm, d), lambda i: (i, 0))] + [any_spec] * (3 + n_ride),
        out_specs=out_specs,
        out_shape=out_shape,
        scratch_shapes=scratch,
        compiler_params=_params(("arbitrary",), 58),
        name="dense_swiglu",
    )(h, wg, wu, wd, *ride_w)
    casts = tuple(outs[1:]) if n_ride else tuple(_select(w, lead).astype(BF16) for w, lead in ride)
    return outs[0], casts


def _expert_kernel(te_ref, nt_ref, rows_ref, idx_ref, idx_next_ref, src_hbm, wg_hbm, wu_hbm, wd_hbm, y_ref,
                   xs_buf, h_scr, acc, wgb, wub, wdb, sem, wsems):
    t = pl.program_id(0)
    n_live = nt_ref[0]
    live = t < n_live
    tm, half = xs_buf.shape
    tf = wgb.shape[2]
    n_chunks = wg_hbm.shape[2] // tf
    top = tm // 2
    mostly_padding = rows_ref[t] <= top

    def gather(iref):
        def start(r, carry):
            pltpu.make_async_copy(src_hbm.at[pl.ds(iref[0, 0, r], 1)], xs_buf.at[pl.ds(r, 1)], sem).start()
            return carry
        lax.fori_loop(0, tm, start, 0, unroll=8)

    def copies(e, c, slot):
        start = pl.multiple_of(c * tf, LANES)
        return (pltpu.make_async_copy(wg_hbm.at[e, :, pl.ds(start, tf)], wgb.at[slot], wsems.at[slot, 0]),
                pltpu.make_async_copy(wu_hbm.at[e, :, pl.ds(start, tf)], wub.at[slot], wsems.at[slot, 1]),
                pltpu.make_async_copy(wd_hbm.at[e, pl.ds(start, tf), :], wdb.at[slot], wsems.at[slot, 2]))

    def start(e, c, slot):
        for cp in copies(e, c, slot):
            cp.start()

    def wait(e, c, slot):
        for cp in copies(e, c, slot):
            cp.wait()

    def multiply(slot, rows):
        h = h_scr[:rows]
        g = jnp.dot(h, wgb[slot], preferred_element_type=F32)
        u = jnp.dot(h, wub[slot], preferred_element_type=F32)
        a = (g * _sigmoid(g) * u).astype(BF16)
        cw = acc.shape[1] // 4
        for q in range(4):
            acc[:rows, q * cw:(q + 1) * cw] += jnp.dot(a, wdb[slot, :, q * cw:(q + 1) * cw],
                                                       preferred_element_type=F32)

    @pl.when(t == 0)
    def _():
        gather(idx_ref)
        start(te_ref[0], 0, 0)

    @pl.when(live)
    def _():
        e = te_ref[t]
        pltpu.make_async_copy(xs_buf, xs_buf, sem).wait()
        hi, lo = _unpack_pair(xs_buf[...])
        h_scr[:, :half] = hi
        h_scr[:, half:] = lo
        acc[...] = jnp.zeros_like(acc)

        @pl.when(t + 1 < n_live)
        def _():
            gather(idx_next_ref)

        def run(rows):
            def step(c, carry):
                slot = c % 2
                wait(e, c, slot)
                start(e, c + 1, 1 - slot)
                multiply(slot, rows)
                return carry

            lax.fori_loop(0, n_chunks - 1, step, 0)
            last_slot = (n_chunks - 1) % 2
            wait(e, n_chunks - 1, last_slot)

            @pl.when(t + 1 < n_live)
            def _():
                start(te_ref[jnp.minimum(t + 1, pl.num_programs(0) - 1)], 0, 1 - last_slot)
            multiply(last_slot, rows)

        @pl.when(jnp.logical_not(mostly_padding))
        def _():
            run(tm)

        @pl.when(mostly_padding)
        def _():
            run(top)

        y_ref[...] = _pack_pair(acc[:, :half], acc[:, half:])

    @pl.when(jnp.logical_not(live))
    def _():
        y_ref[...] = jnp.zeros_like(y_ref)


def _experts(src, src_tok, tile_expert, n_tiles, tile_rows, wg, wu, wd, tm, tf):
    half = src.shape[1]
    d = 2 * half
    rows = src_tok.shape[0]
    dff = wg.shape[2]
    n_t = rows // tm
    idx3 = src_tok.reshape(n_t, 1, tm)
    assert dff % tf == 0 and (dff // tf) % 2 == 0
    return pl.pallas_call(
        _expert_kernel,
        grid_spec=pltpu.PrefetchScalarGridSpec(
            num_scalar_prefetch=3,
            grid=(n_t,),
            in_specs=[
                pl.BlockSpec((1, 1, tm), lambda t, te, nt, tr: (t, 0, 0), memory_space=pltpu.SMEM),
                pl.BlockSpec((1, 1, tm), lambda t, te, nt, tr: (jnp.minimum(t + 1, n_t - 1), 0, 0),
                             memory_space=pltpu.SMEM),
                pl.BlockSpec(memory_space=pl.ANY),
                pl.BlockSpec(memory_space=pl.ANY),
                pl.BlockSpec(memory_space=pl.ANY),
                pl.BlockSpec(memory_space=pl.ANY),
            ],
            out_specs=pl.BlockSpec((tm, half), lambda t, te, nt, tr: (t, 0)),
            scratch_shapes=[pltpu.VMEM((tm, half), U32), pltpu.VMEM((tm, d), BF16), pltpu.VMEM((tm, d), F32),
                            pltpu.VMEM((2, d, tf), BF16), pltpu.VMEM((2, d, tf), BF16), pltpu.VMEM((2, tf, d), BF16),
                            pltpu.SemaphoreType.DMA(()), pltpu.SemaphoreType.DMA((2, 3))],
        ),
        out_shape=jax.ShapeDtypeStruct((rows, half), U32),
        compiler_params=_params(("arbitrary",), 58),
        name="expert_swiglu",
    )(tile_expert, n_tiles, tile_rows, idx3, idx3, src, wg, wu, wd)


def _moe_plan(top_idx, tm):
    n = top_idx.shape[0]
    p = 2 * n
    flat = top_idx.reshape(p)
    experts = jnp.arange(N_EXPERTS, dtype=I32)
    onehot = (flat[:, None] == experts[None, :]).astype(I32)
    csum = jnp.cumsum(onehot, axis=0)
    rank = jnp.sum(onehot * csum, axis=1) - 1
    counts = csum[-1]
    tiles_per = (counts + (tm - 1)) // tm
    tile_end = jnp.cumsum(tiles_per)
    tile_start = tile_end - tiles_per
    pos = (tile_start * tm)[flat] + rank
    n_tiles = tile_end[-1:]
    t_max = p // tm + N_EXPERTS
    t_ids = jnp.arange(t_max, dtype=I32)
    te = jnp.sum((t_ids[:, None] >= tile_end[None, :]).astype(I32), axis=1)
    last_e = jnp.max(jnp.where(tiles_per > 0, experts, 0))
    te = jnp.minimum(te, last_e)
    tile_rows = jnp.clip(counts[te] - (t_ids - tile_start[te]) * tm, 0, tm)
    src_tok = jnp.zeros((t_max * tm,), I32).at[pos].set(jnp.arange(p, dtype=I32) // 2)
    return pos.astype(I32), src_tok, te.astype(I32), n_tiles.astype(I32), tile_rows.astype(I32)


def kernel(x, c, w_ada, b_ada, norm_pre_mix, norm_post_mix, norm_pre_ffn, norm_post_ffn, w_in, sinks, w_pool,
           pool_scale, w_out, ffn_w_gate, ffn_w_up, ffn_w_down, router_w, router_b, moe_w_gate, moe_w_up, moe_w_down):
    batch, seq, d = x.shape
    depth = w_ada.shape[0]
    n = batch * seq
    d_attn = d // 2
    n_heads = sinks.shape[1]
    d_kv = (n_heads // GQA_GROUP) * HEAD_DIM
    u_col0 = d_attn + 2 * d_kv
    assert n_heads * HEAD_DIM == d_attn and seq % BLOCK == 0

    x2 = x.reshape(n, d)
    mod = _ada(c, w_ada, b_ada)

    def mods(l):
        return [mod[l, :, k * d:(k + 1) * d].reshape(batch, 1, d) for k in range(N_MOD)]

    shift1, scale1, gate1, shift2, scale2, gate2 = mods(0)
    h = _prenorm(x2, norm_pre_mix[0], shift1, scale1, seq)
    flat = lambda w: w.reshape(-1, w.shape[-1])
    ready = {}

    def hosted(host, rides, *a):
        out, casts = host(*a, ride=[(w, lead) for _, w, lead in rides])
        ready.update({key: v for (key, _, _), v in zip(rides, casts)})
        return out

    def take(key, w, lead=None):
        shape = w.shape if lead is None else w.shape[1:]
        return ready.pop(key).reshape(shape) if key in ready else _select(w, lead).astype(BF16)

    for l in range(depth):
        i = l // 2
        dense = l % 2 == 0
        nm = (l + 1) // 2 if (dense and l + 1 < depth) else None
        on_proj_in, on_attn, on_proj_out, on_ffn = [(("w_out", l), w_out, l)], [], [], []
        if dense:
            on_proj_in += [(("ffn_wg", i), ffn_w_gate, i), (("ffn_wd", i), ffn_w_down, i)]
            on_proj_out.append((("ffn_wu", i), ffn_w_up, i))
            if nm is not None:
                on_attn.append((("moe_wg", nm), flat(moe_w_gate[nm]), None))
                on_ffn.append((("moe_wd", nm), flat(moe_w_down[nm]), None))
            if l + 1 < depth:
                on_proj_out.append((("w_in", l + 1), w_in, l + 1))
        else:
            on_attn.append((("moe_wu", i), flat(moe_w_up[i]), None))

        proj = hosted(_mm, on_proj_in, h, take(("w_in", l), w_in, l), BF16, PROJ_ROWS, PROJ_IN_COLS)
        attn = hosted(_attention, on_attn, proj, sinks[l], batch, seq, d_attn, d_kv)
        pool = _pool(proj, w_pool[l].astype(BF16), pool_scale[l], batch, seq, u_col0)
        mix = hosted(_mm2, on_proj_out, attn, pool, take(("w_out", l), w_out, l), BF16, PROJ_ROWS, PROJ_OUT_COLS)
        nxt = (norm_pre_ffn[l], shift2, scale2)
        if dense:
            x2, h2 = _post("next", mix, x2, gate1, norm_post_mix[l], seq, nxt=nxt)
            f = hosted(_ffn_loop, on_ffn, h2, take(("ffn_wg", i), ffn_w_gate, i), take(("ffn_wu", i), ffn_w_up, i),
                       take(("ffn_wd", i), ffn_w_down, i), FFN_ROWS, FFN_CHUNK)
            gather = None
        else:
            rw = jnp.zeros((d, LANES), BF16).at[:, :N_EXPERTS].set(router_w[i].astype(BF16))
            rb = jnp.full((1, LANES), NEG_INF, F32).at[0, :N_EXPERTS].set(router_b[i])
            x2, h2p, route = _post("route", mix, x2, gate1, norm_post_mix[l], seq, nxt=nxt, router=(rw, rb))
            pos, src_tok, tile_expert, n_tiles, tile_rows = _moe_plan(route[:, :2].astype(I32), FFN_ROWS)
            y = _experts(h2p, src_tok, tile_expert, n_tiles, tile_rows, take(("moe_wg", i), moe_w_gate[i]),
                         take(("moe_wu", i), moe_w_up[i]), take(("moe_wd", i), moe_w_down[i]), FFN_ROWS, FFN_CHUNK)
            f, gather = None, (pos, y, route)
        if l + 1 < depth:
            shift1, scale1, gate1n, shift2n, scale2n, gate2n = mods(l + 1)
            x2, h = _post("next", f, x2, gate2, norm_post_ffn[l], seq,
                          nxt=(norm_pre_mix[l + 1], shift1, scale1), gather=gather)
            gate1, shift2, scale2, gate2 = gate1n, shift2n, scale2n, gate2n
        else:
            (x2,) = _post("last", f, x2, gate2, norm_post_ffn[l], seq, gather=gather)
    return x2.reshape(batch, seq, d)
```

```python
import functools

import jax
import jax.numpy as jnp
import numpy as np
from jax import lax
from jax.experimental import pallas as pl
from jax.experimental.pallas import tpu as pltpu

F32 = jnp.float32
BF16 = jnp.bfloat16
U32 = jnp.uint32
I32 = jnp.int32

EPS = 1e-6
NEG_INF = -1e30
BLOCK = 128
HEAD_DIM = 64
GQA_GROUP = 8
POOL_WINDOWS = (2, 4, 8, 16)
N_EXPERTS = 8
N_MOD = 6
LANES = 128
PROJ_ROWS = 1024
PROJ_IN_COLS = 768
PROJ_OUT_COLS = 1024
FFN_ROWS = 512
FFN_CHUNK = 512
BF16_SUBLANES = 16
RIDER_BLOCK_BYTES = 4 << 20

MIB = 1 << 20


def _params(sem, vmem_mib):
    return pltpu.CompilerParams(dimension_semantics=sem, vmem_limit_bytes=vmem_mib * MIB)


def _sigmoid(x):
    return 1.0 / (1.0 + jnp.exp(-x))


def _rms(x, g):
    return x * lax.rsqrt(jnp.mean(x * x, axis=-1, keepdims=True) + EPS) * g


def _pack_pair(a, b):
    ua = pltpu.bitcast(a.astype(BF16).astype(F32), U32)
    ub = pltpu.bitcast(b.astype(BF16).astype(F32), U32)
    return ua | (ub >> 16)


def _unpack_pair_f32(p):
    return pltpu.bitcast(p & jnp.uint32(0xFFFF0000), F32), pltpu.bitcast(p << 16, F32)


def _unpack_pair(p):
    hi, lo = _unpack_pair_f32(p)
    return hi.astype(BF16), lo.astype(BF16)


def _select(w, lead):
    return w if lead is None else w[lead]


def _plan_riders(ride, grid):
    steps = grid[0] * grid[1]
    plans = []
    for w, _ in ride:
        rows, cols = w.shape[-2:]
        plan = None
        for rb in range(BF16_SUBLANES, rows + 1, BF16_SUBLANES):
            if rows % rb == 0 and rows // rb <= steps:
                plan = rb if rb * cols * 4 <= RIDER_BLOCK_BYTES else None
                break
        plans.append(plan)
    return plans


def _ridden(outs, ride, plans):
    extra = iter(outs[1:])
    return outs[0], tuple(next(extra) if rb is not None else _select(w, lead).astype(BF16)
                          for (w, lead), rb in zip(ride, plans))


def _add_riders(kernel_fn, args, in_specs, out_specs, out_shape, grid, ride, plans):
    riders = [(w, lead, rb) for (w, lead), rb in zip(ride, plans) if rb is not None]
    if not riders:
        return kernel_fn, list(args)
    n1 = grid[1]
    n_in, n_out, n_r = len(in_specs), len(out_specs), len(riders)
    for w, lead, rb in riders:
        rows, cols = w.shape[-2:]
        last = rows // rb - 1
        blk = functools.partial(lambda a, b, last: (jnp.minimum(a * n1 + b, last), 0), last=last)
        if lead is None:
            in_specs.append(pl.BlockSpec((rb, cols), blk))
        else:
            in_specs.append(pl.BlockSpec((None, rb, cols), functools.partial(
                lambda a, b, last, lead: (lead, jnp.minimum(a * n1 + b, last), 0), last=last, lead=lead)))
        out_specs.append(pl.BlockSpec((rb, cols), blk))
        out_shape.append(jax.ShapeDtypeStruct((rows, cols), BF16))

    def body(*refs):
        ins, srcs = refs[:n_in], refs[n_in:n_in + n_r]
        refs = refs[n_in + n_r:]
        outs, dsts = refs[:n_out], refs[n_out:n_out + n_r]
        kernel_fn(*ins, *outs, *refs[n_out + n_r:])
        for src, dst in zip(srcs, dsts):
            dst[...] = src[...].astype(BF16)

    return body, list(args) + [w for w, _, _ in riders]


def _ada_kernel(c_ref, w_ref, b_ref, o_ref):
    c = c_ref[...]
    ca = (c * _sigmoid(c)).astype(BF16)
    o_ref[0] = jnp.dot(ca, w_ref[0].astype(BF16), preferred_element_type=F32) + b_ref[0]


def _ada(c, w_ada, b_ada):
    depth, d, nm = w_ada.shape
    b = c.shape[0]
    tn = 512
    return pl.pallas_call(
        _ada_kernel,
        grid=(depth, nm // tn),
        in_specs=[
            pl.BlockSpec((b, d), lambda l, j: (0, 0)),
            pl.BlockSpec((1, d, tn), lambda l, j: (l, 0, j)),
            pl.BlockSpec((1, 1, tn), lambda l, j: (l, 0, j)),
        ],
        out_specs=pl.BlockSpec((1, b, tn), lambda l, j: (l, 0, j)),
        out_shape=jax.ShapeDtypeStruct((depth, b, nm), F32),
        compiler_params=_params(("parallel", "parallel"), 40),
        name="ada_mod",
    )(c, w_ada, b_ada.reshape(depth, 1, nm))


def _prenorm_kernel(x_ref, g_ref, sh_ref, sc_ref, o_ref):
    h = _rms(x_ref[...], g_ref[...]) * (1.0 + sc_ref[0]) + sh_ref[0]
    o_ref[...] = h.astype(BF16)


def _prenorm(x2, g, shift, scale, seq):
    n, d = x2.shape
    tr = min(512, seq)
    per = seq // tr
    return pl.pallas_call(
        _prenorm_kernel,
        grid=(n // tr,),
        in_specs=[
            pl.BlockSpec((tr, d), lambda i: (i, 0)),
            pl.BlockSpec((1, d), lambda i: (0, 0)),
            pl.BlockSpec((1, 1, d), lambda i: (i // per, 0, 0)),
            pl.BlockSpec((1, 1, d), lambda i: (i // per, 0, 0)),
        ],
        out_specs=pl.BlockSpec((tr, d), lambda i: (i, 0)),
        out_shape=jax.ShapeDtypeStruct((n, d), BF16),
        compiler_params=_params(("parallel",), 40),
        name="prenorm",
    )(x2, g.reshape(1, d), shift, scale)


def _mm_kernel(a_ref, w_ref, o_ref):
    o_ref[...] = jnp.dot(a_ref[...], w_ref[...], preferred_element_type=F32).astype(o_ref.dtype)


def _mm(a, w, out_dtype, tm, tn, ride=()):
    m, k = a.shape
    n = w.shape[1]
    tm = min(tm, m)
    grid = (m // tm, n // tn)
    plans = _plan_riders(ride, grid)
    in_specs = [
        pl.BlockSpec((tm, k), lambda i, j: (i, 0)),
        pl.BlockSpec((k, tn), lambda i, j: (0, j)),
    ]
    out_specs = [pl.BlockSpec((tm, tn), lambda i, j: (i, j))]
    out_shape = [jax.ShapeDtypeStruct((m, n), out_dtype)]
    body, args = _add_riders(_mm_kernel, [a, w], in_specs, out_specs, out_shape, grid, ride, plans)
    outs = pl.pallas_call(
        body,
        grid=grid,
        in_specs=in_specs,
        out_specs=out_specs,
        out_shape=out_shape,
        compiler_params=_params(("parallel", "arbitrary"), 56),
        name="proj_in",
    )(*args)
    return _ridden(outs, ride, plans)


def _mm2_kernel(a1_ref, a2_ref, w1_ref, w2_ref, o_ref):
    acc = jnp.dot(a1_ref[...], w1_ref[...], preferred_element_type=F32)
    acc = acc + jnp.dot(a2_ref[...], w2_ref[...], preferred_element_type=F32)
    o_ref[...] = acc.astype(o_ref.dtype)


def _mm2(a1, a2, w, out_dtype, tm, tn, ride=()):
    m, k1 = a1.shape
    n = w.shape[1]
    tm = min(tm, m)
    grid = (m // tm, n // tn)
    plans = _plan_riders(ride, grid)
    in_specs = [
        pl.BlockSpec((tm, k1), lambda i, j: (i, 0)),
        pl.BlockSpec((tm, k1), lambda i, j: (i, 0)),
        pl.BlockSpec((k1, tn), lambda i, j: (0, j)),
        pl.BlockSpec((k1, tn), lambda i, j: (1, j)),
    ]
    out_specs = [pl.BlockSpec((tm, tn), lambda i, j: (i, j))]
    out_shape = [jax.ShapeDtypeStruct((m, n), out_dtype)]
    body, args = _add_riders(_mm2_kernel, [a1, a2, w, w], in_specs, out_specs, out_shape, grid, ride, plans)
    outs = pl.pallas_call(
        body,
        grid=grid,
        in_specs=in_specs,
        out_specs=out_specs,
        out_shape=out_shape,
        compiler_params=_params(("parallel", "arbitrary"), 56),
        name="proj_out",
    )(*args)
    return _ridden(outs, ride, plans)


def _split3_const(x):
    parts = []
    r = np.float32(x)
    for _ in range(3):
        p = np.float32(np.asarray(r, dtype=jnp.bfloat16))
        parts.append(float(p))
        r = np.float32(r - p)
    return parts


def _swap_halves(x):
    half = x.shape[1] // 2
    return jnp.concatenate([x[:, half:], x[:, :half]], axis=1)


def _attn_tables(slopes):
    n_heads = len(slopes)
    qx = np.zeros((n_heads, BLOCK, LANES), np.float32)
    for h, s in enumerate(slopes):
        qx[h, :, HEAD_DIM:HEAD_DIM + 3] = _split3_const(s)
    kx = np.zeros((2 * BLOCK, LANES), np.float32)
    kx[:, HEAD_DIM:HEAD_DIM + 3] = np.arange(2 * BLOCK, dtype=np.float32)[:, None]
    return jnp.asarray(qx, BF16), jnp.asarray(kx, BF16)


def _attn_kernel(slopes, sinks_ref, qx_ref, kx_ref, q_ref, kp_ref, kc_ref, vp_ref, vc_ref, o_ref):
    n = pl.program_id(1)
    n_heads = len(slopes)
    k = jnp.concatenate([kp_ref[...], kc_ref[...]], axis=0)
    v = jnp.concatenate([vp_ref[...], vc_ref[...]], axis=0)
    c_idx = lax.broadcasted_iota(I32, (BLOCK, BLOCK), 0)
    i_idx = lax.broadcasted_iota(I32, (BLOCK, BLOCK), 1)
    from_prev = c_idx > i_idx
    prev_bias = jnp.where(n > 0, 0.0, NEG_INF)
    qpos = (lax.broadcasted_iota(I32, (1, BLOCK), 1) + BLOCK).astype(F32)
    qlane = lax.broadcasted_iota(I32, (BLOCK, LANES), 1) < HEAD_DIM
    klane = lax.broadcasted_iota(I32, (2 * BLOCK, LANES), 1) < HEAD_DIM
    scale = jnp.asarray(HEAD_DIM ** -0.5, BF16)
    outs = []
    for kv in range(n_heads // GQA_GROUP):
        heads = range(kv * GQA_GROUP, (kv + 1) * GQA_GROUP)
        kt = k[:, (kv // 2) * LANES:(kv // 2 + 1) * LANES]
        vt = v[:, (kv // 2) * LANES:(kv // 2 + 1) * LANES]
        if kv % 2:
            kt, vt = _swap_halves(kt), _swap_halves(vt)
        k_aug = jnp.where(klane, kt, kx_ref[...])
        pieces = []
        for h in heads:
            qt = q_ref[:, (h // 2) * LANES:(h // 2 + 1) * LANES]
            if h % 2:
                qt = _swap_halves(qt)
            pieces.append(jnp.where(qlane, qt * scale, qx_ref[h]))
        qg = jnp.concatenate(pieces, axis=0)
        st = lax.dot_general(k_aug, qg, (((1,), (1,)), ((), ())), preferred_element_type=F32)
        pts = []
        for g, h in enumerate(heads):
            s_prev = st[:BLOCK, g * BLOCK:(g + 1) * BLOCK] + prev_bias
            s_cur = st[BLOCK:, g * BLOCK:(g + 1) * BLOCK]
            s = jnp.where(from_prev, s_prev, s_cur)
            sink = sinks_ref[h] + np.float32(slopes[h]) * qpos
            m = jnp.maximum(jnp.max(s, axis=0, keepdims=True), sink)
            e = jnp.exp(s - m)
            denom = jnp.sum(e, axis=0, keepdims=True) + jnp.exp(sink - m)
            p = e * (1.0 / denom)
            pts.append(jnp.concatenate([jnp.where(from_prev, p, 0.0).astype(BF16),
                                        jnp.where(from_prev, 0.0, p).astype(BF16)], axis=0))
        pt = jnp.concatenate(pts, axis=1)
        og = lax.dot_general(pt, vt, (((0,), (0,)), ((), ())), preferred_element_type=F32)
        for g in range(GQA_GROUP):
            outs.append(og[g * BLOCK:(g + 1) * BLOCK, :HEAD_DIM].astype(BF16))
    o_ref[...] = jnp.concatenate(outs, axis=-1)


def _attention(proj, sinks, batch, seq, d_attn, d_kv, ride=()):
    n = proj.shape[0]
    nb = seq // BLOCK
    n_heads = d_attn // HEAD_DIM
    slopes = tuple(2.0 ** (-8.0 * (h + 1) / n_heads) for h in range(n_heads))
    kcol = d_attn // d_kv
    vcol = kcol + 1

    def cur(col):
        return lambda b, i: (b * nb + i, col)

    def prev(col):
        return lambda b, i: (b * nb + jnp.maximum(i - 1, 0), col)

    grid = (batch, nb)
    plans = _plan_riders(ride, grid)
    qx, kx = _attn_tables(slopes)
    in_specs = [
        pl.BlockSpec(memory_space=pltpu.SMEM),
        pl.BlockSpec(qx.shape, lambda b, i: (0, 0, 0)),
        pl.BlockSpec(kx.shape, lambda b, i: (0, 0)),
        pl.BlockSpec((BLOCK, d_attn), cur(0)),
        pl.BlockSpec((BLOCK, d_kv), prev(kcol)),
        pl.BlockSpec((BLOCK, d_kv), cur(kcol)),
        pl.BlockSpec((BLOCK, d_kv), prev(vcol)),
        pl.BlockSpec((BLOCK, d_kv), cur(vcol)),
    ]
    out_specs = [pl.BlockSpec((BLOCK, d_attn), cur(0))]
    out_shape = [jax.ShapeDtypeStruct((n, d_attn), BF16)]
    body, args = _add_riders(functools.partial(_attn_kernel, slopes), [sinks, qx, kx, proj, proj, proj, proj, proj],
                             in_specs, out_specs, out_shape, grid, ride, plans)
    outs = pl.pallas_call(
        body,
        grid=grid,
        in_specs=in_specs,
        out_specs=out_specs,
        out_shape=out_shape,
        compiler_params=_params(("parallel", "arbitrary"), 40),
        name="swa_attention",
    )(*args)
    return _ridden(outs, ride, plans)


def _pool_kernel(u_ref, w_ref, ps_ref, o_ref):
    g = pl.program_id(1)
    u = u_ref[...].astype(F32)
    row = lax.broadcasted_iota(I32, u.shape, 0)

    def shifted(x, k):
        return jnp.where(row >= k, pltpu.roll(x, k, 0), 0.0)

    s2 = u + shifted(u, 1)
    s4 = s2 + shifted(s2, 2)
    s8 = s4 + shifted(s4, 4)
    s16 = s8 + shifted(s8, 8)
    wsum = jnp.where(g == 0, s2, jnp.where(g == 1, s4, jnp.where(g == 2, s8, s16)))
    win = jnp.left_shift(jnp.int32(POOL_WINDOWS[0]), g)
    cnt = jnp.minimum(row + 1, win).astype(F32)
    pooled = wsum / cnt - u
    y = jnp.dot(pooled.astype(BF16), w_ref[0], preferred_element_type=F32)
    o_ref[...] = (y * ps_ref[...]).astype(BF16)


def _pool(proj, w_pool, pool_scale, batch, seq, u_col0):
    n = proj.shape[0]
    ng, c, _ = w_pool.shape
    assert POOL_WINDOWS == tuple(POOL_WINDOWS[0] << g for g in range(ng))
    col0 = u_col0 // c
    return pl.pallas_call(
        _pool_kernel,
        grid=(batch, ng),
        in_specs=[
            pl.BlockSpec((seq, c), lambda b, g: (b, col0 + g)),
            pl.BlockSpec((1, c, c), lambda b, g: (g, 0, 0)),
            pl.BlockSpec((1, c), lambda b, g: (0, g)),
        ],
        out_specs=pl.BlockSpec((seq, c), lambda b, g: (b, g)),
        out_shape=jax.ShapeDtypeStruct((n, ng * c), BF16),
        compiler_params=_params(("parallel", "arbitrary"), 48),
        name="pool_mixer",
    )(proj, w_pool, pool_scale.reshape(1, ng * c))


def _top2_route(logits):
    lane = lax.broadcasted_iota(I32, logits.shape, 1).astype(F32)
    m1 = jnp.max(logits, axis=-1, keepdims=True)
    i1 = jnp.min(jnp.where(logits == m1, lane, float(LANES)), axis=-1, keepdims=True)
    l2 = jnp.where(lane == i1, -jnp.inf, logits)
    m2 = jnp.max(l2, axis=-1, keepdims=True)
    i2 = jnp.min(jnp.where(l2 == m2, lane, float(LANES)), axis=-1, keepdims=True)
    e = jnp.exp(m2 - m1)
    w1 = 1.0 / (1.0 + e)
    w2 = e / (1.0 + e)
    return jnp.where(lane == 0.0, i1, jnp.where(lane == 1.0, i2, jnp.where(lane == 2.0, w1, jnp.where(lane == 3.0, w2, 0.0))))


def _post_kernel(mode, gathered, *refs):
    refs = list(refs)
    if gathered:
        pos_ref, pos_next_ref, y_hbm, route_in = refs[:4]
        refs = refs[4:]
    else:
        f_ref = refs.pop(0)
    x_ref, gate_ref, gpost_ref = refs[:3]
    refs = refs[3:]
    if mode != "last":
        gpre_ref, sh_ref, sc_ref = refs[:3]
        refs = refs[3:]
    if mode == "route":
        rw_ref, rb_ref = refs[:2]
        refs = refs[2:]
    xo_ref = refs.pop(0)
    if mode != "last":
        ho_ref = refs.pop(0)
    if mode == "route":
        ro_ref = refs.pop(0)

    if gathered:
        ybuf, sems = refs
        tr = x_ref.shape[0]
        i = pl.program_id(0)
        slot = i % 2

        def gather(idx_ref, s):
            def start(r, carry):
                for kk in range(2):
                    pltpu.make_async_copy(y_hbm.at[pl.ds(idx_ref[0, 0, 2 * r + kk], 1)],
                                          ybuf.at[s, kk, pl.ds(r, 1)], sems.at[s]).start()
                return carry
            lax.fori_loop(0, tr, start, 0, unroll=8)

        @pl.when(i == 0)
        def _():
            gather(pos_ref, 0)

        @pl.when(i + 1 < pl.num_programs(0))
        def _():
            gather(pos_next_ref, 1 - slot)

        pltpu.make_async_copy(ybuf.at[slot], ybuf.at[slot], sems.at[slot]).wait()
        rt = route_in[...]
        a_hi, a_lo = _unpack_pair_f32(ybuf[slot, 0])
        b_hi, b_lo = _unpack_pair_f32(ybuf[slot, 1])
        w_a, w_b = rt[:, 2:3], rt[:, 3:4]
        f = jnp.concatenate([a_hi * w_a + b_hi * w_b, a_lo * w_a + b_lo * w_b], axis=1)
    else:
        f = f_ref[...].astype(F32)

    xn = x_ref[...] + gate_ref[0] * _rms(f, gpost_ref[...])
    xo_ref[...] = xn
    if mode == "last":
        return
    h = _rms(xn, gpre_ref[...]) * (1.0 + sc_ref[0]) + sh_ref[0]
    if mode == "next":
        ho_ref[...] = h.astype(BF16)
        return
    half = h.shape[1] // 2
    ho_ref[...] = _pack_pair(h[:, :half], h[:, half:])
    logits = jnp.dot(h.astype(BF16), rw_ref[...], preferred_element_type=F32) + rb_ref[...]
    ro_ref[...] = _top2_route(logits)


def _post(mode, f, x2, gate, g_post, seq, nxt=None, router=None, gather=None):
    n, d = x2.shape
    tr = min(256, seq)
    per = seq // tr
    row = lambda i: (i, 0)
    fix = lambda i: (0, 0)
    bat = lambda i: (i // per, 0, 0)
    in_specs, args, scratch = [], [], []
    if gather is not None:
        pos, y, route = gather
        last = n // tr - 1
        pos3 = pos.reshape(n // tr, 1, 2 * tr)
        in_specs += [
            pl.BlockSpec((1, 1, 2 * tr), lambda i: (i, 0, 0), memory_space=pltpu.SMEM),
            pl.BlockSpec((1, 1, 2 * tr), lambda i: (jnp.minimum(i + 1, last), 0, 0), memory_space=pltpu.SMEM),
            pl.BlockSpec(memory_space=pl.ANY),
            pl.BlockSpec((tr, LANES), row),
        ]
        args += [pos3, pos3, y, route]
        scratch = [pltpu.VMEM((2, 2, tr, d // 2), U32), pltpu.SemaphoreType.DMA((2,))]
    else:
        in_specs.append(pl.BlockSpec((tr, d), row))
        args.append(f)
    in_specs += [pl.BlockSpec((tr, d), row), pl.BlockSpec((1, 1, d), bat), pl.BlockSpec((1, d), fix)]
    args += [x2, gate, g_post.reshape(1, d)]
    out_specs = [pl.BlockSpec((tr, d), row)]
    out_shape = [jax.ShapeDtypeStruct((n, d), F32)]
    if mode != "last":
        g_pre, shift, scale = nxt
        in_specs += [pl.BlockSpec((1, d), fix), pl.BlockSpec((1, 1, d), bat), pl.BlockSpec((1, 1, d), bat)]
        args += [g_pre.reshape(1, d), shift, scale]
    if mode == "next":
        out_specs.append(pl.BlockSpec((tr, d), row))
        out_shape.append(jax.ShapeDtypeStruct((n, d), BF16))
    if mode == "route":
        rw, rb = router
        in_specs += [pl.BlockSpec((d, LANES), fix), pl.BlockSpec((1, LANES), fix)]
        args += [rw, rb]
        out_specs += [pl.BlockSpec((tr, d // 2), row), pl.BlockSpec((tr, LANES), row)]
        out_shape += [jax.ShapeDtypeStruct((n, d // 2), U32), jax.ShapeDtypeStruct((n, LANES), F32)]
    return pl.pallas_call(
        functools.partial(_post_kernel, mode, gather is not None),
        grid=(n // tr,),
        in_specs=in_specs,
        out_specs=out_specs,
        out_shape=out_shape,
        scratch_shapes=scratch,
        compiler_params=_params(("arbitrary",), 56),
        name="post_" + mode + ("_gather" if gather is not None else ""),
    )(*args)


def _ffn_loop_kernel(dff, tf, n_ride, rb, h_ref, wg_hbm, wu_hbm, wd_hbm, *rest):
    src = rest[:n_ride]
    o_hbm = rest[n_ride]
    dst = rest[n_ride + 1:2 * n_ride + 1]
    acc, ostage, wgb, wub, wdb, sems, osem = rest[2 * n_ride + 1:2 * n_ride + 8]
    i = pl.program_id(0)
    n_full = dff // tf
    tail = dff - n_full * tf
    if n_ride:
        rin, rout, rsems = rest[2 * n_ride + 8:]
        per = src[0].shape[0] // rb
        n_blk = n_ride * per

        def rows_of(g, k):
            return pl.ds(pl.multiple_of((g - k * per) * rb, rb), rb)

        def ride_in(g, k):
            return pltpu.make_async_copy(src[k].at[rows_of(g, k)], rin.at[g % 2], rsems.at[0, g % 2])

        def ride_out(g, k):
            return pltpu.make_async_copy(rout.at[g % 2], dst[k].at[rows_of(g, k)], rsems.at[1, g % 2])

        def for_block(g, fn):
            for k in range(n_ride):
                @pl.when((g >= k * per) & (g < (k + 1) * per))
                def _():
                    fn(g, k)

    def copies(c, slot, width):
        start = pl.multiple_of(c * tf, LANES)
        return (pltpu.make_async_copy(wg_hbm.at[:, pl.ds(start, width)], wgb.at[slot, :, pl.ds(0, width)],
                                      sems.at[slot, 0]),
                pltpu.make_async_copy(wu_hbm.at[:, pl.ds(start, width)], wub.at[slot, :, pl.ds(0, width)],
                                      sems.at[slot, 1]),
                pltpu.make_async_copy(wd_hbm.at[pl.ds(start, width), :], wdb.at[slot, pl.ds(0, width), :],
                                      sems.at[slot, 2]))

    def start(c, slot, width=tf):
        for cp in copies(c, slot, width):
            cp.start()

    def wait(c, slot, width=tf):
        for cp in copies(c, slot, width):
            cp.wait()

    def multiply(slot, width=tf):
        h = h_ref[...]
        g = jnp.dot(h, wgb[slot, :, :width], preferred_element_type=F32)
        u = jnp.dot(h, wub[slot, :, :width], preferred_element_type=F32)
        a = (g * _sigmoid(g) * u).astype(BF16)
        cw = acc.shape[1] // 4
        for q in range(4):
            acc[:, q * cw:(q + 1) * cw] += jnp.dot(a, wdb[slot, :width, q * cw:(q + 1) * cw],
                                                   preferred_element_type=F32)

    @pl.when(i == 0)
    def _():
        start(0, 0)
        if n_ride:
            ride_in(0, 0).start()

    acc[...] = jnp.zeros_like(acc)

    def step(c, carry):
        slot = c % 2
        wait(c, slot)
        if n_ride:
            g = i * (n_full - 1) + c
            for_block(g, lambda g, k: ride_in(g, k).wait())
            for_block(g + 1, lambda g, k: ride_in(g, k).start())
            for_block(g - 2, lambda g, k: ride_out(g, k).wait())
        start(c + 1, 1 - slot)
        multiply(slot)
        if n_ride:
            rout[g % 2] = rin[g % 2].astype(BF16)
            for_block(g, lambda g, k: ride_out(g, k).start())
        return carry

    lax.fori_loop(0, n_full - 1, step, 0)
    last, last_slot = n_full - 1, (n_full - 1) % 2
    wait(last, last_slot)
    if tail:
        start(n_full, 1 - last_slot, tail)
    else:
        @pl.when(i + 1 < pl.num_programs(0))
        def _():
            start(0, 1 - last_slot)
    multiply(last_slot)
    if tail:
        wait(n_full, 1 - last_slot, tail)

        @pl.when(i + 1 < pl.num_programs(0))
        def _():
            start(0, 0)
        multiply(1 - last_slot, tail)

    tm = acc.shape[0]

    def store(tile):
        return pltpu.make_async_copy(ostage, o_hbm.at[pl.ds(pl.multiple_of(tile * tm, tm), tm)], osem)

    @pl.when(i > 0)
    def _():
        store(i - 1).wait()

    ostage[...] = acc[...].astype(ostage.dtype)
    store(i).start()

    @pl.when(i + 1 == pl.num_programs(0))
    def _():
        store(i).wait()


def _ffn_loop(h, wg, wu, wd, tm, tf, ride=()):
    n, d = h.shape
    dff = wg.shape[1]
    tm = min(tm, n)
    n_tiles = n // tm
    n_full = dff // tf
    assert n_full >= 2 and (n_full + (1 if dff % tf else 0)) % 2 == 0 and (dff % tf) % LANES == 0
    ride_w = [w for w, _ in ride]
    rb = None
    if ride_w:
        rows, cols = ride_w[0].shape
        assert all(w.shape == (rows, cols) for w in ride_w)
        hosts = n_tiles * (n_full - 1) - 2
        for cand in range(BF16_SUBLANES, rows + 1, BF16_SUBLANES):
            if rows % cand == 0 and len(ride_w) * (rows // cand) <= hosts:
                rb = cand if cand * cols * 12 <= 2 * RIDER_BLOCK_BYTES else None
                break
    if rb is None:
        ride_w = []
    n_ride = len(ride_w)
    any_spec = pl.BlockSpec(memory_space=pl.ANY)
    scratch = [pltpu.VMEM((tm, d), F32), pltpu.VMEM((tm, d), BF16), pltpu.VMEM((2, d, tf), BF16),
               pltpu.VMEM((2, d, tf), BF16), pltpu.VMEM((2, tf, d), BF16), pltpu.SemaphoreType.DMA((2, 3)),
               pltpu.SemaphoreType.DMA(())]
    out_specs = [any_spec]
    out_shape = [jax.ShapeDtypeStruct((n, d), BF16)]
    if n_ride:
        scratch += [pltpu.VMEM((2, rb, cols), F32), pltpu.VMEM((2, rb, cols), BF16), pltpu.SemaphoreType.DMA((2, 2))]
        out_specs += [any_spec] * n_ride
        out_shape += [jax.ShapeDtypeStruct((rows, cols), BF16)] * n_ride
    outs = pl.pallas_call(
        functools.partial(_ffn_loop_kernel, dff, tf, n_ride, rb),
        grid=(n_tiles,),
        in_specs=[pl.BlockSpec((tm, d), lambda i: (i, 0))] + [any_spec] * (3 + n_ride),
        out_specs=out_specs,
        out_shape=out_shape,
        scratch_shapes=scratch,
        compiler_params=_params(("arbitrary",), 58),
        name="dense_swiglu",
    )(h, wg, wu, wd, *ride_w)
    casts = tuple(outs[1:]) if n_ride else tuple(_select(w, lead).astype(BF16) for w, lead in ride)
    return outs[0], casts


def _expert_kernel(te_ref, nt_ref, rows_ref, idx_ref, idx_next_ref, src_hbm, wg_hbm, wu_hbm, wd_hbm, y_ref,
                   xs_buf, h_scr, acc, wgb, wub, wdb, sem, wsems):
    t = pl.program_id(0)
    n_live = nt_ref[0]
    live = t < n_live
    tm, half = xs_buf.shape
    tf = wgb.shape[2]
    n_chunks = wg_hbm.shape[2] // tf
    top = tm // 2
    mostly_padding = rows_ref[t] <= top

    def gather(iref):
        def start(r, carry):
            pltpu.make_async_copy(src_hbm.at[pl.ds(iref[0, 0, r], 1)], xs_buf.at[pl.ds(r, 1)], sem).start()
            return carry
        lax.fori_loop(0, tm, start, 0, unroll=8)

    def copies(e, c, slot):
        start = pl.multiple_of(c * tf, LANES)
        return (pltpu.make_async_copy(wg_hbm.at[e, :, pl.ds(start, tf)], wgb.at[slot], wsems.at[slot, 0]),
                pltpu.make_async_copy(wu_hbm.at[e, :, pl.ds(start, tf)], wub.at[slot], wsems.at[slot, 1]),
                pltpu.make_async_copy(wd_hbm.at[e, pl.ds(start, tf), :], wdb.at[slot], wsems.at[slot, 2]))

    def start(e, c, slot):
        for cp in copies(e, c, slot):
            cp.start()

    def wait(e, c, slot):
        for cp in copies(e, c, slot):
            cp.wait()

    def multiply(slot, rows):
        h = h_scr[:rows]
        g = jnp.dot(h, wgb[slot], preferred_element_type=F32)
        u = jnp.dot(h, wub[slot], preferred_element_type=F32)
        a = (g * _sigmoid(g) * u).astype(BF16)
        cw = acc.shape[1] // 4
        for q in range(4):
            acc[:rows, q * cw:(q + 1) * cw] += jnp.dot(a, wdb[slot, :, q * cw:(q + 1) * cw],
                                                       preferred_element_type=F32)

    @pl.when(t == 0)
    def _():
        gather(idx_ref)
        start(te_ref[0], 0, 0)

    @pl.when(live)
    def _():
        e = te_ref[t]
        pltpu.make_async_copy(xs_buf, xs_buf, sem).wait()
        hi, lo = _unpack_pair(xs_buf[...])
        h_scr[:, :half] = hi
        h_scr[:, half:] = lo
        acc[...] = jnp.zeros_like(acc)

        @pl.when(t + 1 < n_live)
        def _():
            gather(idx_next_ref)

        def run(rows):
            def step(c, carry):
                slot = c % 2
                wait(e, c, slot)
                start(e, c + 1, 1 - slot)
                multiply(slot, rows)
                return carry

            lax.fori_loop(0, n_chunks - 1, step, 0)
            last_slot = (n_chunks - 1) % 2
            wait(e, n_chunks - 1, last_slot)

            @pl.when(t + 1 < n_live)
            def _():
                start(te_ref[jnp.minimum(t + 1, pl.num_programs(0) - 1)], 0, 1 - last_slot)
            multiply(last_slot, rows)

        @pl.when(jnp.logical_not(mostly_padding))
        def _():
            run(tm)

        @pl.when(mostly_padding)
        def _():
            run(top)

        y_ref[...] = _pack_pair(acc[:, :half], acc[:, half:])

    @pl.when(jnp.logical_not(live))
    def _():
        y_ref[...] = jnp.zeros_like(y_ref)


def _experts(src, src_tok, tile_expert, n_tiles, tile_rows, wg, wu, wd, tm, tf):
    half = src.shape[1]
    d = 2 * half
    rows = src_tok.shape[0]
    dff = wg.shape[2]
    n_t = rows // tm
    idx3 = src_tok.reshape(n_t, 1, tm)
    assert dff % tf == 0 and (dff // tf) % 2 == 0
    return pl.pallas_call(
        _expert_kernel,
        grid_spec=pltpu.PrefetchScalarGridSpec(
            num_scalar_prefetch=3,
            grid=(n_t,),
            in_specs=[
                pl.BlockSpec((1, 1, tm), lambda t, te, nt, tr: (t, 0, 0), memory_space=pltpu.SMEM),
                pl.BlockSpec((1, 1, tm), lambda t, te, nt, tr: (jnp.minimum(t + 1, n_t - 1), 0, 0),
                             memory_space=pltpu.SMEM),
                pl.BlockSpec(memory_space=pl.ANY),
                pl.BlockSpec(memory_space=pl.ANY),
                pl.BlockSpec(memory_space=pl.ANY),
                pl.BlockSpec(memory_space=pl.ANY),
            ],
            out_specs=pl.BlockSpec((tm, half), lambda t, te, nt, tr: (t, 0)),
            scratch_shapes=[pltpu.VMEM((tm, half), U32), pltpu.VMEM((tm, d), BF16), pltpu.VMEM((tm, d), F32),
                            pltpu.VMEM((2, d, tf), BF16), pltpu.VMEM((2, d, tf), BF16), pltpu.VMEM((2, tf, d), BF16),
                            pltpu.SemaphoreType.DMA(()), pltpu.SemaphoreType.DMA((2, 3))],
        ),
        out_shape=jax.ShapeDtypeStruct((rows, half), U32),
        compiler_params=_params(("arbitrary",), 58),
        name="expert_swiglu",
    )(tile_expert, n_tiles, tile_rows, idx3, idx3, src, wg, wu, wd)


def _moe_plan(top_idx, tm):
    n = top_idx.shape[0]
    p = 2 * n
    flat = top_idx.reshape(p)
    experts = jnp.arange(N_EXPERTS, dtype=I32)
    onehot = (flat[:, None] == experts[None, :]).astype(I32)
    csum = jnp.cumsum(onehot, axis=0)
    rank = jnp.sum(onehot * csum, axis=1) - 1
    counts = csum[-1]
    tiles_per = (counts + (tm - 1)) // tm
    tile_end = jnp.cumsum(tiles_per)
    tile_start = tile_end - tiles_per
    pos = (tile_start * tm)[flat] + rank
    n_tiles = tile_end[-1:]
    t_max = p // tm + N_EXPERTS
    t_ids = jnp.arange(t_max, dtype=I32)
    te = jnp.sum((t_ids[:, None] >= tile_end[None, :]).astype(I32), axis=1)
    last_e = jnp.max(jnp.where(tiles_per > 0, experts, 0))
    te = jnp.minimum(te, last_e)
    tile_rows = jnp.clip(counts[te] - (t_ids - tile_start[te]) * tm, 0, tm)
    src_tok = jnp.zeros((t_max * tm,), I32).at[pos].set(jnp.arange(p, dtype=I32) // 2)
    return pos.astype(I32), src_tok, te.astype(I32), n_tiles.astype(I32), tile_rows.astype(I32)


def kernel(x, c, w_ada, b_ada, norm_pre_mix, norm_post_mix, norm_pre_ffn, norm_post_ffn, w_in, sinks, w_pool,
           pool_scale, w_out, ffn_w_gate, ffn_w_up, ffn_w_down, router_w, router_b, moe_w_gate, moe_w_up, moe_w_down):
    batch, seq, d = x.shape
    depth = w_ada.shape[0]
    n = batch * seq
    d_attn = d // 2
    n_heads = sinks.shape[1]
    d_kv = (n_heads // GQA_GROUP) * HEAD_DIM
    u_col0 = d_attn + 2 * d_kv
    assert n_heads * HEAD_DIM == d_attn and seq % BLOCK == 0

    x2 = x.reshape(n, d)
    mod = _ada(c, w_ada, b_ada)

    def mods(l):
        return [mod[l, :, k * d:(k + 1) * d].reshape(batch, 1, d) for k in range(N_MOD)]

    shift1, scale1, gate1, shift2, scale2, gate2 = mods(0)
    h = _prenorm(x2, norm_pre_mix[0], shift1, scale1, seq)
    flat = lambda w: w.reshape(-1, w.shape[-1])
    ready = {}

    def hosted(host, rides, *a):
        out, casts = host(*a, ride=[(w, lead) for _, w, lead in rides])
        ready.update({key: v for (key, _, _), v in zip(rides, casts)})
        return out

    def take(key, w, lead=None):
        shape = w.shape if lead is None else w.shape[1:]
        return ready.pop(key).reshape(shape) if key in ready else _select(w, lead).astype(BF16)

    for l in range(depth):
        i = l // 2
        dense = l % 2 == 0
        nm = (l + 1) // 2 if (dense and l + 1 < depth) else None
        on_proj_in, on_attn, on_proj_out, on_ffn = [(("w_out", l), w_out, l)], [], [], []
        if dense:
            on_proj_in += [(("ffn_wg", i), ffn_w_gate, i), (("ffn_wd", i), ffn_w_down, i)]
            on_proj_out.append((("ffn_wu", i), ffn_w_up, i))
            if nm is not None:
                on_attn.append((("moe_wg", nm), flat(moe_w_gate[nm]), None))
                on_ffn.append((("moe_wd", nm), flat(moe_w_down[nm]), None))
            if l + 1 < depth:
                on_proj_out.append((("w_in", l + 1), w_in, l + 1))
        else:
            on_attn.append((("moe_wu", i), flat(moe_w_up[i]), None))

        proj = hosted(_mm, on_proj_in, h, take(("w_in", l), w_in, l), BF16, PROJ_ROWS, PROJ_IN_COLS)
        attn = hosted(_attention, on_attn, proj, sinks[l], batch, seq, d_attn, d_kv)
        pool = _pool(proj, w_pool[l].astype(BF16), pool_scale[l], batch, seq, u_col0)
        mix = hosted(_mm2, on_proj_out, attn, pool, take(("w_out", l), w_out, l), BF16, PROJ_ROWS, PROJ_OUT_COLS)
        nxt = (norm_pre_ffn[l], shift2, scale2)
        if dense:
            x2, h2 = _post("next", mix, x2, gate1, norm_post_mix[l], seq, nxt=nxt)
            f = hosted(_ffn_loop, on_ffn, h2, take(("ffn_wg", i), ffn_w_gate, i), take(("ffn_wu", i), ffn_w_up, i),
                       take(("ffn_wd", i), ffn_w_down, i), FFN_ROWS, FFN_CHUNK)
            gather = None
        else:
            rw = jnp.zeros((d, LANES), BF16).at[:, :N_EXPERTS].set(router_w[i].astype(BF16))
            rb = jnp.full((1, LANES), NEG_INF, F32).at[0, :N_EXPERTS].set(router_b[i])
            x2, h2p, route = _post("route", mix, x2, gate1, norm_post_mix[l], seq, nxt=nxt, router=(rw, rb))
            pos, src_tok, tile_expert, n_tiles, tile_rows = _moe_plan(route[:, :2].astype(I32), FFN_ROWS)
            y = _experts(h2p, src_tok, tile_expert, n_tiles, tile_rows, take(("moe_wg", i), moe_w_gate[i]),
                         take(("moe_wu", i), moe_w_up[i]), take(("moe_wd", i), moe_w_down[i]), FFN_ROWS, FFN_CHUNK)
            f, gather = None, (pos, y, route)
        if l + 1 < depth:
            shift1, scale1, gate1n, shift2n, scale2n, gate2n = mods(l + 1)
            x2, h = _post("next", f, x2, gate2, norm_post_ffn[l], seq,
                          nxt=(norm_pre_mix[l + 1], shift1, scale1), gather=gather)
            gate1, shift2, scale2, gate2 = gate1n, shift2n, scale2n, gate2n
        else:
            (x2,) = _post("last", f, x2, gate2, norm_post_ffn[l], seq, gather=gather)
    return x2.reshape(batch, seq, d)
```

```python
import functools

import jax
import jax.numpy as jnp
import numpy as np
from jax import lax
from jax.experimental import pallas as pl
from jax.experimental.pallas import tpu as pltpu

F32 = jnp.float32
BF16 = jnp.bfloat16
U32 = jnp.uint32
I32 = jnp.int32

EPS = 1e-6
NEG_INF = -1e30
BLOCK = 128
HEAD_DIM = 64
GQA_GROUP = 8
POOL_WINDOWS = (2, 4, 8, 16)
N_EXPERTS = 8
N_MOD = 6
LANES = 128
PROJ_ROWS = 1024
PROJ_IN_COLS = 768
PROJ_OUT_COLS = 1024
FFN_ROWS = 512
FFN_CHUNK = 512
WEIGHT_DMA_PRIORITY = 1
BF16_SUBLANES = 16
RIDER_BLOCK_BYTES = 4 << 20

MIB = 1 << 20


def _params(sem, vmem_mib):
    return pltpu.CompilerParams(dimension_semantics=sem, vmem_limit_bytes=vmem_mib * MIB)


def _sigmoid(x):
    return 1.0 / (1.0 + jnp.exp(-x))


def _rms(x, g):
    return x * lax.rsqrt(jnp.mean(x * x, axis=-1, keepdims=True) + EPS) * g


def _pack_pair(a, b):
    ua = pltpu.bitcast(a.astype(BF16).astype(F32), U32)
    ub = pltpu.bitcast(b.astype(BF16).astype(F32), U32)
    return ua | (ub >> 16)


def _unpack_pair_f32(p):
    return pltpu.bitcast(p & jnp.uint32(0xFFFF0000), F32), pltpu.bitcast(p << 16, F32)


def _unpack_pair(p):
    hi, lo = _unpack_pair_f32(p)
    return hi.astype(BF16), lo.astype(BF16)


def _select(w, lead):
    return w if lead is None else w[lead]


def _plan_riders(ride, grid):
    steps = grid[0] * grid[1]
    plans = []
    for w, _ in ride:
        rows, cols = w.shape[-2:]
        plan = None
        for rb in range(BF16_SUBLANES, rows + 1, BF16_SUBLANES):
            if rows % rb == 0 and rows // rb <= steps:
                plan = rb if rb * cols * 4 <= RIDER_BLOCK_BYTES else None
                break
        plans.append(plan)
    return plans


def _ridden(outs, ride, plans):
    extra = iter(outs[1:])
    return outs[0], tuple(next(extra) if rb is not None else _select(w, lead).astype(BF16)
                          for (w, lead), rb in zip(ride, plans))


def _add_riders(kernel_fn, args, in_specs, out_specs, out_shape, grid, ride, plans):
    riders = [(w, lead, rb) for (w, lead), rb in zip(ride, plans) if rb is not None]
    if not riders:
        return kernel_fn, list(args)
    n1 = grid[1]
    n_in, n_out, n_r = len(in_specs), len(out_specs), len(riders)
    for w, lead, rb in riders:
        rows, cols = w.shape[-2:]
        last = rows // rb - 1
        blk = functools.partial(lambda a, b, last: (jnp.minimum(a * n1 + b, last), 0), last=last)
        if lead is None:
            in_specs.append(pl.BlockSpec((rb, cols), blk))
        else:
            in_specs.append(pl.BlockSpec((None, rb, cols), functools.partial(
                lambda a, b, last, lead: (lead, jnp.minimum(a * n1 + b, last), 0), last=last, lead=lead)))
        out_specs.append(pl.BlockSpec((rb, cols), blk))
        out_shape.append(jax.ShapeDtypeStruct((rows, cols), BF16))

    def body(*refs):
        ins, srcs = refs[:n_in], refs[n_in:n_in + n_r]
        refs = refs[n_in + n_r:]
        outs, dsts = refs[:n_out], refs[n_out:n_out + n_r]
        kernel_fn(*ins, *outs, *refs[n_out + n_r:])
        for src, dst in zip(srcs, dsts):
            dst[...] = src[...].astype(BF16)

    return body, list(args) + [w for w, _, _ in riders]


def _ada_kernel(c_ref, w_ref, b_ref, o_ref):
    c = c_ref[...]
    ca = (c * _sigmoid(c)).astype(BF16)
    o_ref[0] = jnp.dot(ca, w_ref[0].astype(BF16), preferred_element_type=F32) + b_ref[0]


def _ada(c, w_ada, b_ada):
    depth, d, nm = w_ada.shape
    b = c.shape[0]
    tn = 512
    return pl.pallas_call(
        _ada_kernel,
        grid=(depth, nm // tn),
        in_specs=[
            pl.BlockSpec((b, d), lambda l, j: (0, 0)),
            pl.BlockSpec((1, d, tn), lambda l, j: (l, 0, j)),
            pl.BlockSpec((1, 1, tn), lambda l, j: (l, 0, j)),
        ],
        out_specs=pl.BlockSpec((1, b, tn), lambda l, j: (l, 0, j)),
        out_shape=jax.ShapeDtypeStruct((depth, b, nm), F32),
        compiler_params=_params(("parallel", "parallel"), 40),
        name="ada_mod",
    )(c, w_ada, b_ada.reshape(depth, 1, nm))


def _prenorm_kernel(x_ref, g_ref, sh_ref, sc_ref, o_ref):
    h = _rms(x_ref[...], g_ref[...]) * (1.0 + sc_ref[0]) + sh_ref[0]
    o_ref[...] = h.astype(BF16)


def _prenorm(x2, g, shift, scale, seq):
    n, d = x2.shape
    tr = min(512, seq)
    per = seq // tr
    return pl.pallas_call(
        _prenorm_kernel,
        grid=(n // tr,),
        in_specs=[
            pl.BlockSpec((tr, d), lambda i: (i, 0)),
            pl.BlockSpec((1, d), lambda i: (0, 0)),
            pl.BlockSpec((1, 1, d), lambda i: (i // per, 0, 0)),
            pl.BlockSpec((1, 1, d), lambda i: (i // per, 0, 0)),
        ],
        out_specs=pl.BlockSpec((tr, d), lambda i: (i, 0)),
        out_shape=jax.ShapeDtypeStruct((n, d), BF16),
        compiler_params=_params(("parallel",), 40),
        name="prenorm",
    )(x2, g.reshape(1, d), shift, scale)


def _mm_kernel(a_ref, w_ref, o_ref):
    o_ref[...] = jnp.dot(a_ref[...], w_ref[...], preferred_element_type=F32).astype(o_ref.dtype)


def _mm(a, w, out_dtype, tm, tn, ride=()):
    m, k = a.shape
    n = w.shape[1]
    tm = min(tm, m)
    grid = (m // tm, n // tn)
    plans = _plan_riders(ride, grid)
    in_specs = [
        pl.BlockSpec((tm, k), lambda i, j: (i, 0)),
        pl.BlockSpec((k, tn), lambda i, j: (0, j)),
    ]
    out_specs = [pl.BlockSpec((tm, tn), lambda i, j: (i, j))]
    out_shape = [jax.ShapeDtypeStruct((m, n), out_dtype)]
    body, args = _add_riders(_mm_kernel, [a, w], in_specs, out_specs, out_shape, grid, ride, plans)
    outs = pl.pallas_call(
        body,
        grid=grid,
        in_specs=in_specs,
        out_specs=out_specs,
        out_shape=out_shape,
        compiler_params=_params(("parallel", "arbitrary"), 56),
        name="proj_in",
    )(*args)
    return _ridden(outs, ride, plans)


def _mm2_kernel(a1_ref, a2_ref, w1_ref, w2_ref, o_ref):
    acc = jnp.dot(a1_ref[...], w1_ref[...], preferred_element_type=F32)
    acc = acc + jnp.dot(a2_ref[...], w2_ref[...], preferred_element_type=F32)
    o_ref[...] = acc.astype(o_ref.dtype)


def _mm2(a1, a2, w, out_dtype, tm, tn, ride=()):
    m, k1 = a1.shape
    n = w.shape[1]
    tm = min(tm, m)
    grid = (m // tm, n // tn)
    plans = _plan_riders(ride, grid)
    in_specs = [
        pl.BlockSpec((tm, k1), lambda i, j: (i, 0)),
        pl.BlockSpec((tm, k1), lambda i, j: (i, 0)),
        pl.BlockSpec((k1, tn), lambda i, j: (0, j)),
        pl.BlockSpec((k1, tn), lambda i, j: (1, j)),
    ]
    out_specs = [pl.BlockSpec((tm, tn), lambda i, j: (i, j))]
    out_shape = [jax.ShapeDtypeStruct((m, n), out_dtype)]
    body, args = _add_riders(_mm2_kernel, [a1, a2, w, w], in_specs, out_specs, out_shape, grid, ride, plans)
    outs = pl.pallas_call(
        body,
        grid=grid,
        in_specs=in_specs,
        out_specs=out_specs,
        out_shape=out_shape,
        compiler_params=_params(("parallel", "arbitrary"), 56),
        name="proj_out",
    )(*args)
    return _ridden(outs, ride, plans)


def _split3_const(x):
    parts = []
    r = np.float32(x)
    for _ in range(3):
        p = np.float32(np.asarray(r, dtype=jnp.bfloat16))
        parts.append(float(p))
        r = np.float32(r - p)
    return parts


def _swap_halves(x):
    half = x.shape[1] // 2
    return jnp.concatenate([x[:, half:], x[:, :half]], axis=1)


def _attn_tables(slopes):
    n_heads = len(slopes)
    qx = np.zeros((n_heads, BLOCK, LANES), np.float32)
    for h, s in enumerate(slopes):
        qx[h, :, HEAD_DIM:HEAD_DIM + 3] = _split3_const(s)
    kx = np.zeros((2 * BLOCK, LANES), np.float32)
    kx[:, HEAD_DIM:HEAD_DIM + 3] = np.arange(2 * BLOCK, dtype=np.float32)[:, None]
    return jnp.asarray(qx, BF16), jnp.asarray(kx, BF16)


def _attn_kernel(slopes, sinks_ref, qx_ref, kx_ref, q_ref, kp_ref, kc_ref, vp_ref, vc_ref, o_ref):
    n = pl.program_id(1)
    n_heads = len(slopes)
    k = jnp.concatenate([kp_ref[...], kc_ref[...]], axis=0)
    v = jnp.concatenate([vp_ref[...], vc_ref[...]], axis=0)
    c_idx = lax.broadcasted_iota(I32, (BLOCK, BLOCK), 0)
    i_idx = lax.broadcasted_iota(I32, (BLOCK, BLOCK), 1)
    from_prev = c_idx > i_idx
    prev_bias = jnp.where(n > 0, 0.0, NEG_INF)
    qpos = (lax.broadcasted_iota(I32, (1, BLOCK), 1) + BLOCK).astype(F32)
    qlane = lax.broadcasted_iota(I32, (BLOCK, LANES), 1) < HEAD_DIM
    klane = lax.broadcasted_iota(I32, (2 * BLOCK, LANES), 1) < HEAD_DIM
    scale = jnp.asarray(HEAD_DIM ** -0.5, BF16)
    outs = []
    for kv in range(n_heads // GQA_GROUP):
        heads = range(kv * GQA_GROUP, (kv + 1) * GQA_GROUP)
        kt = k[:, (kv // 2) * LANES:(kv // 2 + 1) * LANES]
        vt = v[:, (kv // 2) * LANES:(kv // 2 + 1) * LANES]
        if kv % 2:
            kt, vt = _swap_halves(kt), _swap_halves(vt)
        k_aug = jnp.where(klane, kt, kx_ref[...])
        pieces = []
        for h in heads:
            qt = q_ref[:, (h // 2) * LANES:(h // 2 + 1) * LANES]
            if h % 2:
                qt = _swap_halves(qt)
            pieces.append(jnp.where(qlane, qt * scale, qx_ref[h]))
        qg = jnp.concatenate(pieces, axis=0)
        st = lax.dot_general(k_aug, qg, (((1,), (1,)), ((), ())), preferred_element_type=F32)
        pts = []
        for g, h in enumerate(heads):
            s_prev = st[:BLOCK, g * BLOCK:(g + 1) * BLOCK] + prev_bias
            s_cur = st[BLOCK:, g * BLOCK:(g + 1) * BLOCK]
            s = jnp.where(from_prev, s_prev, s_cur)
            sink = sinks_ref[h] + np.float32(slopes[h]) * qpos
            m = jnp.maximum(jnp.max(s, axis=0, keepdims=True), sink)
            e = jnp.exp(s - m)
            denom = jnp.sum(e, axis=0, keepdims=True) + jnp.exp(sink - m)
            p = e * (1.0 / denom)
            pts.append(jnp.concatenate([jnp.where(from_prev, p, 0.0).astype(BF16),
                                        jnp.where(from_prev, 0.0, p).astype(BF16)], axis=0))
        pt = jnp.concatenate(pts, axis=1)
        og = lax.dot_general(pt, vt, (((0,), (0,)), ((), ())), preferred_element_type=F32)
        for g in range(GQA_GROUP):
            outs.append(og[g * BLOCK:(g + 1) * BLOCK, :HEAD_DIM].astype(BF16))
    o_ref[...] = jnp.concatenate(outs, axis=-1)


def _attention(proj, sinks, batch, seq, d_attn, d_kv, ride=()):
    n = proj.shape[0]
    nb = seq // BLOCK
    n_heads = d_attn // HEAD_DIM
    slopes = tuple(2.0 ** (-8.0 * (h + 1) / n_heads) for h in range(n_heads))
    kcol = d_attn // d_kv
    vcol = kcol + 1

    def cur(col):
        return lambda b, i: (b * nb + i, col)

    def prev(col):
        return lambda b, i: (b * nb + jnp.maximum(i - 1, 0), col)

    grid = (batch, nb)
    plans = _plan_riders(ride, grid)
    qx, kx = _attn_tables(slopes)
    in_specs = [
        pl.BlockSpec(memory_space=pltpu.SMEM),
        pl.BlockSpec(qx.shape, lambda b, i: (0, 0, 0)),
        pl.BlockSpec(kx.shape, lambda b, i: (0, 0)),
        pl.BlockSpec((BLOCK, d_attn), cur(0)),
        pl.BlockSpec((BLOCK, d_kv), prev(kcol)),
        pl.BlockSpec((BLOCK, d_kv), cur(kcol)),
        pl.BlockSpec((BLOCK, d_kv), prev(vcol)),
        pl.BlockSpec((BLOCK, d_kv), cur(vcol)),
    ]
    out_specs = [pl.BlockSpec((BLOCK, d_attn), cur(0))]
    out_shape = [jax.ShapeDtypeStruct((n, d_attn), BF16)]
    body, args = _add_riders(functools.partial(_attn_kernel, slopes), [sinks, qx, kx, proj, proj, proj, proj, proj],
                             in_specs, out_specs, out_shape, grid, ride, plans)
    outs = pl.pallas_call(
        body,
        grid=grid,
        in_specs=in_specs,
        out_specs=out_specs,
        out_shape=out_shape,
        compiler_params=_params(("parallel", "arbitrary"), 40),
        name="swa_attention",
    )(*args)
    return _ridden(outs, ride, plans)


def _pool_kernel(u_ref, w_ref, ps_ref, o_ref):
    g = pl.program_id(1)
    u = u_ref[...].astype(F32)
    row = lax.broadcasted_iota(I32, u.shape, 0)

    def shifted(x, k):
        return jnp.where(row >= k, pltpu.roll(x, k, 0), 0.0)

    s2 = u + shifted(u, 1)
    s4 = s2 + shifted(s2, 2)
    s8 = s4 + shifted(s4, 4)
    s16 = s8 + shifted(s8, 8)
    wsum = jnp.where(g == 0, s2, jnp.where(g == 1, s4, jnp.where(g == 2, s8, s16)))
    win = jnp.left_shift(jnp.int32(POOL_WINDOWS[0]), g)
    cnt = jnp.minimum(row + 1, win).astype(F32)
    pooled = wsum / cnt - u
    y = jnp.dot(pooled.astype(BF16), w_ref[0], preferred_element_type=F32)
    o_ref[...] = (y * ps_ref[...]).astype(BF16)


def _pool(proj, w_pool, pool_scale, batch, seq, u_col0):
    n = proj.shape[0]
    ng, c, _ = w_pool.shape
    assert POOL_WINDOWS == tuple(POOL_WINDOWS[0] << g for g in range(ng))
    col0 = u_col0 // c
    return pl.pallas_call(
        _pool_kernel,
        grid=(batch, ng),
        in_specs=[
            pl.BlockSpec((seq, c), lambda b, g: (b, col0 + g)),
            pl.BlockSpec((1, c, c), lambda b, g: (g, 0, 0)),
            pl.BlockSpec((1, c), lambda b, g: (0, g)),
        ],
        out_specs=pl.BlockSpec((seq, c), lambda b, g: (b, g)),
        out_shape=jax.ShapeDtypeStruct((n, ng * c), BF16),
        compiler_params=_params(("parallel", "arbitrary"), 48),
        name="pool_mixer",
    )(proj, w_pool, pool_scale.reshape(1, ng * c))


def _top2_route(logits):
    lane = lax.broadcasted_iota(I32, logits.shape, 1).astype(F32)
    m1 = jnp.max(logits, axis=-1, keepdims=True)
    i1 = jnp.min(jnp.where(logits == m1, lane, float(LANES)), axis=-1, keepdims=True)
    l2 = jnp.where(lane == i1, -jnp.inf, logits)
    m2 = jnp.max(l2, axis=-1, keepdims=True)
    i2 = jnp.min(jnp.where(l2 == m2, lane, float(LANES)), axis=-1, keepdims=True)
    e = jnp.exp(m2 - m1)
    w1 = 1.0 / (1.0 + e)
    w2 = e / (1.0 + e)
    return jnp.where(lane == 0.0, i1, jnp.where(lane == 1.0, i2, jnp.where(lane == 2.0, w1, jnp.where(lane == 3.0, w2, 0.0))))


def _post_kernel(mode, gathered, *refs):
    refs = list(refs)
    if gathered:
        pos_ref, pos_next_ref, y_hbm, route_in = refs[:4]
        refs = refs[4:]
    else:
        f_ref = refs.pop(0)
    x_ref, gate_ref, gpost_ref = refs[:3]
    refs = refs[3:]
    if mode != "last":
        gpre_ref, sh_ref, sc_ref = refs[:3]
        refs = refs[3:]
    if mode == "route":
        rw_ref, rb_ref = refs[:2]
        refs = refs[2:]
    xo_ref = refs.pop(0)
    if mode != "last":
        ho_ref = refs.pop(0)
    if mode == "route":
        ro_ref = refs.pop(0)

    if gathered:
        ybuf, sems = refs
        tr = x_ref.shape[0]
        i = pl.program_id(0)
        slot = i % 2

        def gather(idx_ref, s):
            def start(r, carry):
                for kk in range(2):
                    pltpu.make_async_copy(y_hbm.at[pl.ds(idx_ref[0, 0, 2 * r + kk], 1)],
                                          ybuf.at[s, kk, pl.ds(r, 1)], sems.at[s]).start()
                return carry
            lax.fori_loop(0, tr, start, 0, unroll=8)

        @pl.when(i == 0)
        def _():
            gather(pos_ref, 0)

        @pl.when(i + 1 < pl.num_programs(0))
        def _():
            gather(pos_next_ref, 1 - slot)

        pltpu.make_async_copy(ybuf.at[slot], ybuf.at[slot], sems.at[slot]).wait()
        rt = route_in[...]
        a_hi, a_lo = _unpack_pair_f32(ybuf[slot, 0])
        b_hi, b_lo = _unpack_pair_f32(ybuf[slot, 1])
        w_a, w_b = rt[:, 2:3], rt[:, 3:4]
        f = jnp.concatenate([a_hi * w_a + b_hi * w_b, a_lo * w_a + b_lo * w_b], axis=1)
    else:
        f = f_ref[...].astype(F32)

    xn = x_ref[...] + gate_ref[0] * _rms(f, gpost_ref[...])
    xo_ref[...] = xn
    if mode == "last":
        return
    h = _rms(xn, gpre_ref[...]) * (1.0 + sc_ref[0]) + sh_ref[0]
    if mode == "next":
        ho_ref[...] = h.astype(BF16)
        return
    half = h.shape[1] // 2
    ho_ref[...] = _pack_pair(h[:, :half], h[:, half:])
    logits = jnp.dot(h.astype(BF16), rw_ref[...], preferred_element_type=F32) + rb_ref[...]
    ro_ref[...] = _top2_route(logits)


def _post(mode, f, x2, gate, g_post, seq, nxt=None, router=None, gather=None):
    n, d = x2.shape
    tr = min(256, seq)
    per = seq // tr
    row = lambda i: (i, 0)
    fix = lambda i: (0, 0)
    bat = lambda i: (i // per, 0, 0)
    in_specs, args, scratch = [], [], []
    if gather is not None:
        pos, y, route = gather
        last = n // tr - 1
        pos3 = pos.reshape(n // tr, 1, 2 * tr)
        in_specs += [
            pl.BlockSpec((1, 1, 2 * tr), lambda i: (i, 0, 0), memory_space=pltpu.SMEM),
            pl.BlockSpec((1, 1, 2 * tr), lambda i: (jnp.minimum(i + 1, last), 0, 0), memory_space=pltpu.SMEM),
            pl.BlockSpec(memory_space=pl.ANY),
            pl.BlockSpec((tr, LANES), row),
        ]
        args += [pos3, pos3, y, route]
        scratch = [pltpu.VMEM((2, 2, tr, d // 2), U32), pltpu.SemaphoreType.DMA((2,))]
    else:
        in_specs.append(pl.BlockSpec((tr, d), row))
        args.append(f)
    in_specs += [pl.BlockSpec((tr, d), row), pl.BlockSpec((1, 1, d), bat), pl.BlockSpec((1, d), fix)]
    args += [x2, gate, g_post.reshape(1, d)]
    out_specs = [pl.BlockSpec((tr, d), row)]
    out_shape = [jax.ShapeDtypeStruct((n, d), F32)]
    if mode != "last":
        g_pre, shift, scale = nxt
        in_specs += [pl.BlockSpec((1, d), fix), pl.BlockSpec((1, 1, d), bat), pl.BlockSpec((1, 1, d), bat)]
        args += [g_pre.reshape(1, d), shift, scale]
    if mode == "next":
        out_specs.append(pl.BlockSpec((tr, d), row))
        out_shape.append(jax.ShapeDtypeStruct((n, d), BF16))
    if mode == "route":
        rw, rb = router
        in_specs += [pl.BlockSpec((d, LANES), fix), pl.BlockSpec((1, LANES), fix)]
        args += [rw, rb]
        out_specs += [pl.BlockSpec((tr, d // 2), row), pl.BlockSpec((tr, LANES), row)]
        out_shape += [jax.ShapeDtypeStruct((n, d // 2), U32), jax.ShapeDtypeStruct((n, LANES), F32)]
    return pl.pallas_call(
        functools.partial(_post_kernel, mode, gather is not None),
        grid=(n // tr,),
        in_specs=in_specs,
        out_specs=out_specs,
        out_shape=out_shape,
        scratch_shapes=scratch,
        compiler_params=_params(("arbitrary",), 56),
        name="post_" + mode + ("_gather" if gather is not None else ""),
    )(*args)


def _ffn_loop_kernel(dff, tf, n_ride, rb, h_ref, wg_hbm, wu_hbm, wd_hbm, *rest):
    src = rest[:n_ride]
    o_hbm = rest[n_ride]
    dst = rest[n_ride + 1:2 * n_ride + 1]
    acc, ostage, wgb, wub, wdb, sems, osem = rest[2 * n_ride + 1:2 * n_ride + 8]
    i = pl.program_id(0)
    n_full = dff // tf
    tail = dff - n_full * tf
    if n_ride:
        rin, rout, rsems = rest[2 * n_ride + 8:]
        per = src[0].shape[0] // rb
        n_blk = n_ride * per

        def rows_of(g, k):
            return pl.ds(pl.multiple_of((g - k * per) * rb, rb), rb)

        def ride_in(g, k):
            return pltpu.make_async_copy(src[k].at[rows_of(g, k)], rin.at[g % 2], rsems.at[0, g % 2])

        def ride_out(g, k):
            return pltpu.make_async_copy(rout.at[g % 2], dst[k].at[rows_of(g, k)], rsems.at[1, g % 2])

        def for_block(g, fn):
            for k in range(n_ride):
                @pl.when((g >= k * per) & (g < (k + 1) * per))
                def _():
                    fn(g, k)

    def copies(c, slot, width):
        start = pl.multiple_of(c * tf, LANES)
        return (pltpu.make_async_copy(wg_hbm.at[:, pl.ds(start, width)], wgb.at[slot, :, pl.ds(0, width)],
                                      sems.at[slot, 0]),
                pltpu.make_async_copy(wu_hbm.at[:, pl.ds(start, width)], wub.at[slot, :, pl.ds(0, width)],
                                      sems.at[slot, 1]),
                pltpu.make_async_copy(wd_hbm.at[pl.ds(start, width), :], wdb.at[slot, pl.ds(0, width), :],
                                      sems.at[slot, 2]))

    def start(c, slot, width=tf):
        for cp in copies(c, slot, width):
            cp.start(priority=WEIGHT_DMA_PRIORITY)

    def wait(c, slot, width=tf):
        for cp in copies(c, slot, width):
            cp.wait()

    def multiply(slot, width=tf):
        h = h_ref[...]
        g = jnp.dot(h, wgb[slot, :, :width], preferred_element_type=F32)
        u = jnp.dot(h, wub[slot, :, :width], preferred_element_type=F32)
        a = (g * _sigmoid(g) * u).astype(BF16)
        cw = acc.shape[1] // 4
        for q in range(4):
            acc[:, q * cw:(q + 1) * cw] += jnp.dot(a, wdb[slot, :width, q * cw:(q + 1) * cw],
                                                   preferred_element_type=F32)

    @pl.when(i == 0)
    def _():
        start(0, 0)
        if n_ride:
            ride_in(0, 0).start()

    acc[...] = jnp.zeros_like(acc)

    def step(c, carry):
        slot = c % 2
        wait(c, slot)
        if n_ride:
            g = i * (n_full - 1) + c
            for_block(g, lambda g, k: ride_in(g, k).wait())
            for_block(g + 1, lambda g, k: ride_in(g, k).start())
            for_block(g - 2, lambda g, k: ride_out(g, k).wait())
        start(c + 1, 1 - slot)
        multiply(slot)
        if n_ride:
            rout[g % 2] = rin[g % 2].astype(BF16)
            for_block(g, lambda g, k: ride_out(g, k).start())
        return carry

    lax.fori_loop(0, n_full - 1, step, 0)
    last, last_slot = n_full - 1, (n_full - 1) % 2
    wait(last, last_slot)
    if tail:
        start(n_full, 1 - last_slot, tail)
    else:
        @pl.when(i + 1 < pl.num_programs(0))
        def _():
            start(0, 1 - last_slot)
    multiply(last_slot)
    if tail:
        wait(n_full, 1 - last_slot, tail)

        @pl.when(i + 1 < pl.num_programs(0))
        def _():
            start(0, 0)
        multiply(1 - last_slot, tail)

    tm = acc.shape[0]

    def store(tile):
        return pltpu.make_async_copy(ostage, o_hbm.at[pl.ds(pl.multiple_of(tile * tm, tm), tm)], osem)

    @pl.when(i > 0)
    def _():
        store(i - 1).wait()

    ostage[...] = acc[...].astype(ostage.dtype)
    store(i).start()

    @pl.when(i + 1 == pl.num_programs(0))
    def _():
        store(i).wait()


def _ffn_loop(h, wg, wu, wd, tm, tf, ride=()):
    n, d = h.shape
    dff = wg.shape[1]
    tm = min(tm, n)
    n_tiles = n // tm
    n_full = dff // tf
    assert n_full >= 2 and (n_full + (1 if dff % tf else 0)) % 2 == 0 and (dff % tf) % LANES == 0
    ride_w = [w for w, _ in ride]
    rb = None
    if ride_w:
        rows, cols = ride_w[0].shape
        assert all(w.shape == (rows, cols) for w in ride_w)
        hosts = n_tiles * (n_full - 1) - 2
        for cand in range(BF16_SUBLANES, rows + 1, BF16_SUBLANES):
            if rows % cand == 0 and len(ride_w) * (rows // cand) <= hosts:
                rb = cand if cand * cols * 12 <= 2 * RIDER_BLOCK_BYTES else None
                break
    if rb is None:
        ride_w = []
    n_ride = len(ride_w)
    any_spec = pl.BlockSpec(memory_space=pl.ANY)
    scratch = [pltpu.VMEM((tm, d), F32), pltpu.VMEM((tm, d), BF16), pltpu.VMEM((2, d, tf), BF16),
               pltpu.VMEM((2, d, tf), BF16), pltpu.VMEM((2, tf, d), BF16), pltpu.SemaphoreType.DMA((2, 3)),
               pltpu.SemaphoreType.DMA(())]
    out_specs = [any_spec]
    out_shape = [jax.ShapeDtypeStruct((n, d), BF16)]
    if n_ride:
        scratch += [pltpu.VMEM((2, rb, cols), F32), pltpu.VMEM((2, rb, cols), BF16), pltpu.SemaphoreType.DMA((2, 2))]
        out_specs += [any_spec] * n_ride
        out_shape += [jax.ShapeDtypeStruct((rows, cols), BF16)] * n_ride
    outs = pl.pallas_call(
        functools.partial(_ffn_loop_kernel, dff, tf, n_ride, rb),
        grid=(n_tiles,),
        in_specs=[pl.BlockSpec((tm, d), lambda i: (i, 0))] + [any_spec] * (3 + n_ride),
        out_specs=out_specs,
        out_shape=out_shape,
        scratch_shapes=scratch,
        compiler_params=_params(("arbitrary",), 58),
        name="dense_swiglu",
    )(h, wg, wu, wd, *ride_w)
    casts = tuple(outs[1:]) if n_ride else tuple(_select(w, lead).astype(BF16) for w, lead in ride)
    return outs[0], casts


def _expert_kernel(te_ref, nt_ref, rows_ref, idx_ref, idx_next_ref, src_hbm, wg_hbm, wu_hbm, wd_hbm, y_ref,
                   xs_buf, h_scr, acc, wgb, wub, wdb, sem, wsems):
    t = pl.program_id(0)
    n_live = nt_ref[0]
    live = t < n_live
    tm, half = xs_buf.shape
    tf = wgb.shape[2]
    n_chunks = wg_hbm.shape[2] // tf
    top = tm // 2
    mostly_padding = rows_ref[t] <= top

    def gather(iref):
        def start(r, carry):
            pltpu.make_async_copy(src_hbm.at[pl.ds(iref[0, 0, r], 1)], xs_buf.at[pl.ds(r, 1)], sem).start()
            return carry
        lax.fori_loop(0, tm, start, 0, unroll=8)

    def copies(e, c, slot):
        start = pl.multiple_of(c * tf, LANES)
        return (pltpu.make_async_copy(wg_hbm.at[e, :, pl.ds(start, tf)], wgb.at[slot], wsems.at[slot, 0]),
                pltpu.make_async_copy(wu_hbm.at[e, :, pl.ds(start, tf)], wub.at[slot], wsems.at[slot, 1]),
                pltpu.make_async_copy(wd_hbm.at[e, pl.ds(start, tf), :], wdb.at[slot], wsems.at[slot, 2]))

    def start(e, c, slot):
        for cp in copies(e, c, slot):
            cp.start(priority=WEIGHT_DMA_PRIORITY)

    def wait(e, c, slot):
        for cp in copies(e, c, slot):
            cp.wait()

    def multiply(slot, rows):
        h = h_scr[:rows]
        g = jnp.dot(h, wgb[slot], preferred_element_type=F32)
        u = jnp.dot(h, wub[slot], preferred_element_type=F32)
        a = (g * _sigmoid(g) * u).astype(BF16)
        cw = acc.shape[1] // 4
        for q in range(4):
            acc[:rows, q * cw:(q + 1) * cw] += jnp.dot(a, wdb[slot, :, q * cw:(q + 1) * cw],
                                                       preferred_element_type=F32)

    @pl.when(t == 0)
    def _():
        gather(idx_ref)
        start(te_ref[0], 0, 0)

    @pl.when(live)
    def _():
        e = te_ref[t]
        pltpu.make_async_copy(xs_buf, xs_buf, sem).wait()
        hi, lo = _unpack_pair(xs_buf[...])
        h_scr[:, :half] = hi
        h_scr[:, half:] = lo
        acc[...] = jnp.zeros_like(acc)

        @pl.when(t + 1 < n_live)
        def _():
            gather(idx_next_ref)

        def run(rows):
            def step(c, carry):
                slot = c % 2
                wait(e, c, slot)
                start(e, c + 1, 1 - slot)
                multiply(slot, rows)
                return carry

            lax.fori_loop(0, n_chunks - 1, step, 0)
            last_slot = (n_chunks - 1) % 2
            wait(e, n_chunks - 1, last_slot)

            @pl.when(t + 1 < n_live)
            def _():
                start(te_ref[jnp.minimum(t + 1, pl.num_programs(0) - 1)], 0, 1 - last_slot)
            multiply(last_slot, rows)

        @pl.when(jnp.logical_not(mostly_padding))
        def _():
            run(tm)

        @pl.when(mostly_padding)
        def _():
            run(top)

        y_ref[...] = _pack_pair(acc[:, :half], acc[:, half:])

    @pl.when(jnp.logical_not(live))
    def _():
        y_ref[...] = jnp.zeros_like(y_ref)


def _experts(src, src_tok, tile_expert, n_tiles, tile_rows, wg, wu, wd, tm, tf):
    half = src.shape[1]
    d = 2 * half
    rows = src_tok.shape[0]
    dff = wg.shape[2]
    n_t = rows // tm
    idx3 = src_tok.reshape(n_t, 1, tm)
    assert dff % tf == 0 and (dff // tf) % 2 == 0
    return pl.pallas_call(
        _expert_kernel,
        grid_spec=pltpu.PrefetchScalarGridSpec(
            num_scalar_prefetch=3,
            grid=(n_t,),
            in_specs=[
                pl.BlockSpec((1, 1, tm), lambda t, te, nt, tr: (t, 0, 0), memory_space=pltpu.SMEM),
                pl.BlockSpec((1, 1, tm), lambda t, te, nt, tr: (jnp.minimum(t + 1, n_t - 1), 0, 0),
                             memory_space=pltpu.SMEM),
                pl.BlockSpec(memory_space=pl.ANY),
                pl.BlockSpec(memory_space=pl.ANY),
                pl.BlockSpec(memory_space=pl.ANY),
                pl.BlockSpec(memory_space=pl.ANY),
            ],
            out_specs=pl.BlockSpec((tm, half), lambda t, te, nt, tr: (t, 0)),
            scratch_shapes=[pltpu.VMEM((tm, half), U32), pltpu.VMEM((tm, d), BF16), pltpu.VMEM((tm, d), F32),
                            pltpu.VMEM((2, d, tf), BF16), pltpu.VMEM((2, d, tf), BF16), pltpu.VMEM((2, tf, d), BF16),
                            pltpu.SemaphoreType.DMA(()), pltpu.SemaphoreType.DMA((2, 3))],
        ),
        out_shape=jax.ShapeDtypeStruct((rows, half), U32),
        compiler_params=_params(("arbitrary",), 58),
        name="expert_swiglu",
    )(tile_expert, n_tiles, tile_rows, idx3, idx3, src, wg, wu, wd)


def _moe_plan(top_idx, tm):
    n = top_idx.shape[0]
    p = 2 * n
    flat = top_idx.reshape(p)
    experts = jnp.arange(N_EXPERTS, dtype=I32)
    onehot = (flat[:, None] == experts[None, :]).astype(I32)
    csum = jnp.cumsum(onehot, axis=0)
    rank = jnp.sum(onehot * csum, axis=1) - 1
    counts = csum[-1]
    tiles_per = (counts + (tm - 1)) // tm
    tile_end = jnp.cumsum(tiles_per)
    tile_start = tile_end - tiles_per
    pos = (tile_start * tm)[flat] + rank
    n_tiles = tile_end[-1:]
    t_max = p // tm + N_EXPERTS
    t_ids = jnp.arange(t_max, dtype=I32)
    te = jnp.sum((t_ids[:, None] >= tile_end[None, :]).astype(I32), axis=1)
    last_e = jnp.max(jnp.where(tiles_per > 0, experts, 0))
    te = jnp.minimum(te, last_e)
    tile_rows = jnp.clip(counts[te] - (t_ids - tile_start[te]) * tm, 0, tm)
    src_tok = jnp.zeros((t_max * tm,), I32).at[pos].set(jnp.arange(p, dtype=I32) // 2)
    return pos.astype(I32), src_tok, te.astype(I32), n_tiles.astype(I32), tile_rows.astype(I32)


def kernel(x, c, w_ada, b_ada, norm_pre_mix, norm_post_mix, norm_pre_ffn, norm_post_ffn, w_in, sinks, w_pool,
           pool_scale, w_out, ffn_w_gate, ffn_w_up, ffn_w_down, router_w, router_b, moe_w_gate, moe_w_up, moe_w_down):
    batch, seq, d = x.shape
    depth = w_ada.shape[0]
    n = batch * seq
    d_attn = d // 2
    n_heads = sinks.shape[1]
    d_kv = (n_heads // GQA_GROUP) * HEAD_DIM
    u_col0 = d_attn + 2 * d_kv
    assert n_heads * HEAD_DIM == d_attn and seq % BLOCK == 0

    x2 = x.reshape(n, d)
    mod = _ada(c, w_ada, b_ada)

    def mods(l):
        return [mod[l, :, k * d:(k + 1) * d].reshape(batch, 1, d) for k in range(N_MOD)]

    shift1, scale1, gate1, shift2, scale2, gate2 = mods(0)
    h = _prenorm(x2, norm_pre_mix[0], shift1, scale1, seq)
    flat = lambda w: w.reshape(-1, w.shape[-1])
    ready = {}

    def hosted(host, rides, *a):
        out, casts = host(*a, ride=[(w, lead) for _, w, lead in rides])
        ready.update({key: v for (key, _, _), v in zip(rides, casts)})
        return out

    def take(key, w, lead=None):
        shape = w.shape if lead is None else w.shape[1:]
        return ready.pop(key).reshape(shape) if key in ready else _select(w, lead).astype(BF16)

    for l in range(depth):
        i = l // 2
        dense = l % 2 == 0
        nm = (l + 1) // 2 if (dense and l + 1 < depth) else None
        on_proj_in, on_attn, on_proj_out, on_ffn = [(("w_out", l), w_out, l)], [], [], []
        if dense:
            on_proj_in += [(("ffn_wg", i), ffn_w_gate, i), (("ffn_wd", i), ffn_w_down, i)]
            on_proj_out.append((("ffn_wu", i), ffn_w_up, i))
            if nm is not None:
                on_attn.append((("moe_wg", nm), flat(moe_w_gate[nm]), None))
                on_ffn.append((("moe_wd", nm), flat(moe_w_down[nm]), None))
            if l + 1 < depth:
                on_proj_out.append((("w_in", l + 1), w_in, l + 1))
        else:
            on_attn.append((("moe_wu", i), flat(moe_w_up[i]), None))

        proj = hosted(_mm, on_proj_in, h, take(("w_in", l), w_in, l), BF16, PROJ_ROWS, PROJ_IN_COLS)
        attn = hosted(_attention, on_attn, proj, sinks[l], batch, seq, d_attn, d_kv)
        pool = _pool(proj, w_pool[l].astype(BF16), pool_scale[l], batch, seq, u_col0)
        mix = hosted(_mm2, on_proj_out, attn, pool, take(("w_out", l), w_out, l), BF16, PROJ_ROWS, PROJ_OUT_COLS)
        nxt = (norm_pre_ffn[l], shift2, scale2)
        if dense:
            x2, h2 = _post("next", mix, x2, gate1, norm_post_mix[l], seq, nxt=nxt)
            f = hosted(_ffn_loop, on_ffn, h2, take(("ffn_wg", i), ffn_w_gate, i), take(("ffn_wu", i), ffn_w_up, i),
                       take(("ffn_wd", i), ffn_w_down, i), FFN_ROWS, FFN_CHUNK)
            gather = None
        else:
            rw = jnp.zeros((d, LANES), BF16).at[:, :N_EXPERTS].set(router_w[i].astype(BF16))
            rb = jnp.full((1, LANES), NEG_INF, F32).at[0, :N_EXPERTS].set(router_b[i])
            x2, h2p, route = _post("route", mix, x2, gate1, norm_post_mix[l], seq, nxt=nxt, router=(rw, rb))
            pos, src_tok, tile_expert, n_tiles, tile_rows = _moe_plan(route[:, :2].astype(I32), FFN_ROWS)
            y = _experts(h2p, src_tok, tile_expert, n_tiles, tile_rows, take(("moe_wg", i), moe_w_gate[i]),
                         take(("moe_wu", i), moe_w_up[i]), take(("moe_wd", i), moe_w_down[i]), FFN_ROWS, FFN_CHUNK)
            f, gather = None, (pos, y, route)
        if l + 1 < depth:
            shift1, scale1, gate1n, shift2n, scale2n, gate2n = mods(l + 1)
            x2, h = _post("next", f, x2, gate2, norm_post_ffn[l], seq,
                          nxt=(norm_pre_mix[l + 1], shift1, scale1), gather=gather)
            gate1, shift2, scale2, gate2 = gate1n, shift2n, scale2n, gate2n
        else:
            (x2,) = _post("last", f, x2, gate2, norm_post_ffn[l], seq, gather=gather)
    return x2.reshape(batch, seq, d)
```

```python
import functools

import jax
import jax.numpy as jnp
import numpy as np
from jax import lax
from jax.experimental import pallas as pl
from jax.experimental.pallas import tpu as pltpu

F32 = jnp.float32
BF16 = jnp.bfloat16
U32 = jnp.uint32
I32 = jnp.int32

EPS = 1e-6
NEG_INF = -1e30
BLOCK = 128
HEAD_DIM = 64
GQA_GROUP = 8
POOL_WINDOWS = (2, 4, 8, 16)
N_EXPERTS = 8
N_MOD = 6
LANES = 128
PROJ_ROWS = 1024
PROJ_IN_COLS = 768
PROJ_OUT_COLS = 1024
FFN_ROWS = 512
FFN_CHUNK = 512
GATHER_CHUNK = 32
BF16_SUBLANES = 16
RIDER_BLOCK_BYTES = 4 << 20

MIB = 1 << 20


def _params(sem, vmem_mib):
    return pltpu.CompilerParams(dimension_semantics=sem, vmem_limit_bytes=vmem_mib * MIB)


def _sigmoid(x):
    return 1.0 / (1.0 + jnp.exp(-x))


def _rms(x, g):
    return x * lax.rsqrt(jnp.mean(x * x, axis=-1, keepdims=True) + EPS) * g


def _pack_pair(a, b):
    ua = pltpu.bitcast(a.astype(BF16).astype(F32), U32)
    ub = pltpu.bitcast(b.astype(BF16).astype(F32), U32)
    return ua | (ub >> 16)


def _unpack_pair_f32(p):
    return pltpu.bitcast(p & jnp.uint32(0xFFFF0000), F32), pltpu.bitcast(p << 16, F32)


def _unpack_pair(p):
    hi, lo = _unpack_pair_f32(p)
    return hi.astype(BF16), lo.astype(BF16)


def _select(w, lead):
    return w if lead is None else w[lead]


def _plan_riders(ride, grid):
    steps = grid[0] * grid[1]
    plans = []
    for w, _ in ride:
        rows, cols = w.shape[-2:]
        plan = None
        for rb in range(BF16_SUBLANES, rows + 1, BF16_SUBLANES):
            if rows % rb == 0 and rows // rb <= steps:
                plan = rb if rb * cols * 4 <= RIDER_BLOCK_BYTES else None
                break
        plans.append(plan)
    return plans


def _ridden(outs, ride, plans):
    extra = iter(outs[1:])
    return outs[0], tuple(next(extra) if rb is not None else _select(w, lead).astype(BF16)
                          for (w, lead), rb in zip(ride, plans))


def _add_riders(kernel_fn, args, in_specs, out_specs, out_shape, grid, ride, plans):
    riders = [(w, lead, rb) for (w, lead), rb in zip(ride, plans) if rb is not None]
    if not riders:
        return kernel_fn, list(args)
    n1 = grid[1]
    n_in, n_out, n_r = len(in_specs), len(out_specs), len(riders)
    for w, lead, rb in riders:
        rows, cols = w.shape[-2:]
        last = rows // rb - 1
        blk = functools.partial(lambda a, b, last: (jnp.minimum(a * n1 + b, last), 0), last=last)
        if lead is None:
            in_specs.append(pl.BlockSpec((rb, cols), blk))
        else:
            in_specs.append(pl.BlockSpec((None, rb, cols), functools.partial(
                lambda a, b, last, lead: (lead, jnp.minimum(a * n1 + b, last), 0), last=last, lead=lead)))
        out_specs.append(pl.BlockSpec((rb, cols), blk))
        out_shape.append(jax.ShapeDtypeStruct((rows, cols), BF16))

    def body(*refs):
        ins, srcs = refs[:n_in], refs[n_in:n_in + n_r]
        refs = refs[n_in + n_r:]
        outs, dsts = refs[:n_out], refs[n_out:n_out + n_r]
        kernel_fn(*ins, *outs, *refs[n_out + n_r:])
        for src, dst in zip(srcs, dsts):
            dst[...] = src[...].astype(BF16)

    return body, list(args) + [w for w, _, _ in riders]


def _ada_kernel(c_ref, w_ref, b_ref, o_ref):
    c = c_ref[...]
    ca = (c * _sigmoid(c)).astype(BF16)
    o_ref[0] = jnp.dot(ca, w_ref[0].astype(BF16), preferred_element_type=F32) + b_ref[0]


def _ada(c, w_ada, b_ada):
    depth, d, nm = w_ada.shape
    b = c.shape[0]
    tn = 512
    return pl.pallas_call(
        _ada_kernel,
        grid=(depth, nm // tn),
        in_specs=[
            pl.BlockSpec((b, d), lambda l, j: (0, 0)),
            pl.BlockSpec((1, d, tn), lambda l, j: (l, 0, j)),
            pl.BlockSpec((1, 1, tn), lambda l, j: (l, 0, j)),
        ],
        out_specs=pl.BlockSpec((1, b, tn), lambda l, j: (l, 0, j)),
        out_shape=jax.ShapeDtypeStruct((depth, b, nm), F32),
        compiler_params=_params(("parallel", "parallel"), 40),
        name="ada_mod",
    )(c, w_ada, b_ada.reshape(depth, 1, nm))


def _prenorm_kernel(x_ref, g_ref, sh_ref, sc_ref, o_ref):
    h = _rms(x_ref[...], g_ref[...]) * (1.0 + sc_ref[0]) + sh_ref[0]
    o_ref[...] = h.astype(BF16)


def _prenorm(x2, g, shift, scale, seq):
    n, d = x2.shape
    tr = min(512, seq)
    per = seq // tr
    return pl.pallas_call(
        _prenorm_kernel,
        grid=(n // tr,),
        in_specs=[
            pl.BlockSpec((tr, d), lambda i: (i, 0)),
            pl.BlockSpec((1, d), lambda i: (0, 0)),
            pl.BlockSpec((1, 1, d), lambda i: (i // per, 0, 0)),
            pl.BlockSpec((1, 1, d), lambda i: (i // per, 0, 0)),
        ],
        out_specs=pl.BlockSpec((tr, d), lambda i: (i, 0)),
        out_shape=jax.ShapeDtypeStruct((n, d), BF16),
        compiler_params=_params(("parallel",), 40),
        name="prenorm",
    )(x2, g.reshape(1, d), shift, scale)


def _mm_kernel(a_ref, w_ref, o_ref):
    o_ref[...] = jnp.dot(a_ref[...], w_ref[...], preferred_element_type=F32).astype(o_ref.dtype)


def _mm(a, w, out_dtype, tm, tn, ride=()):
    m, k = a.shape
    n = w.shape[1]
    tm = min(tm, m)
    grid = (m // tm, n // tn)
    plans = _plan_riders(ride, grid)
    in_specs = [
        pl.BlockSpec((tm, k), lambda i, j: (i, 0)),
        pl.BlockSpec((k, tn), lambda i, j: (0, j)),
    ]
    out_specs = [pl.BlockSpec((tm, tn), lambda i, j: (i, j))]
    out_shape = [jax.ShapeDtypeStruct((m, n), out_dtype)]
    body, args = _add_riders(_mm_kernel, [a, w], in_specs, out_specs, out_shape, grid, ride, plans)
    outs = pl.pallas_call(
        body,
        grid=grid,
        in_specs=in_specs,
        out_specs=out_specs,
        out_shape=out_shape,
        compiler_params=_params(("parallel", "arbitrary"), 56),
        name="proj_in",
    )(*args)
    return _ridden(outs, ride, plans)


def _mm2_kernel(a1_ref, a2_ref, w1_ref, w2_ref, o_ref):
    acc = jnp.dot(a1_ref[...], w1_ref[...], preferred_element_type=F32)
    acc = acc + jnp.dot(a2_ref[...], w2_ref[...], preferred_element_type=F32)
    o_ref[...] = acc.astype(o_ref.dtype)


def _mm2(a1, a2, w, out_dtype, tm, tn, ride=()):
    m, k1 = a1.shape
    n = w.shape[1]
    tm = min(tm, m)
    grid = (m // tm, n // tn)
    plans = _plan_riders(ride, grid)
    in_specs = [
        pl.BlockSpec((tm, k1), lambda i, j: (i, 0)),
        pl.BlockSpec((tm, k1), lambda i, j: (i, 0)),
        pl.BlockSpec((k1, tn), lambda i, j: (0, j)),
        pl.BlockSpec((k1, tn), lambda i, j: (1, j)),
    ]
    out_specs = [pl.BlockSpec((tm, tn), lambda i, j: (i, j))]
    out_shape = [jax.ShapeDtypeStruct((m, n), out_dtype)]
    body, args = _add_riders(_mm2_kernel, [a1, a2, w, w], in_specs, out_specs, out_shape, grid, ride, plans)
    outs = pl.pallas_call(
        body,
        grid=grid,
        in_specs=in_specs,
        out_specs=out_specs,
        out_shape=out_shape,
        compiler_params=_params(("parallel", "arbitrary"), 56),
        name="proj_out",
    )(*args)
    return _ridden(outs, ride, plans)


def _split3_const(x):
    parts = []
    r = np.float32(x)
    for _ in range(3):
        p = np.float32(np.asarray(r, dtype=jnp.bfloat16))
        parts.append(float(p))
        r = np.float32(r - p)
    return parts


def _swap_halves(x):
    half = x.shape[1] // 2
    return jnp.concatenate([x[:, half:], x[:, :half]], axis=1)


def _attn_tables(slopes):
    n_heads = len(slopes)
    qx = np.zeros((n_heads, BLOCK, LANES), np.float32)
    for h, s in enumerate(slopes):
        qx[h, :, HEAD_DIM:HEAD_DIM + 3] = _split3_const(s)
    kx = np.zeros((2 * BLOCK, LANES), np.float32)
    kx[:, HEAD_DIM:HEAD_DIM + 3] = np.arange(2 * BLOCK, dtype=np.float32)[:, None]
    return jnp.asarray(qx, BF16), jnp.asarray(kx, BF16)


def _attn_kernel(slopes, sinks_ref, qx_ref, kx_ref, q_ref, kp_ref, kc_ref, vp_ref, vc_ref, o_ref):
    n = pl.program_id(1)
    n_heads = len(slopes)
    k = jnp.concatenate([kp_ref[...], kc_ref[...]], axis=0)
    v = jnp.concatenate([vp_ref[...], vc_ref[...]], axis=0)
    c_idx = lax.broadcasted_iota(I32, (BLOCK, BLOCK), 0)
    i_idx = lax.broadcasted_iota(I32, (BLOCK, BLOCK), 1)
    from_prev = c_idx > i_idx
    prev_bias = jnp.where(n > 0, 0.0, NEG_INF)
    qpos = (lax.broadcasted_iota(I32, (1, BLOCK), 1) + BLOCK).astype(F32)
    qlane = lax.broadcasted_iota(I32, (BLOCK, LANES), 1) < HEAD_DIM
    klane = lax.broadcasted_iota(I32, (2 * BLOCK, LANES), 1) < HEAD_DIM
    scale = jnp.asarray(HEAD_DIM ** -0.5, BF16)
    outs = []
    for kv in range(n_heads // GQA_GROUP):
        heads = range(kv * GQA_GROUP, (kv + 1) * GQA_GROUP)
        kt = k[:, (kv // 2) * LANES:(kv // 2 + 1) * LANES]
        vt = v[:, (kv // 2) * LANES:(kv // 2 + 1) * LANES]
        if kv % 2:
            kt, vt = _swap_halves(kt), _swap_halves(vt)
        k_aug = jnp.where(klane, kt, kx_ref[...])
        pieces = []
        for h in heads:
            qt = q_ref[:, (h // 2) * LANES:(h // 2 + 1) * LANES]
            if h % 2:
                qt = _swap_halves(qt)
            pieces.append(jnp.where(qlane, qt * scale, qx_ref[h]))
        qg = jnp.concatenate(pieces, axis=0)
        st = lax.dot_general(k_aug, qg, (((1,), (1,)), ((), ())), preferred_element_type=F32)
        pts = []
        for g, h in enumerate(heads):
            s_prev = st[:BLOCK, g * BLOCK:(g + 1) * BLOCK] + prev_bias
            s_cur = st[BLOCK:, g * BLOCK:(g + 1) * BLOCK]
            s = jnp.where(from_prev, s_prev, s_cur)
            sink = sinks_ref[h] + np.float32(slopes[h]) * qpos
            m = jnp.maximum(jnp.max(s, axis=0, keepdims=True), sink)
            e = jnp.exp(s - m)
            denom = jnp.sum(e, axis=0, keepdims=True) + jnp.exp(sink - m)
            p = e * (1.0 / denom)
            pts.append(jnp.concatenate([jnp.where(from_prev, p, 0.0).astype(BF16),
                                        jnp.where(from_prev, 0.0, p).astype(BF16)], axis=0))
        pt = jnp.concatenate(pts, axis=1)
        og = lax.dot_general(pt, vt, (((0,), (0,)), ((), ())), preferred_element_type=F32)
        for g in range(GQA_GROUP):
            outs.append(og[g * BLOCK:(g + 1) * BLOCK, :HEAD_DIM].astype(BF16))
    o_ref[...] = jnp.concatenate(outs, axis=-1)


def _attention(proj, sinks, batch, seq, d_attn, d_kv, ride=()):
    n = proj.shape[0]
    nb = seq // BLOCK
    n_heads = d_attn // HEAD_DIM
    slopes = tuple(2.0 ** (-8.0 * (h + 1) / n_heads) for h in range(n_heads))
    kcol = d_attn // d_kv
    vcol = kcol + 1

    def cur(col):
        return lambda b, i: (b * nb + i, col)

    def prev(col):
        return lambda b, i: (b * nb + jnp.maximum(i - 1, 0), col)

    grid = (batch, nb)
    plans = _plan_riders(ride, grid)
    qx, kx = _attn_tables(slopes)
    in_specs = [
        pl.BlockSpec(memory_space=pltpu.SMEM),
        pl.BlockSpec(qx.shape, lambda b, i: (0, 0, 0)),
        pl.BlockSpec(kx.shape, lambda b, i: (0, 0)),
        pl.BlockSpec((BLOCK, d_attn), cur(0)),
        pl.BlockSpec((BLOCK, d_kv), prev(kcol)),
        pl.BlockSpec((BLOCK, d_kv), cur(kcol)),
        pl.BlockSpec((BLOCK, d_kv), prev(vcol)),
        pl.BlockSpec((BLOCK, d_kv), cur(vcol)),
    ]
    out_specs = [pl.BlockSpec((BLOCK, d_attn), cur(0))]
    out_shape = [jax.ShapeDtypeStruct((n, d_attn), BF16)]
    body, args = _add_riders(functools.partial(_attn_kernel, slopes), [sinks, qx, kx, proj, proj, proj, proj, proj],
                             in_specs, out_specs, out_shape, grid, ride, plans)
    outs = pl.pallas_call(
        body,
        grid=grid,
        in_specs=in_specs,
        out_specs=out_specs,
        out_shape=out_shape,
        compiler_params=_params(("parallel", "arbitrary"), 40),
        name="swa_attention",
    )(*args)
    return _ridden(outs, ride, plans)


def _pool_kernel(u_ref, w_ref, ps_ref, o_ref):
    g = pl.program_id(1)
    u = u_ref[...].astype(F32)
    row = lax.broadcasted_iota(I32, u.shape, 0)

    def shifted(x, k):
        return jnp.where(row >= k, pltpu.roll(x, k, 0), 0.0)

    s2 = u + shifted(u, 1)
    s4 = s2 + shifted(s2, 2)
    s8 = s4 + shifted(s4, 4)
    s16 = s8 + shifted(s8, 8)
    wsum = jnp.where(g == 0, s2, jnp.where(g == 1, s4, jnp.where(g == 2, s8, s16)))
    win = jnp.left_shift(jnp.int32(POOL_WINDOWS[0]), g)
    cnt = jnp.minimum(row + 1, win).astype(F32)
    pooled = wsum / cnt - u
    y = jnp.dot(pooled.astype(BF16), w_ref[0], preferred_element_type=F32)
    o_ref[...] = (y * ps_ref[...]).astype(BF16)


def _pool(proj, w_pool, pool_scale, batch, seq, u_col0):
    n = proj.shape[0]
    ng, c, _ = w_pool.shape
    assert POOL_WINDOWS == tuple(POOL_WINDOWS[0] << g for g in range(ng))
    col0 = u_col0 // c
    return pl.pallas_call(
        _pool_kernel,
        grid=(batch, ng),
        in_specs=[
            pl.BlockSpec((seq, c), lambda b, g: (b, col0 + g)),
            pl.BlockSpec((1, c, c), lambda b, g: (g, 0, 0)),
            pl.BlockSpec((1, c), lambda b, g: (0, g)),
        ],
        out_specs=pl.BlockSpec((seq, c), lambda b, g: (b, g)),
        out_shape=jax.ShapeDtypeStruct((n, ng * c), BF16),
        compiler_params=_params(("parallel", "arbitrary"), 48),
        name="pool_mixer",
    )(proj, w_pool, pool_scale.reshape(1, ng * c))


def _top2_route(logits):
    lane = lax.broadcasted_iota(I32, logits.shape, 1).astype(F32)
    m1 = jnp.max(logits, axis=-1, keepdims=True)
    i1 = jnp.min(jnp.where(logits == m1, lane, float(LANES)), axis=-1, keepdims=True)
    l2 = jnp.where(lane == i1, -jnp.inf, logits)
    m2 = jnp.max(l2, axis=-1, keepdims=True)
    i2 = jnp.min(jnp.where(l2 == m2, lane, float(LANES)), axis=-1, keepdims=True)
    e = jnp.exp(m2 - m1)
    w1 = 1.0 / (1.0 + e)
    w2 = e / (1.0 + e)
    return jnp.where(lane == 0.0, i1, jnp.where(lane == 1.0, i2, jnp.where(lane == 2.0, w1, jnp.where(lane == 3.0, w2, 0.0))))


def _post_kernel(mode, gathered, *refs):
    refs = list(refs)
    if gathered:
        pos_ref, pos_next_ref, y_hbm, route_in = refs[:4]
        refs = refs[4:]
    else:
        f_ref = refs.pop(0)
    x_ref, gate_ref, gpost_ref = refs[:3]
    refs = refs[3:]
    if mode != "last":
        gpre_ref, sh_ref, sc_ref = refs[:3]
        refs = refs[3:]
    if mode == "route":
        rw_ref, rb_ref = refs[:2]
        refs = refs[2:]
    xo_ref = refs.pop(0)
    if mode != "last":
        ho_ref = refs.pop(0)
    if mode == "route":
        ro_ref = refs.pop(0)

    def finish(f, rows):
        xn = x_ref[rows, :] + gate_ref[0] * _rms(f, gpost_ref[...])
        xo_ref[rows, :] = xn
        if mode == "last":
            return
        h = _rms(xn, gpre_ref[...]) * (1.0 + sc_ref[0]) + sh_ref[0]
        if mode == "next":
            ho_ref[rows, :] = h.astype(BF16)
            return
        half = h.shape[1] // 2
        ho_ref[rows, :] = _pack_pair(h[:, :half], h[:, half:])
        logits = jnp.dot(h.astype(BF16), rw_ref[...], preferred_element_type=F32) + rb_ref[...]
        ro_ref[rows, :] = _top2_route(logits)

    if not gathered:
        finish(f_ref[...].astype(F32), slice(None))
        return

    assert mode != "route"
    ybuf, sems = refs
    tr = x_ref.shape[0]
    i = pl.program_id(0)
    slot = i % 2

    def fetch(idx_ref, s, r):
        for kk in range(2):
            pltpu.make_async_copy(y_hbm.at[pl.ds(idx_ref[0, 0, 2 * r + kk], 1)],
                                  ybuf.at[s, kk, pl.ds(r, 1)], sems.at[s]).start()

    @pl.when(i == 0)
    def _():
        lax.fori_loop(0, tr, lambda r, c: (fetch(pos_ref, 0, r), c)[1], 0, unroll=8)

    pltpu.make_async_copy(ybuf.at[slot], ybuf.at[slot], sems.at[slot]).wait()
    for r0 in range(0, tr, GATHER_CHUNK):
        for r in range(r0, r0 + GATHER_CHUNK):
            fetch(pos_next_ref, 1 - slot, r)
        rows = slice(r0, r0 + GATHER_CHUNK)
        rt = route_in[rows, :]
        a_hi, a_lo = _unpack_pair_f32(ybuf[slot, 0, rows, :])
        b_hi, b_lo = _unpack_pair_f32(ybuf[slot, 1, rows, :])
        w_a, w_b = rt[:, 2:3], rt[:, 3:4]
        finish(jnp.concatenate([a_hi * w_a + b_hi * w_b, a_lo * w_a + b_lo * w_b], axis=1), rows)

    @pl.when(i + 1 == pl.num_programs(0))
    def _():
        pltpu.make_async_copy(ybuf.at[1 - slot], ybuf.at[1 - slot], sems.at[1 - slot]).wait()


def _post(mode, f, x2, gate, g_post, seq, nxt=None, router=None, gather=None):
    n, d = x2.shape
    tr = min(256, seq)
    per = seq // tr
    row = lambda i: (i, 0)
    fix = lambda i: (0, 0)
    bat = lambda i: (i // per, 0, 0)
    in_specs, args, scratch = [], [], []
    if gather is not None:
        pos, y, route = gather
        last = n // tr - 1
        pos3 = pos.reshape(n // tr, 1, 2 * tr)
        in_specs += [
            pl.BlockSpec((1, 1, 2 * tr), lambda i: (i, 0, 0), memory_space=pltpu.SMEM),
            pl.BlockSpec((1, 1, 2 * tr), lambda i: (jnp.minimum(i + 1, last), 0, 0), memory_space=pltpu.SMEM),
            pl.BlockSpec(memory_space=pl.ANY),
            pl.BlockSpec((tr, LANES), row),
        ]
        args += [pos3, pos3, y, route]
        scratch = [pltpu.VMEM((2, 2, tr, d // 2), U32), pltpu.SemaphoreType.DMA((2,))]
    else:
        in_specs.append(pl.BlockSpec((tr, d), row))
        args.append(f)
    in_specs += [pl.BlockSpec((tr, d), row), pl.BlockSpec((1, 1, d), bat), pl.BlockSpec((1, d), fix)]
    args += [x2, gate, g_post.reshape(1, d)]
    out_specs = [pl.BlockSpec((tr, d), row)]
    out_shape = [jax.ShapeDtypeStruct((n, d), F32)]
    if mode != "last":
        g_pre, shift, scale = nxt
        in_specs += [pl.BlockSpec((1, d), fix), pl.BlockSpec((1, 1, d), bat), pl.BlockSpec((1, 1, d), bat)]
        args += [g_pre.reshape(1, d), shift, scale]
    if mode == "next":
        out_specs.append(pl.BlockSpec((tr, d), row))
        out_shape.append(jax.ShapeDtypeStruct((n, d), BF16))
    if mode == "route":
        rw, rb = router
        in_specs += [pl.BlockSpec((d, LANES), fix), pl.BlockSpec((1, LANES), fix)]
        args += [rw, rb]
        out_specs += [pl.BlockSpec((tr, d // 2), row), pl.BlockSpec((tr, LANES), row)]
        out_shape += [jax.ShapeDtypeStruct((n, d // 2), U32), jax.ShapeDtypeStruct((n, LANES), F32)]
    return pl.pallas_call(
        functools.partial(_post_kernel, mode, gather is not None),
        grid=(n // tr,),
        in_specs=in_specs,
        out_specs=out_specs,
        out_shape=out_shape,
        scratch_shapes=scratch,
        compiler_params=_params(("arbitrary",), 56),
        name="post_" + mode + ("_gather" if gather is not None else ""),
    )(*args)


def _ffn_loop_kernel(dff, tf, n_ride, rb, h_ref, wg_hbm, wu_hbm, wd_hbm, *rest):
    src = rest[:n_ride]
    o_hbm = rest[n_ride]
    dst = rest[n_ride + 1:2 * n_ride + 1]
    acc, ostage, wgb, wub, wdb, sems, osem = rest[2 * n_ride + 1:2 * n_ride + 8]
    i = pl.program_id(0)
    n_full = dff // tf
    tail = dff - n_full * tf
    if n_ride:
        rin, rout, rsems = rest[2 * n_ride + 8:]
        per = src[0].shape[0] // rb
        n_blk = n_ride * per

        def rows_of(g, k):
            return pl.ds(pl.multiple_of((g - k * per) * rb, rb), rb)

        def ride_in(g, k):
            return pltpu.make_async_copy(src[k].at[rows_of(g, k)], rin.at[g % 2], rsems.at[0, g % 2])

        def ride_out(g, k):
            return pltpu.make_async_copy(rout.at[g % 2], dst[k].at[rows_of(g, k)], rsems.at[1, g % 2])

        def for_block(g, fn):
            for k in range(n_ride):
                @pl.when((g >= k * per) & (g < (k + 1) * per))
                def _():
                    fn(g, k)

    def copies(c, slot, width):
        start = pl.multiple_of(c * tf, LANES)
        return (pltpu.make_async_copy(wg_hbm.at[:, pl.ds(start, width)], wgb.at[slot, :, pl.ds(0, width)],
                                      sems.at[slot, 0]),
                pltpu.make_async_copy(wu_hbm.at[:, pl.ds(start, width)], wub.at[slot, :, pl.ds(0, width)],
                                      sems.at[slot, 1]),
                pltpu.make_async_copy(wd_hbm.at[pl.ds(start, width), :], wdb.at[slot, pl.ds(0, width), :],
                                      sems.at[slot, 2]))

    def start(c, slot, width=tf):
        for cp in copies(c, slot, width):
            cp.start()

    def wait(c, slot, width=tf):
        for cp in copies(c, slot, width):
            cp.wait()

    def multiply(slot, width=tf):
        h = h_ref[...]
        g = jnp.dot(h, wgb[slot, :, :width], preferred_element_type=F32)
        u = jnp.dot(h, wub[slot, :, :width], preferred_element_type=F32)
        a = (g * _sigmoid(g) * u).astype(BF16)
        cw = acc.shape[1] // 4
        for q in range(4):
            acc[:, q * cw:(q + 1) * cw] += jnp.dot(a, wdb[slot, :width, q * cw:(q + 1) * cw],
                                                   preferred_element_type=F32)

    @pl.when(i == 0)
    def _():
        start(0, 0)
        if n_ride:
            ride_in(0, 0).start()

    acc[...] = jnp.zeros_like(acc)

    def step(c, carry):
        slot = c % 2
        wait(c, slot)
        if n_ride:
            g = i * (n_full - 1) + c
            for_block(g, lambda g, k: ride_in(g, k).wait())
            for_block(g + 1, lambda g, k: ride_in(g, k).start())
            for_block(g - 2, lambda g, k: ride_out(g, k).wait())
        start(c + 1, 1 - slot)
        multiply(slot)
        if n_ride:
            rout[g % 2] = rin[g % 2].astype(BF16)
            for_block(g, lambda g, k: ride_out(g, k).start())
        return carry

    lax.fori_loop(0, n_full - 1, step, 0)
    last, last_slot = n_full - 1, (n_full - 1) % 2
    wait(last, last_slot)
    if tail:
        start(n_full, 1 - last_slot, tail)
    else:
        @pl.when(i + 1 < pl.num_programs(0))
        def _():
            start(0, 1 - last_slot)
    multiply(last_slot)
    if tail:
        wait(n_full, 1 - last_slot, tail)

        @pl.when(i + 1 < pl.num_programs(0))
        def _():
            start(0, 0)
        multiply(1 - last_slot, tail)

    tm = acc.shape[0]

    def store(tile):
        return pltpu.make_async_copy(ostage, o_hbm.at[pl.ds(pl.multiple_of(tile * tm, tm), tm)], osem)

    @pl.when(i > 0)
    def _():
        store(i - 1).wait()

    ostage[...] = acc[...].astype(ostage.dtype)
    store(i).start()

    @pl.when(i + 1 == pl.num_programs(0))
    def _():
        store(i).wait()


def _ffn_loop(h, wg, wu, wd, tm, tf, ride=()):
    n, d = h.shape
    dff = wg.shape[1]
    tm = min(tm, n)
    n_tiles = n // tm
    n_full = dff // tf
    assert n_full >= 2 and (n_full + (1 if dff % tf else 0)) % 2 == 0 and (dff % tf) % LANES == 0
    ride_w = [w for w, _ in ride]
    rb = None
    if ride_w:
        rows, cols = ride_w[0].shape
        assert all(w.shape == (rows, cols) for w in ride_w)
        hosts = n_tiles * (n_full - 1) - 2
        for cand in range(BF16_SUBLANES, rows + 1, BF16_SUBLANES):
            if rows % cand == 0 and len(ride_w) * (rows // cand) <= hosts:
                rb = cand if cand * cols * 12 <= 2 * RIDER_BLOCK_BYTES else None
                break
    if rb is None:
        ride_w = []
    n_ride = len(ride_w)
    any_spec = pl.BlockSpec(memory_space=pl.ANY)
    scratch = [pltpu.VMEM((tm, d), F32), pltpu.VMEM((tm, d), BF16), pltpu.VMEM((2, d, tf), BF16),
               pltpu.VMEM((2, d, tf), BF16), pltpu.VMEM((2, tf, d), BF16), pltpu.SemaphoreType.DMA((2, 3)),
               pltpu.SemaphoreType.DMA(())]
    out_specs = [any_spec]
    out_shape = [jax.ShapeDtypeStruct((n, d), BF16)]
    if n_ride:
        scratch += [pltpu.VMEM((2, rb, cols), F32), pltpu.VMEM((2, rb, cols), BF16), pltpu.SemaphoreType.DMA((2, 2))]
        out_specs += [any_spec] * n_ride
        out_shape += [jax.ShapeDtypeStruct((rows, cols), BF16)] * n_ride
    outs = pl.pallas_call(
        functools.partial(_ffn_loop_kernel, dff, tf, n_ride, rb),
        grid=(n_tiles,),
        in_specs=[pl.BlockSpec((tm, d), lambda i: (i, 0))] + [any_spec] * (3 + n_ride),
        out_specs=out_specs,
        out_shape=out_shape,
        scratch_shapes=scratch,
        compiler_params=_params(("arbitrary",), 58),
        name="dense_swiglu",
    )(h, wg, wu, wd, *ride_w)
    casts = tuple(outs[1:]) if n_ride else tuple(_select(w, lead).astype(BF16) for w, lead in ride)
    return outs[0], casts


def _expert_kernel(te_ref, nt_ref, rows_ref, idx_ref, idx_next_ref, src_hbm, wg_hbm, wu_hbm, wd_hbm, y_ref,
                   xs_buf, h_scr, acc, wgb, wub, wdb, sem, wsems):
    t = pl.program_id(0)
    n_live = nt_ref[0]
    live = t < n_live
    tm, half = xs_buf.shape
    tf = wgb.shape[2]
    n_chunks = wg_hbm.shape[2] // tf
    top = tm // 2
    mostly_padding = rows_ref[t] <= top

    def gather(iref):
        def start(r, carry):
            pltpu.make_async_copy(src_hbm.at[pl.ds(iref[0, 0, r], 1)], xs_buf.at[pl.ds(r, 1)], sem).start()
            return carry
        lax.fori_loop(0, tm, start, 0, unroll=8)

    def copies(e, c, slot):
        start = pl.multiple_of(c * tf, LANES)
        return (pltpu.make_async_copy(wg_hbm.at[e, :, pl.ds(start, tf)], wgb.at[slot], wsems.at[slot, 0]),
                pltpu.make_async_copy(wu_hbm.at[e, :, pl.ds(start, tf)], wub.at[slot], wsems.at[slot, 1]),
                pltpu.make_async_copy(wd_hbm.at[e, pl.ds(start, tf), :], wdb.at[slot], wsems.at[slot, 2]))

    def start(e, c, slot):
        for cp in copies(e, c, slot):
            cp.start()

    def wait(e, c, slot):
        for cp in copies(e, c, slot):
            cp.wait()

    def multiply(slot, rows):
        h = h_scr[:rows]
        g = jnp.dot(h, wgb[slot], preferred_element_type=F32)
        u = jnp.dot(h, wub[slot], preferred_element_type=F32)
        a = (g * _sigmoid(g) * u).astype(BF16)
        cw = acc.shape[1] // 4
        for q in range(4):
            acc[:rows, q * cw:(q + 1) * cw] += jnp.dot(a, wdb[slot, :, q * cw:(q + 1) * cw],
                                                       preferred_element_type=F32)

    @pl.when(t == 0)
    def _():
        gather(idx_ref)
        start(te_ref[0], 0, 0)

    @pl.when(live)
    def _():
        e = te_ref[t]
        pltpu.make_async_copy(xs_buf, xs_buf, sem).wait()
        hi, lo = _unpack_pair(xs_buf[...])
        h_scr[:, :half] = hi
        h_scr[:, half:] = lo
        acc[...] = jnp.zeros_like(acc)

        @pl.when(t + 1 < n_live)
        def _():
            gather(idx_next_ref)

        def run(rows):
            def step(c, carry):
                slot = c % 2
                wait(e, c, slot)
                start(e, c + 1, 1 - slot)
                multiply(slot, rows)
                return carry

            lax.fori_loop(0, n_chunks - 1, step, 0)
            last_slot = (n_chunks - 1) % 2
            wait(e, n_chunks - 1, last_slot)

            @pl.when(t + 1 < n_live)
            def _():
                start(te_ref[jnp.minimum(t + 1, pl.num_programs(0) - 1)], 0, 1 - last_slot)
            multiply(last_slot, rows)

        @pl.when(jnp.logical_not(mostly_padding))
        def _():
            run(tm)

        @pl.when(mostly_padding)
        def _():
            run(top)

        y_ref[...] = _pack_pair(acc[:, :half], acc[:, half:])

    @pl.when(jnp.logical_not(live))
    def _():
        y_ref[...] = jnp.zeros_like(y_ref)


def _experts(src, src_tok, tile_expert, n_tiles, tile_rows, wg, wu, wd, tm, tf):
    half = src.shape[1]
    d = 2 * half
    rows = src_tok.shape[0]
    dff = wg.shape[2]
    n_t = rows // tm
    idx3 = src_tok.reshape(n_t, 1, tm)
    assert dff % tf == 0 and (dff // tf) % 2 == 0
    return pl.pallas_call(
        _expert_kernel,
        grid_spec=pltpu.PrefetchScalarGridSpec(
            num_scalar_prefetch=3,
            grid=(n_t,),
            in_specs=[
                pl.BlockSpec((1, 1, tm), lambda t, te, nt, tr: (t, 0, 0), memory_space=pltpu.SMEM),
                pl.BlockSpec((1, 1, tm), lambda t, te, nt, tr: (jnp.minimum(t + 1, n_t - 1), 0, 0),
                             memory_space=pltpu.SMEM),
                pl.BlockSpec(memory_space=pl.ANY),
                pl.BlockSpec(memory_space=pl.ANY),
                pl.BlockSpec(memory_space=pl.ANY),
                pl.BlockSpec(memory_space=pl.ANY),
            ],
            out_specs=pl.BlockSpec((tm, half), lambda t, te, nt, tr: (t, 0)),
            scratch_shapes=[pltpu.VMEM((tm, half), U32), pltpu.VMEM((tm, d), BF16), pltpu.VMEM((tm, d), F32),
                            pltpu.VMEM((2, d, tf), BF16), pltpu.VMEM((2, d, tf), BF16), pltpu.VMEM((2, tf, d), BF16),
                            pltpu.SemaphoreType.DMA(()), pltpu.SemaphoreType.DMA((2, 3))],
        ),
        out_shape=jax.ShapeDtypeStruct((rows, half), U32),
        compiler_params=_params(("arbitrary",), 58),
        name="expert_swiglu",
    )(tile_expert, n_tiles, tile_rows, idx3, idx3, src, wg, wu, wd)


def _moe_plan(top_idx, tm):
    n = top_idx.shape[0]
    p = 2 * n
    flat = top_idx.reshape(p)
    experts = jnp.arange(N_EXPERTS, dtype=I32)
    onehot = (flat[:, None] == experts[None, :]).astype(I32)
    csum = jnp.cumsum(onehot, axis=0)
    rank = jnp.sum(onehot * csum, axis=1) - 1
    counts = csum[-1]
    tiles_per = (counts + (tm - 1)) // tm
    tile_end = jnp.cumsum(tiles_per)
    tile_start = tile_end - tiles_per
    pos = (tile_start * tm)[flat] + rank
    n_tiles = tile_end[-1:]
    t_max = p // tm + N_EXPERTS
    t_ids = jnp.arange(t_max, dtype=I32)
    te = jnp.sum((t_ids[:, None] >= tile_end[None, :]).astype(I32), axis=1)
    last_e = jnp.max(jnp.where(tiles_per > 0, experts, 0))
    te = jnp.minimum(te, last_e)
    tile_rows = jnp.clip(counts[te] - (t_ids - tile_start[te]) * tm, 0, tm)
    src_tok = jnp.zeros((t_max * tm,), I32).at[pos].set(jnp.arange(p, dtype=I32) // 2)
    return pos.astype(I32), src_tok, te.astype(I32), n_tiles.astype(I32), tile_rows.astype(I32)


def kernel(x, c, w_ada, b_ada, norm_pre_mix, norm_post_mix, norm_pre_ffn, norm_post_ffn, w_in, sinks, w_pool,
           pool_scale, w_out, ffn_w_gate, ffn_w_up, ffn_w_down, router_w, router_b, moe_w_gate, moe_w_up, moe_w_down):
    batch, seq, d = x.shape
    depth = w_ada.shape[0]
    n = batch * seq
    d_attn = d // 2
    n_heads = sinks.shape[1]
    d_kv = (n_heads // GQA_GROUP) * HEAD_DIM
    u_col0 = d_attn + 2 * d_kv
    assert n_heads * HEAD_DIM == d_attn and seq % BLOCK == 0

    x2 = x.reshape(n, d)
    mod = _ada(c, w_ada, b_ada)

    def mods(l):
        return [mod[l, :, k * d:(k + 1) * d].reshape(batch, 1, d) for k in range(N_MOD)]

    shift1, scale1, gate1, shift2, scale2, gate2 = mods(0)
    h = _prenorm(x2, norm_pre_mix[0], shift1, scale1, seq)
    flat = lambda w: w.reshape(-1, w.shape[-1])
    ready = {}

    def hosted(host, rides, *a):
        out, casts = host(*a, ride=[(w, lead) for _, w, lead in rides])
        ready.update({key: v for (key, _, _), v in zip(rides, casts)})
        return out

    def take(key, w, lead=None):
        shape = w.shape if lead is None else w.shape[1:]
        return ready.pop(key).reshape(shape) if key in ready else _select(w, lead).astype(BF16)

    for l in range(depth):
        i = l // 2
        dense = l % 2 == 0
        nm = (l + 1) // 2 if (dense and l + 1 < depth) else None
        on_proj_in, on_attn, on_proj_out, on_ffn = [(("w_out", l), w_out, l)], [], [], []
        if dense:
            on_proj_in += [(("ffn_wg", i), ffn_w_gate, i), (("ffn_wd", i), ffn_w_down, i)]
            on_proj_out.append((("ffn_wu", i), ffn_w_up, i))
            if nm is not None:
                on_attn.append((("moe_wg", nm), flat(moe_w_gate[nm]), None))
                on_ffn.append((("moe_wd", nm), flat(moe_w_down[nm]), None))
            if l + 1 < depth:
                on_proj_out.append((("w_in", l + 1), w_in, l + 1))
        else:
            on_attn.append((("moe_wu", i), flat(moe_w_up[i]), None))

        proj = hosted(_mm, on_proj_in, h, take(("w_in", l), w_in, l), BF16, PROJ_ROWS, PROJ_IN_COLS)
        attn = hosted(_attention, on_attn, proj, sinks[l], batch, seq, d_attn, d_kv)
        pool = _pool(proj, w_pool[l].astype(BF16), pool_scale[l], batch, seq, u_col0)
        mix = hosted(_mm2, on_proj_out, attn, pool, take(("w_out", l), w_out, l), BF16, PROJ_ROWS, PROJ_OUT_COLS)
        nxt = (norm_pre_ffn[l], shift2, scale2)
        if dense:
            x2, h2 = _post("next", mix, x2, gate1, norm_post_mix[l], seq, nxt=nxt)
            f = hosted(_ffn_loop, on_ffn, h2, take(("ffn_wg", i), ffn_w_gate, i), take(("ffn_wu", i), ffn_w_up, i),
                       take(("ffn_wd", i), ffn_w_down, i), FFN_ROWS, FFN_CHUNK)
            gather = None
        else:
            rw = jnp.zeros((d, LANES), BF16).at[:, :N_EXPERTS].set(router_w[i].astype(BF16))
            rb = jnp.full((1, LANES), NEG_INF, F32).at[0, :N_EXPERTS].set(router_b[i])
            x2, h2p, route = _post("route", mix, x2, gate1, norm_post_mix[l], seq, nxt=nxt, router=(rw, rb))
            pos, src_tok, tile_expert, n_tiles, tile_rows = _moe_plan(route[:, :2].astype(I32), FFN_ROWS)
            y = _experts(h2p, src_tok, tile_expert, n_tiles, tile_rows, take(("moe_wg", i), moe_w_gate[i]),
                         take(("moe_wu", i), moe_w_up[i]), take(("moe_wd", i), moe_w_down[i]), FFN_ROWS, FFN_CHUNK)
            f, gather = None, (pos, y, route)
        if l + 1 < depth:
            shift1, scale1, gate1n, shift2n, scale2n, gate2n = mods(l + 1)
            x2, h = _post("next", f, x2, gate2, norm_post_ffn[l], seq,
                          nxt=(norm_pre_mix[l + 1], shift1, scale1), gather=gather)
            gate1, shift2, scale2, gate2 = gate1n, shift2n, scale2n, gate2n
        else:
            (x2,) = _post("last", f, x2, gate2, norm_post_ffn[l], seq, gather=gather)
    return x2.reshape(batch, seq, d)
```

```python
import functools

import jax
import jax.numpy as jnp
import numpy as np
from jax import lax
from jax.experimental import pallas as pl
from jax.experimental.pallas import tpu as pltpu

F32 = jnp.float32
BF16 = jnp.bfloat16
U32 = jnp.uint32
I32 = jnp.int32

EPS = 1e-6
NEG_INF = -1e30
BLOCK = 128
HEAD_DIM = 64
GQA_GROUP = 8
POOL_WINDOWS = (2, 4, 8, 16)
N_EXPERTS = 8
N_MOD = 6
LANES = 128
PROJ_ROWS = 1024
PROJ_IN_COLS = 768
PROJ_OUT_COLS = 1024
FFN_ROWS = 512
FFN_CHUNK = 512
GATHER_CHUNK = 32
BF16_SUBLANES = 16
RIDER_BLOCK_BYTES = 4 << 20

MIB = 1 << 20


def _params(sem, vmem_mib):
    return pltpu.CompilerParams(dimension_semantics=sem, vmem_limit_bytes=vmem_mib * MIB)


def _sigmoid(x):
    return 1.0 / (1.0 + jnp.exp(-x))


def _rms(x, g):
    return x * lax.rsqrt(jnp.mean(x * x, axis=-1, keepdims=True) + EPS) * g


def _pack_pair(a, b):
    ua = pltpu.bitcast(a.astype(BF16).astype(F32), U32)
    ub = pltpu.bitcast(b.astype(BF16).astype(F32), U32)
    return ua | (ub >> 16)


def _unpack_pair_f32(p):
    return pltpu.bitcast(p & jnp.uint32(0xFFFF0000), F32), pltpu.bitcast(p << 16, F32)


def _unpack_pair(p):
    hi, lo = _unpack_pair_f32(p)
    return hi.astype(BF16), lo.astype(BF16)


def _select(w, lead):
    return w if lead is None else w[lead]


def _plan_riders(ride, grid):
    steps = grid[0] * grid[1]
    plans = []
    for w, _ in ride:
        rows, cols = w.shape[-2:]
        plan = None
        for rb in range(BF16_SUBLANES, rows + 1, BF16_SUBLANES):
            if rows % rb == 0 and rows // rb <= steps:
                plan = rb if rb * cols * 4 <= RIDER_BLOCK_BYTES else None
                break
        plans.append(plan)
    return plans


def _ridden(outs, ride, plans):
    extra = iter(outs[1:])
    return outs[0], tuple(next(extra) if rb is not None else _select(w, lead).astype(BF16)
                          for (w, lead), rb in zip(ride, plans))


def _add_riders(kernel_fn, args, in_specs, out_specs, out_shape, grid, ride, plans):
    riders = [(w, lead, rb) for (w, lead), rb in zip(ride, plans) if rb is not None]
    if not riders:
        return kernel_fn, list(args)
    n1 = grid[1]
    n_in, n_out, n_r = len(in_specs), len(out_specs), len(riders)
    for w, lead, rb in riders:
        rows, cols = w.shape[-2:]
        last = rows // rb - 1
        blk = functools.partial(lambda a, b, last: (jnp.minimum(a * n1 + b, last), 0), last=last)
        if lead is None:
            in_specs.append(pl.BlockSpec((rb, cols), blk))
        else:
            in_specs.append(pl.BlockSpec((None, rb, cols), functools.partial(
                lambda a, b, last, lead: (lead, jnp.minimum(a * n1 + b, last), 0), last=last, lead=lead)))
        out_specs.append(pl.BlockSpec((rb, cols), blk))
        out_shape.append(jax.ShapeDtypeStruct((rows, cols), BF16))

    def body(*refs):
        ins, srcs = refs[:n_in], refs[n_in:n_in + n_r]
        refs = refs[n_in + n_r:]
        outs, dsts = refs[:n_out], refs[n_out:n_out + n_r]
        kernel_fn(*ins, *outs, *refs[n_out + n_r:])
        for src, dst in zip(srcs, dsts):
            dst[...] = src[...].astype(BF16)

    return body, list(args) + [w for w, _, _ in riders]


def _ada_kernel(c_ref, w_ref, b_ref, o_ref):
    c = c_ref[...]
    ca = (c * _sigmoid(c)).astype(BF16)
    o_ref[0] = jnp.dot(ca, w_ref[0].astype(BF16), preferred_element_type=F32) + b_ref[0]


def _ada(c, w_ada, b_ada):
    depth, d, nm = w_ada.shape
    b = c.shape[0]
    tn = 512
    return pl.pallas_call(
        _ada_kernel,
        grid=(depth, nm // tn),
        in_specs=[
            pl.BlockSpec((b, d), lambda l, j: (0, 0)),
            pl.BlockSpec((1, d, tn), lambda l, j: (l, 0, j)),
            pl.BlockSpec((1, 1, tn), lambda l, j: (l, 0, j)),
        ],
        out_specs=pl.BlockSpec((1, b, tn), lambda l, j: (l, 0, j)),
        out_shape=jax.ShapeDtypeStruct((depth, b, nm), F32),
        compiler_params=_params(("parallel", "parallel"), 40),
        name="ada_mod",
    )(c, w_ada, b_ada.reshape(depth, 1, nm))


def _prenorm_kernel(x_ref, g_ref, sh_ref, sc_ref, o_ref):
    h = _rms(x_ref[...], g_ref[...]) * (1.0 + sc_ref[0]) + sh_ref[0]
    o_ref[...] = h.astype(BF16)


def _prenorm(x2, g, shift, scale, seq):
    n, d = x2.shape
    tr = min(512, seq)
    per = seq // tr
    return pl.pallas_call(
        _prenorm_kernel,
        grid=(n // tr,),
        in_specs=[
            pl.BlockSpec((tr, d), lambda i: (i, 0)),
            pl.BlockSpec((1, d), lambda i: (0, 0)),
            pl.BlockSpec((1, 1, d), lambda i: (i // per, 0, 0)),
            pl.BlockSpec((1, 1, d), lambda i: (i // per, 0, 0)),
        ],
        out_specs=pl.BlockSpec((tr, d), lambda i: (i, 0)),
        out_shape=jax.ShapeDtypeStruct((n, d), BF16),
        compiler_params=_params(("parallel",), 40),
        name="prenorm",
    )(x2, g.reshape(1, d), shift, scale)


def _mm_kernel(a_ref, w_ref, o_ref):
    o_ref[...] = jnp.dot(a_ref[...], w_ref[...], preferred_element_type=F32).astype(o_ref.dtype)


def _mm(a, w, out_dtype, tm, tn, ride=()):
    m, k = a.shape
    n = w.shape[1]
    tm = min(tm, m)
    grid = (m // tm, n // tn)
    plans = _plan_riders(ride, grid)
    in_specs = [
        pl.BlockSpec((tm, k), lambda i, j: (i, 0)),
        pl.BlockSpec((k, tn), lambda i, j: (0, j)),
    ]
    out_specs = [pl.BlockSpec((tm, tn), lambda i, j: (i, j))]
    out_shape = [jax.ShapeDtypeStruct((m, n), out_dtype)]
    body, args = _add_riders(_mm_kernel, [a, w], in_specs, out_specs, out_shape, grid, ride, plans)
    outs = pl.pallas_call(
        body,
        grid=grid,
        in_specs=in_specs,
        out_specs=out_specs,
        out_shape=out_shape,
        compiler_params=_params(("parallel", "arbitrary"), 56),
        name="proj_in",
    )(*args)
    return _ridden(outs, ride, plans)


def _mm2_kernel(a1_ref, a2_ref, w1_ref, w2_ref, o_ref):
    acc = jnp.dot(a1_ref[...], w1_ref[...], preferred_element_type=F32)
    acc = acc + jnp.dot(a2_ref[...], w2_ref[...], preferred_element_type=F32)
    o_ref[...] = acc.astype(o_ref.dtype)


def _mm2(a1, a2, w, out_dtype, tm, tn, ride=()):
    m, k1 = a1.shape
    n = w.shape[1]
    tm = min(tm, m)
    grid = (m // tm, n // tn)
    plans = _plan_riders(ride, grid)
    in_specs = [
        pl.BlockSpec((tm, k1), lambda i, j: (i, 0)),
        pl.BlockSpec((tm, k1), lambda i, j: (i, 0)),
        pl.BlockSpec((k1, tn), lambda i, j: (0, j)),
        pl.BlockSpec((k1, tn), lambda i, j: (1, j)),
    ]
    out_specs = [pl.BlockSpec((tm, tn), lambda i, j: (i, j))]
    out_shape = [jax.ShapeDtypeStruct((m, n), out_dtype)]
    body, args = _add_riders(_mm2_kernel, [a1, a2, w, w], in_specs, out_specs, out_shape, grid, ride, plans)
    outs = pl.pallas_call(
        body,
        grid=grid,
        in_specs=in_specs,
        out_specs=out_specs,
        out_shape=out_shape,
        compiler_params=_params(("parallel", "arbitrary"), 56),
        name="proj_out",
    )(*args)
    return _ridden(outs, ride, plans)


def _split3_const(x):
    parts = []
    r = np.float32(x)
    for _ in range(3):
        p = np.float32(np.asarray(r, dtype=jnp.bfloat16))
        parts.append(float(p))
        r = np.float32(r - p)
    return parts


def _swap_halves(x):
    half = x.shape[1] // 2
    return jnp.concatenate([x[:, half:], x[:, :half]], axis=1)


def _attn_tables(slopes):
    n_heads = len(slopes)
    qx = np.zeros((n_heads, BLOCK, LANES), np.float32)
    for h, s in enumerate(slopes):
        qx[h, :, HEAD_DIM:HEAD_DIM + 3] = _split3_const(s)
    kx = np.zeros((2 * BLOCK, LANES), np.float32)
    kx[:, HEAD_DIM:HEAD_DIM + 3] = np.arange(2 * BLOCK, dtype=np.float32)[:, None]
    return jnp.asarray(qx, BF16), jnp.asarray(kx, BF16)


def _attn_kernel(slopes, sinks_ref, qx_ref, kx_ref, q_ref, kp_ref, kc_ref, vp_ref, vc_ref, o_ref):
    n = pl.program_id(1)
    n_heads = len(slopes)
    k = jnp.concatenate([kp_ref[...], kc_ref[...]], axis=0)
    v = jnp.concatenate([vp_ref[...], vc_ref[...]], axis=0)
    c_idx = lax.broadcasted_iota(I32, (BLOCK, BLOCK), 0)
    i_idx = lax.broadcasted_iota(I32, (BLOCK, BLOCK), 1)
    from_prev = c_idx > i_idx
    prev_bias = jnp.where(n > 0, 0.0, NEG_INF)
    qpos = (lax.broadcasted_iota(I32, (1, BLOCK), 1) + BLOCK).astype(F32)
    qlane = lax.broadcasted_iota(I32, (BLOCK, LANES), 1) < HEAD_DIM
    klane = lax.broadcasted_iota(I32, (2 * BLOCK, LANES), 1) < HEAD_DIM
    scale = jnp.asarray(HEAD_DIM ** -0.5, BF16)
    outs = []
    for kv in range(n_heads // GQA_GROUP):
        heads = range(kv * GQA_GROUP, (kv + 1) * GQA_GROUP)
        kt = k[:, (kv // 2) * LANES:(kv // 2 + 1) * LANES]
        vt = v[:, (kv // 2) * LANES:(kv // 2 + 1) * LANES]
        if kv % 2:
            kt, vt = _swap_halves(kt), _swap_halves(vt)
        k_aug = jnp.where(klane, kt, kx_ref[...])
        pieces = []
        for h in heads:
            qt = q_ref[:, (h // 2) * LANES:(h // 2 + 1) * LANES]
            if h % 2:
                qt = _swap_halves(qt)
            pieces.append(jnp.where(qlane, qt * scale, qx_ref[h]))
        qg = jnp.concatenate(pieces, axis=0)
        st = lax.dot_general(k_aug, qg, (((1,), (1,)), ((), ())), preferred_element_type=F32)
        pts = []
        for g, h in enumerate(heads):
            s_prev = st[:BLOCK, g * BLOCK:(g + 1) * BLOCK] + prev_bias
            s_cur = st[BLOCK:, g * BLOCK:(g + 1) * BLOCK]
            s = jnp.where(from_prev, s_prev, s_cur)
            sink = sinks_ref[h] + np.float32(slopes[h]) * qpos
            m = jnp.maximum(jnp.max(s, axis=0, keepdims=True), sink)
            e = jnp.exp(s - m)
            denom = jnp.sum(e, axis=0, keepdims=True) + jnp.exp(sink - m)
            p = e * (1.0 / denom)
            pts.append(jnp.concatenate([jnp.where(from_prev, p, 0.0).astype(BF16),
                                        jnp.where(from_prev, 0.0, p).astype(BF16)], axis=0))
        pt = jnp.concatenate(pts, axis=1)
        og = lax.dot_general(pt, vt, (((0,), (0,)), ((), ())), preferred_element_type=F32)
        for g in range(GQA_GROUP):
            outs.append(og[g * BLOCK:(g + 1) * BLOCK, :HEAD_DIM].astype(BF16))
    o_ref[...] = jnp.concatenate(outs, axis=-1)


def _attention(proj, sinks, batch, seq, d_attn, d_kv, ride=()):
    n = proj.shape[0]
    nb = seq // BLOCK
    n_heads = d_attn // HEAD_DIM
    slopes = tuple(2.0 ** (-8.0 * (h + 1) / n_heads) for h in range(n_heads))
    kcol = d_attn // d_kv
    vcol = kcol + 1

    def cur(col):
        return lambda b, i: (b * nb + i, col)

    def prev(col):
        return lambda b, i: (b * nb + jnp.maximum(i - 1, 0), col)

    grid = (batch, nb)
    plans = _plan_riders(ride, grid)
    qx, kx = _attn_tables(slopes)
    in_specs = [
        pl.BlockSpec(memory_space=pltpu.SMEM),
        pl.BlockSpec(qx.shape, lambda b, i: (0, 0, 0)),
        pl.BlockSpec(kx.shape, lambda b, i: (0, 0)),
        pl.BlockSpec((BLOCK, d_attn), cur(0)),
        pl.BlockSpec((BLOCK, d_kv), prev(kcol)),
        pl.BlockSpec((BLOCK, d_kv), cur(kcol)),
        pl.BlockSpec((BLOCK, d_kv), prev(vcol)),
        pl.BlockSpec((BLOCK, d_kv), cur(vcol)),
    ]
    out_specs = [pl.BlockSpec((BLOCK, d_attn), cur(0))]
    out_shape = [jax.ShapeDtypeStruct((n, d_attn), BF16)]
    body, args = _add_riders(functools.partial(_attn_kernel, slopes), [sinks, qx, kx, proj, proj, proj, proj, proj],
                             in_specs, out_specs, out_shape, grid, ride, plans)
    outs = pl.pallas_call(
        body,
        grid=grid,
        in_specs=in_specs,
        out_specs=out_specs,
        out_shape=out_shape,
        compiler_params=_params(("parallel", "arbitrary"), 40),
        name="swa_attention",
    )(*args)
    return _ridden(outs, ride, plans)


def _pool_kernel(u_ref, w_ref, ps_ref, o_ref):
    g = pl.program_id(1)
    u = u_ref[...].astype(F32)
    row = lax.broadcasted_iota(I32, u.shape, 0)

    def shifted(x, k):
        return jnp.where(row >= k, pltpu.roll(x, k, 0), 0.0)

    s2 = u + shifted(u, 1)
    s4 = s2 + shifted(s2, 2)
    s8 = s4 + shifted(s4, 4)
    s16 = s8 + shifted(s8, 8)
    wsum = jnp.where(g == 0, s2, jnp.where(g == 1, s4, jnp.where(g == 2, s8, s16)))
    win = jnp.left_shift(jnp.int32(POOL_WINDOWS[0]), g)
    cnt = jnp.minimum(row + 1, win).astype(F32)
    pooled = wsum / cnt - u
    y = jnp.dot(pooled.astype(BF16), w_ref[0], preferred_element_type=F32)
    o_ref[...] = (y * ps_ref[...]).astype(BF16)


def _pool(proj, w_pool, pool_scale, batch, seq, u_col0):
    n = proj.shape[0]
    ng, c, _ = w_pool.shape
    assert POOL_WINDOWS == tuple(POOL_WINDOWS[0] << g for g in range(ng))
    col0 = u_col0 // c
    return pl.pallas_call(
        _pool_kernel,
        grid=(batch, ng),
        in_specs=[
            pl.BlockSpec((seq, c), lambda b, g: (b, col0 + g)),
            pl.BlockSpec((1, c, c), lambda b, g: (g, 0, 0)),
            pl.BlockSpec((1, c), lambda b, g: (0, g)),
        ],
        out_specs=pl.BlockSpec((seq, c), lambda b, g: (b, g)),
        out_shape=jax.ShapeDtypeStruct((n, ng * c), BF16),
        compiler_params=_params(("parallel", "arbitrary"), 48),
        name="pool_mixer",
    )(proj, w_pool, pool_scale.reshape(1, ng * c))


def _top2_route(logits):
    lane = lax.broadcasted_iota(I32, logits.shape, 1).astype(F32)
    m1 = jnp.max(logits, axis=-1, keepdims=True)
    i1 = jnp.min(jnp.where(logits == m1, lane, float(LANES)), axis=-1, keepdims=True)
    l2 = jnp.where(lane == i1, -jnp.inf, logits)
    m2 = jnp.max(l2, axis=-1, keepdims=True)
    i2 = jnp.min(jnp.where(l2 == m2, lane, float(LANES)), axis=-1, keepdims=True)
    e = jnp.exp(m2 - m1)
    w1 = 1.0 / (1.0 + e)
    w2 = e / (1.0 + e)
    return jnp.where(lane == 0.0, i1, jnp.where(lane == 1.0, i2, jnp.where(lane == 2.0, w1, jnp.where(lane == 3.0, w2, 0.0))))


def _post_kernel(mode, gathered, *refs):
    refs = list(refs)
    if gathered:
        pos_ref, pos_next_ref, y_hbm, route_in = refs[:4]
        refs = refs[4:]
    else:
        f_ref = refs.pop(0)
    x_ref, gate_ref, gpost_ref = refs[:3]
    refs = refs[3:]
    if mode != "last":
        gpre_ref, sh_ref, sc_ref = refs[:3]
        refs = refs[3:]
    if mode == "route":
        rw_ref, rb_ref = refs[:2]
        refs = refs[2:]
    xo_ref = refs.pop(0)
    if mode != "last":
        ho_ref = refs.pop(0)
    if mode == "route":
        ro_ref = refs.pop(0)

    def finish(f, rows):
        xn = x_ref[rows, :] + gate_ref[0] * _rms(f, gpost_ref[...])
        xo_ref[rows, :] = xn
        if mode == "last":
            return
        h = _rms(xn, gpre_ref[...]) * (1.0 + sc_ref[0]) + sh_ref[0]
        if mode == "next":
            ho_ref[rows, :] = h.astype(BF16)
            return
        half = h.shape[1] // 2
        ho_ref[rows, :] = _pack_pair(h[:, :half], h[:, half:])
        logits = jnp.dot(h.astype(BF16), rw_ref[...], preferred_element_type=F32) + rb_ref[...]
        ro_ref[rows, :] = _top2_route(logits)

    if not gathered:
        finish(f_ref[...].astype(F32), slice(None))
        return

    assert mode != "route"
    ybuf, sems = refs
    tr = x_ref.shape[0]
    i = pl.program_id(0)
    slot = i % 2

    def fetch(idx_ref, s, r):
        for kk in range(2):
            pltpu.make_async_copy(y_hbm.at[pl.ds(idx_ref[0, 0, 2 * r + kk], 1)],
                                  ybuf.at[s, kk, pl.ds(r, 1)], sems.at[s]).start()

    @pl.when(i == 0)
    def _():
        lax.fori_loop(0, tr, lambda r, c: (fetch(pos_ref, 0, r), c)[1], 0, unroll=8)

    pltpu.make_async_copy(ybuf.at[slot], ybuf.at[slot], sems.at[slot]).wait()
    for r0 in range(0, tr, GATHER_CHUNK):
        for r in range(r0, r0 + GATHER_CHUNK):
            fetch(pos_next_ref, 1 - slot, r)
        rows = slice(r0, r0 + GATHER_CHUNK)
        rt = route_in[rows, :]
        a_hi, a_lo = _unpack_pair_f32(ybuf[slot, 0, rows, :])
        b_hi, b_lo = _unpack_pair_f32(ybuf[slot, 1, rows, :])
        w_a, w_b = rt[:, 2:3], rt[:, 3:4]
        finish(jnp.concatenate([a_hi * w_a + b_hi * w_b, a_lo * w_a + b_lo * w_b], axis=1), rows)

    @pl.when(i + 1 == pl.num_programs(0))
    def _():
        pltpu.make_async_copy(ybuf.at[1 - slot], ybuf.at[1 - slot], sems.at[1 - slot]).wait()


def _post(mode, f, x2, gate, g_post, seq, nxt=None, router=None, gather=None):
    n, d = x2.shape
    tr = min(256, seq)
    per = seq // tr
    row = lambda i: (i, 0)
    fix = lambda i: (0, 0)
    bat = lambda i: (i // per, 0, 0)
    in_specs, args, scratch = [], [], []
    if gather is not None:
        pos, y, route = gather
        last = n // tr - 1
        pos3 = pos.reshape(n // tr, 1, 2 * tr)
        in_specs += [
            pl.BlockSpec((1, 1, 2 * tr), lambda i: (i, 0, 0), memory_space=pltpu.SMEM),
            pl.BlockSpec((1, 1, 2 * tr), lambda i: (jnp.minimum(i + 1, last), 0, 0), memory_space=pltpu.SMEM),
            pl.BlockSpec(memory_space=pl.ANY),
            pl.BlockSpec((tr, LANES), row),
        ]
        args += [pos3, pos3, y, route]
        scratch = [pltpu.VMEM((2, 2, tr, d // 2), U32), pltpu.SemaphoreType.DMA((2,))]
    else:
        in_specs.append(pl.BlockSpec((tr, d), row))
        args.append(f)
    in_specs += [pl.BlockSpec((tr, d), row), pl.BlockSpec((1, 1, d), bat), pl.BlockSpec((1, d), fix)]
    args += [x2, gate, g_post.reshape(1, d)]
    out_specs = [pl.BlockSpec((tr, d), row)]
    out_shape = [jax.ShapeDtypeStruct((n, d), F32)]
    if mode != "last":
        g_pre, shift, scale = nxt
        in_specs += [pl.BlockSpec((1, d), fix), pl.BlockSpec((1, 1, d), bat), pl.BlockSpec((1, 1, d), bat)]
        args += [g_pre.reshape(1, d), shift, scale]
    if mode == "next":
        out_specs.append(pl.BlockSpec((tr, d), row))
        out_shape.append(jax.ShapeDtypeStruct((n, d), BF16))
    if mode == "route":
        rw, rb = router
        in_specs += [pl.BlockSpec((d, LANES), fix), pl.BlockSpec((1, LANES), fix)]
        args += [rw, rb]
        out_specs += [pl.BlockSpec((tr, d // 2), row), pl.BlockSpec((tr, LANES), row)]
        out_shape += [jax.ShapeDtypeStruct((n, d // 2), U32), jax.ShapeDtypeStruct((n, LANES), F32)]
    return pl.pallas_call(
        functools.partial(_post_kernel, mode, gather is not None),
        grid=(n // tr,),
        in_specs=in_specs,
        out_specs=out_specs,
        out_shape=out_shape,
        scratch_shapes=scratch,
        compiler_params=_params(("arbitrary",), 56),
        name="post_" + mode + ("_gather" if gather is not None else ""),
    )(*args)


def _ffn_loop_kernel(dff, tf, n_ride, rb, h_ref, wg_hbm, wu_hbm, wd_hbm, *rest):
    src = rest[:n_ride]
    o_hbm = rest[n_ride]
    dst = rest[n_ride + 1:2 * n_ride + 1]
    acc, ostage, wgb, wub, wdb, sems, osem = rest[2 * n_ride + 1:2 * n_ride + 8]
    i = pl.program_id(0)
    n_full = dff // tf
    tail = dff - n_full * tf
    if n_ride:
        rin, rout, rsems = rest[2 * n_ride + 8:]
        per = src[0].shape[0] // rb
        n_blk = n_ride * per

        def rows_of(g, k):
            return pl.ds(pl.multiple_of((g - k * per) * rb, rb), rb)

        def ride_in(g, k):
            return pltpu.make_async_copy(src[k].at[rows_of(g, k)], rin.at[g % 2], rsems.at[0, g % 2])

        def ride_out(g, k):
            return pltpu.make_async_copy(rout.at[g % 2], dst[k].at[rows_of(g, k)], rsems.at[1, g % 2])

        def for_block(g, fn):
            for k in range(n_ride):
                @pl.when((g >= k * per) & (g < (k + 1) * per))
                def _():
                    fn(g, k)

    def copies(c, slot, width):
        start = pl.multiple_of(c * tf, LANES)
        return (pltpu.make_async_copy(wg_hbm.at[:, pl.ds(start, width)], wgb.at[slot, :, pl.ds(0, width)],
                                      sems.at[slot, 0]),
                pltpu.make_async_copy(wu_hbm.at[:, pl.ds(start, width)], wub.at[slot, :, pl.ds(0, width)],
                                      sems.at[slot, 1]),
                pltpu.make_async_copy(wd_hbm.at[pl.ds(start, width), :], wdb.at[slot, pl.ds(0, width), :],
                                      sems.at[slot, 2]))

    def start(c, slot, width=tf):
        for cp in copies(c, slot, width):
            cp.start()

    def wait(c, slot, width=tf):
        for cp in copies(c, slot, width):
            cp.wait()

    def multiply(slot, width=tf):
        h = h_ref[...]
        g = jnp.dot(h, wgb[slot, :, :width], preferred_element_type=F32)
        u = jnp.dot(h, wub[slot, :, :width], preferred_element_type=F32)
        a = (g * _sigmoid(g) * u).astype(BF16)
        cw = acc.shape[1] // 4
        for q in range(4):
            acc[:, q * cw:(q + 1) * cw] += jnp.dot(a, wdb[slot, :width, q * cw:(q + 1) * cw],
                                                   preferred_element_type=F32)

    @pl.when(i == 0)
    def _():
        start(0, 0)
        if n_ride:
            ride_in(0, 0).start()

    acc[...] = jnp.zeros_like(acc)

    def step(c, carry):
        slot = c % 2
        wait(c, slot)
        if n_ride:
            g = i * (n_full - 1) + c
            for_block(g, lambda g, k: ride_in(g, k).wait())
            for_block(g + 1, lambda g, k: ride_in(g, k).start())
            for_block(g - 2, lambda g, k: ride_out(g, k).wait())
        start(c + 1, 1 - slot)
        multiply(slot)
        if n_ride:
            rout[g % 2] = rin[g % 2].astype(BF16)
            for_block(g, lambda g, k: ride_out(g, k).start())
        return carry

    lax.fori_loop(0, n_full - 1, step, 0)
    last, last_slot = n_full - 1, (n_full - 1) % 2
    wait(last, last_slot)
    if tail:
        start(n_full, 1 - last_slot, tail)
    else:
        @pl.when(i + 1 < pl.num_programs(0))
        def _():
            start(0, 1 - last_slot)
    multiply(last_slot)
    if tail:
        wait(n_full, 1 - last_slot, tail)

        @pl.when(i + 1 < pl.num_programs(0))
        def _():
            start(0, 0)
        multiply(1 - last_slot, tail)

    tm = acc.shape[0]

    def store(tile):
        return pltpu.make_async_copy(ostage, o_hbm.at[pl.ds(pl.multiple_of(tile * tm, tm), tm)], osem)

    @pl.when(i > 0)
    def _():
        store(i - 1).wait()

    ostage[...] = acc[...].astype(ostage.dtype)
    store(i).start()

    @pl.when(i + 1 == pl.num_programs(0))
    def _():
        store(i).wait()


def _ffn_loop(h, wg, wu, wd, tm, tf, ride=()):
    n, d = h.shape
    dff = wg.shape[1]
    tm = min(tm, n)
    n_tiles = n // tm
    n_full = dff // tf
    assert n_full >= 2 and (n_full + (1 if dff % tf else 0)) % 2 == 0 and (dff % tf) % LANES == 0
    ride_w = [w for w, _ in ride]
    rb = None
    if ride_w:
        rows, cols = ride_w[0].shape
        assert all(w.shape == (rows, cols) for w in ride_w)
        hosts = n_tiles * (n_full - 1) - 2
        for cand in range(BF16_SUBLANES, rows + 1, BF16_SUBLANES):
            if rows % cand == 0 and len(ride_w) * (rows // cand) <= hosts:
                rb = cand if cand * cols * 12 <= 2 * RIDER_BLOCK_BYTES else None
                break
    if rb is None:
        ride_w = []
    n_ride = len(ride_w)
    any_spec = pl.BlockSpec(memory_space=pl.ANY)
    scratch = [pltpu.VMEM((tm, d), F32), pltpu.VMEM((tm, d), BF16), pltpu.VMEM((2, d, tf), BF16),
               pltpu.VMEM((2, d, tf), BF16), pltpu.VMEM((2, tf, d), BF16), pltpu.SemaphoreType.DMA((2, 3)),
               pltpu.SemaphoreType.DMA(())]
    out_specs = [any_spec]
    out_shape = [jax.ShapeDtypeStruct((n, d), BF16)]
    if n_ride:
        scratch += [pltpu.VMEM((2, rb, cols), F32), pltpu.VMEM((2, rb, cols), BF16), pltpu.SemaphoreType.DMA((2, 2))]
        out_specs += [any_spec] * n_ride
        out_shape += [jax.ShapeDtypeStruct((rows, cols), BF16)] * n_ride
    outs = pl.pallas_call(
        functools.partial(_ffn_loop_kernel, dff, tf, n_ride, rb),
        grid=(n_tiles,),
        in_specs=[pl.BlockSpec((tm, d), lambda i: (i, 0))] + [any_spec] * (3 + n_ride),
        out_specs=out_specs,
        out_shape=out_shape,
        scratch_shapes=scratch,
        compiler_params=_params(("arbitrary",), 58),
        name="dense_swiglu",
    )(h, wg, wu, wd, *ride_w)
    casts = tuple(outs[1:]) if n_ride else tuple(_select(w, lead).astype(BF16) for w, lead in ride)
    return outs[0], casts


def _expert_kernel(te_ref, nt_ref, rows_ref, idx_ref, idx_next_ref, src_hbm, wg_hbm, wu_hbm, wd_hbm, y_ref,
                   xs_buf, h_scr, acc, wgb, wub, wdb, sem, wsems):
    t = pl.program_id(0)
    n_live = nt_ref[0]
    live = t < n_live
    tm, half = xs_buf.shape
    tf = wgb.shape[2]
    n_chunks = wg_hbm.shape[2] // tf
    top = tm // 2
    mostly_padding = rows_ref[t] <= top

    def gather(iref):
        def start(r, carry):
            pltpu.make_async_copy(src_hbm.at[pl.ds(iref[0, 0, r], 1)], xs_buf.at[pl.ds(r, 1)], sem).start()
            return carry
        lax.fori_loop(0, tm, start, 0, unroll=8)

    def copies(e, c, slot):
        start = pl.multiple_of(c * tf, LANES)
        return (pltpu.make_async_copy(wg_hbm.at[e, :, pl.ds(start, tf)], wgb.at[slot], wsems.at[slot, 0]),
                pltpu.make_async_copy(wu_hbm.at[e, :, pl.ds(start, tf)], wub.at[slot], wsems.at[slot, 1]),
                pltpu.make_async_copy(wd_hbm.at[e, pl.ds(start, tf), :], wdb.at[slot], wsems.at[slot, 2]))

    def start(e, c, slot):
        for cp in copies(e, c, slot):
            cp.start()

    def wait(e, c, slot):
        for cp in copies(e, c, slot):
            cp.wait()

    def multiply(slot, rows):
        h = h_scr[:rows]
        g = jnp.dot(h, wgb[slot], preferred_element_type=F32)
        u = jnp.dot(h, wub[slot], preferred_element_type=F32)
        a = (g * _sigmoid(g) * u).astype(BF16)
        cw = acc.shape[1] // 4
        for q in range(4):
            acc[:rows, q * cw:(q + 1) * cw] += jnp.dot(a, wdb[slot, :, q * cw:(q + 1) * cw],
                                                       preferred_element_type=F32)

    @pl.when(t == 0)
    def _():
        gather(idx_ref)
        start(te_ref[0], 0, 0)

    @pl.when(live)
    def _():
        e = te_ref[t]
        pltpu.make_async_copy(xs_buf, xs_buf, sem).wait()
        hi, lo = _unpack_pair(xs_buf[...])
        h_scr[:, :half] = hi
        h_scr[:, half:] = lo
        acc[...] = jnp.zeros_like(acc)

        per_chunk = tm // n_chunks

        def fetch_share(c):
            for u in range(per_chunk):
                r = c * per_chunk + u
                pltpu.make_async_copy(src_hbm.at[pl.ds(idx_next_ref[0, 0, r], 1)], xs_buf.at[pl.ds(r, 1)], sem).start()

        def run(rows):
            def step(c, carry):
                slot = c % 2
                wait(e, c, slot)
                start(e, c + 1, 1 - slot)
                fetch_share(c)
                multiply(slot, rows)
                return carry

            lax.fori_loop(0, n_chunks - 1, step, 0)
            last_slot = (n_chunks - 1) % 2
            wait(e, n_chunks - 1, last_slot)

            @pl.when(t + 1 < n_live)
            def _():
                start(te_ref[jnp.minimum(t + 1, pl.num_programs(0) - 1)], 0, 1 - last_slot)
            fetch_share(n_chunks - 1)
            multiply(last_slot, rows)

        @pl.when(jnp.logical_not(mostly_padding))
        def _():
            run(tm)

        @pl.when(mostly_padding)
        def _():
            run(top)

        y_ref[...] = _pack_pair(acc[:, :half], acc[:, half:])

        @pl.when(t + 1 >= n_live)
        def _():
            pltpu.make_async_copy(xs_buf, xs_buf, sem).wait()

    @pl.when(jnp.logical_not(live))
    def _():
        y_ref[...] = jnp.zeros_like(y_ref)


def _experts(src, src_tok, tile_expert, n_tiles, tile_rows, wg, wu, wd, tm, tf):
    half = src.shape[1]
    d = 2 * half
    rows = src_tok.shape[0]
    dff = wg.shape[2]
    n_t = rows // tm
    idx3 = src_tok.reshape(n_t, 1, tm)
    assert dff % tf == 0 and (dff // tf) % 2 == 0
    return pl.pallas_call(
        _expert_kernel,
        grid_spec=pltpu.PrefetchScalarGridSpec(
            num_scalar_prefetch=3,
            grid=(n_t,),
            in_specs=[
                pl.BlockSpec((1, 1, tm), lambda t, te, nt, tr: (t, 0, 0), memory_space=pltpu.SMEM),
                pl.BlockSpec((1, 1, tm), lambda t, te, nt, tr: (jnp.minimum(t + 1, n_t - 1), 0, 0),
                             memory_space=pltpu.SMEM),
                pl.BlockSpec(memory_space=pl.ANY),
                pl.BlockSpec(memory_space=pl.ANY),
                pl.BlockSpec(memory_space=pl.ANY),
                pl.BlockSpec(memory_space=pl.ANY),
            ],
            out_specs=pl.BlockSpec((tm, half), lambda t, te, nt, tr: (t, 0)),
            scratch_shapes=[pltpu.VMEM((tm, half), U32), pltpu.VMEM((tm, d), BF16), pltpu.VMEM((tm, d), F32),
                            pltpu.VMEM((2, d, tf), BF16), pltpu.VMEM((2, d, tf), BF16), pltpu.VMEM((2, tf, d), BF16),
                            pltpu.SemaphoreType.DMA(()), pltpu.SemaphoreType.DMA((2, 3))],
        ),
        out_shape=jax.ShapeDtypeStruct((rows, half), U32),
        compiler_params=_params(("arbitrary",), 58),
        name="expert_swiglu",
    )(tile_expert, n_tiles, tile_rows, idx3, idx3, src, wg, wu, wd)


def _moe_plan(top_idx, tm):
    n = top_idx.shape[0]
    p = 2 * n
    flat = top_idx.reshape(p)
    experts = jnp.arange(N_EXPERTS, dtype=I32)
    onehot = (flat[:, None] == experts[None, :]).astype(I32)
    csum = jnp.cumsum(onehot, axis=0)
    rank = jnp.sum(onehot * csum, axis=1) - 1
    counts = csum[-1]
    tiles_per = (counts + (tm - 1)) // tm
    tile_end = jnp.cumsum(tiles_per)
    tile_start = tile_end - tiles_per
    pos = (tile_start * tm)[flat] + rank
    n_tiles = tile_end[-1:]
    t_max = p // tm + N_EXPERTS
    t_ids = jnp.arange(t_max, dtype=I32)
    te = jnp.sum((t_ids[:, None] >= tile_end[None, :]).astype(I32), axis=1)
    last_e = jnp.max(jnp.where(tiles_per > 0, experts, 0))
    te = jnp.minimum(te, last_e)
    tile_rows = jnp.clip(counts[te] - (t_ids - tile_start[te]) * tm, 0, tm)
    src_tok = jnp.zeros((t_max * tm,), I32).at[pos].set(jnp.arange(p, dtype=I32) // 2)
    return pos.astype(I32), src_tok, te.astype(I32), n_tiles.astype(I32), tile_rows.astype(I32)


def kernel(x, c, w_ada, b_ada, norm_pre_mix, norm_post_mix, norm_pre_ffn, norm_post_ffn, w_in, sinks, w_pool,
           pool_scale, w_out, ffn_w_gate, ffn_w_up, ffn_w_down, router_w, router_b, moe_w_gate, moe_w_up, moe_w_down):
    batch, seq, d = x.shape
    depth = w_ada.shape[0]
    n = batch * seq
    d_attn = d // 2
    n_heads = sinks.shape[1]
    d_kv = (n_heads // GQA_GROUP) * HEAD_DIM
    u_col0 = d_attn + 2 * d_kv
    assert n_heads * HEAD_DIM == d_attn and seq % BLOCK == 0

    x2 = x.reshape(n, d)
    mod = _ada(c, w_ada, b_ada)

    def mods(l):
        return [mod[l, :, k * d:(k + 1) * d].reshape(batch, 1, d) for k in range(N_MOD)]

    shift1, scale1, gate1, shift2, scale2, gate2 = mods(0)
    h = _prenorm(x2, norm_pre_mix[0], shift1, scale1, seq)
    flat = lambda w: w.reshape(-1, w.shape[-1])
    ready = {}

    def hosted(host, rides, *a):
        out, casts = host(*a, ride=[(w, lead) for _, w, lead in rides])
        ready.update({key: v for (key, _, _), v in zip(rides, casts)})
        return out

    def take(key, w, lead=None):
        shape = w.shape if lead is None else w.shape[1:]
        return ready.pop(key).reshape(shape) if key in ready else _select(w, lead).astype(BF16)

    for l in range(depth):
        i = l // 2
        dense = l % 2 == 0
        nm = (l + 1) // 2 if (dense and l + 1 < depth) else None
        on_proj_in, on_attn, on_proj_out, on_ffn = [(("w_out", l), w_out, l)], [], [], []
        if dense:
            on_proj_in += [(("ffn_wg", i), ffn_w_gate, i), (("ffn_wd", i), ffn_w_down, i)]
            on_proj_out.append((("ffn_wu", i), ffn_w_up, i))
            if nm is not None:
                on_attn.append((("moe_wg", nm), flat(moe_w_gate[nm]), None))
                on_ffn.append((("moe_wd", nm), flat(moe_w_down[nm]), None))
            if l + 1 < depth:
                on_proj_out.append((("w_in", l + 1), w_in, l + 1))
        else:
            on_attn.append((("moe_wu", i), flat(moe_w_up[i]), None))

        proj = hosted(_mm, on_proj_in, h, take(("w_in", l), w_in, l), BF16, PROJ_ROWS, PROJ_IN_COLS)
        attn = hosted(_attention, on_attn, proj, sinks[l], batch, seq, d_attn, d_kv)
        pool = _pool(proj, w_pool[l].astype(BF16), pool_scale[l], batch, seq, u_col0)
        mix = hosted(_mm2, on_proj_out, attn, pool, take(("w_out", l), w_out, l), BF16, PROJ_ROWS, PROJ_OUT_COLS)
        nxt = (norm_pre_ffn[l], shift2, scale2)
        if dense:
            x2, h2 = _post("next", mix, x2, gate1, norm_post_mix[l], seq, nxt=nxt)
            f = hosted(_ffn_loop, on_ffn, h2, take(("ffn_wg", i), ffn_w_gate, i), take(("ffn_wu", i), ffn_w_up, i),
                       take(("ffn_wd", i), ffn_w_down, i), FFN_ROWS, FFN_CHUNK)
            gather = None
        else:
            rw = jnp.zeros((d, LANES), BF16).at[:, :N_EXPERTS].set(router_w[i].astype(BF16))
            rb = jnp.full((1, LANES), NEG_INF, F32).at[0, :N_EXPERTS].set(router_b[i])
            x2, h2p, route = _post("route", mix, x2, gate1, norm_post_mix[l], seq, nxt=nxt, router=(rw, rb))
            pos, src_tok, tile_expert, n_tiles, tile_rows = _moe_plan(route[:, :2].astype(I32), FFN_ROWS)
            y = _experts(h2p, src_tok, tile_expert, n_tiles, tile_rows, take(("moe_wg", i), moe_w_gate[i]),
                         take(("moe_wu", i), moe_w_up[i]), take(("moe_wd", i), moe_w_down[i]), FFN_ROWS, FFN_CHUNK)
            f, gather = None, (pos, y, route)
        if l + 1 < depth:
            shift1, scale1, gate1n, shift2n, scale2n, gate2n = mods(l + 1)
            x2, h = _post("next", f, x2, gate2, norm_post_ffn[l], seq,
                          nxt=(norm_pre_mix[l + 1], shift1, scale1), gather=gather)
            gate1, shift2, scale2, gate2 = gate1n, shift2n, scale2n, gate2n
        else:
            (x2,) = _post("last", f, x2, gate2, norm_post_ffn[l], seq, gather=gather)
    return x2.reshape(batch, seq, d)
```
